```python
import math
import jax, jax.numpy as jnp
from jax import lax
import numpy as np

D_MODEL = 2048
BATCH = 2
SEQ = 4096
DEPTH = 1
DEC_BATCH = 32
DEC_SEQ = 1
PAST_LEN = 16384
PAGE_SIZE = 128

SB_HEAD_DIM = 64
SB_HEADS = D_MODEL // (2 * SB_HEAD_DIM)
SB_WIDTH = SB_HEADS * SB_HEAD_DIM
SB_BLOCK = 128
SB_BIAS_INIT = -6.0
GDN_HEAD_DIM = 128
GDN_HEADS = D_MODEL // (2 * GDN_HEAD_DIM)
GDN_WIDTH = GDN_HEADS * GDN_HEAD_DIM
GDN_CONV_CH = 3 * GDN_WIDTH
CONV_WIDTH = 4
GDN_CHUNK = 64
N_GROUPS = 4
EXPERTS_PER_GROUP = 8
N_EXPERTS = N_GROUPS * EXPERTS_PER_GROUP
TOP_K_IN_GROUP = 2
EXPERT_FF = D_MODEL // 4
DEEPNORM_ALPHA = (2.0 * DEPTH) ** 0.25
DEEPNORM_BETA = (8.0 * DEPTH) ** -0.25
LN_EPS = 1e-5
RMS_EPS = 1e-6
PROJ_DIM = 3 * SB_WIDTH + GDN_CONV_CH + GDN_WIDTH + 2 * GDN_HEADS + 2 * D_MODEL

kernel_name = "hybrid_stickbreak_gdn_hmoe_step"


def layer_norm(x, g, b):
    xf = x.astype(jnp.float32)
    mu = jnp.mean(xf, axis=-1, keepdims=True)
    var = jnp.mean(jnp.square(xf - mu), axis=-1, keepdims=True)
    return ((xf - mu) * lax.rsqrt(var + LN_EPS) * g + b).astype(x.dtype)


def l2norm(t):
    return t * lax.rsqrt(jnp.sum(t * t, axis=-1, keepdims=True) + RMS_EPS)


def split_projection(z):
    bounds = np.cumsum([3 * SB_WIDTH, GDN_CONV_CH, GDN_WIDTH, GDN_HEADS, GDN_HEADS, D_MODEL]).tolist()
    return jnp.split(z, bounds, axis=-1)


def sb_block(q, k, v, bias, mask, log_surv, acc):
    z = jnp.einsum('bhqd,bkhd->bhqk', q, k.astype(jnp.float32)) + bias.astype(jnp.float32)[None, :, None, None]
    neg_log_1m = jnp.where(mask, jax.nn.softplus(z), 0.0)
    later = lax.cumsum(neg_log_1m, axis=3, reverse=True) - neg_log_1m + log_surv[..., None]
    w = jnp.where(mask, jnp.exp(jax.nn.log_sigmoid(z) - later), 0.0)
    acc = acc + jnp.einsum('bhqk,bkhd->bhqd', w, v.astype(jnp.float32))
    return log_surv + jnp.sum(neg_log_1m, axis=3), acc


def sb_prompt(q, k, v, bias):
    B, S, H, d = q.shape
    nb = S // SB_BLOCK
    qb = (q.astype(jnp.float32) * d ** -0.5).reshape(B, nb, SB_BLOCK, H, d).transpose(1, 0, 3, 2, 4)
    k_pos = jnp.arange(S)

    def one_block(args):
        q_blk, start = args
        mask = k_pos[None, :] < (start + jnp.arange(SB_BLOCK))[:, None]
        _, acc = sb_block(q_blk, k, v, bias, mask, jnp.zeros(q_blk.shape[:3], jnp.float32), jnp.zeros_like(q_blk))
        return acc

    out = lax.map(one_block, (qb, jnp.arange(nb) * SB_BLOCK))
    return out.transpose(1, 0, 3, 2, 4).reshape(B, S, H, d)


def sb_sample(q, k_new, v_new, bias, cache_k, cache_v, layer, page_table):
    B, T, H, d = q.shape
    qh = (q.astype(jnp.float32) * d ** -0.5).transpose(0, 2, 1, 3)
    mask_new = jnp.arange(T)[None, :] < jnp.arange(T)[:, None]
    carry = sb_block(qh, k_new, v_new, bias, mask_new, jnp.zeros((B, H, T), jnp.float32),
                     jnp.zeros((B, H, T, d), jnp.float32))
    mask_past = jnp.ones((T, cache_k.shape[2]), bool)

    def page_step(carry, pages):
        return sb_block(qh, cache_k[layer, pages], cache_v[layer, pages], bias, mask_past, *carry), None

    (_, acc), _ = lax.scan(page_step, carry, page_table.T[::-1])
    return acc.transpose(0, 2, 1, 3)


def causal_conv_silu(ext, w):
    L = ext.shape[1] - (CONV_WIDTH - 1)
    acc = ext[:, 0:L] * w[0]
    for i in range(1, CONV_WIDTH):
        acc = acc + ext[:, i:i + L] * w[i]
    return jax.nn.silu(acc.astype(jnp.float32))


def gated_delta_chunked(q, k, v, beta, g, s0, chunk):
    B, L, H, _ = q.shape
    DV = v.shape[-1]
    n = L // chunk

    def blocks(t):
        t = t.reshape((B, n, chunk, H) + t.shape[3:])
        return jnp.moveaxis(t, (1, 3), (0, 2))

    q, k, v, beta, g = (blocks(t) for t in (q, k, v, beta, g))
    gc = jnp.cumsum(g, axis=-1)
    tri = jnp.tril(jnp.ones((chunk, chunk), bool))
    strict = jnp.tril(jnp.ones((chunk, chunk), bool), -1)
    decay = jnp.exp(jnp.where(tri, gc[..., :, None] - gc[..., None, :], -jnp.inf))
    kb = k * beta[..., None]
    a_low = jnp.where(strict, jnp.einsum('nbhid,nbhjd->nbhij', kb, k) * decay, 0.0)
    rhs = jnp.concatenate([v * beta[..., None], kb * jnp.exp(gc)[..., None]], axis=-1)
    sol = lax.linalg.triangular_solve(a_low + jnp.eye(chunk, dtype=jnp.float32), rhs,
                                      left_side=True, lower=True, unit_diagonal=True)
    u, w = sol[..., :DV], sol[..., DV:]
    qk = jnp.where(tri, jnp.einsum('nbhid,nbhjd->nbhij', q, k) * decay, 0.0)
    q_dec = q * jnp.exp(gc)[..., None]
    k_dec = k * jnp.exp(gc[..., -1:] - gc)[..., None]
    g_last = jnp.exp(gc[..., -1])[..., None, None]

    def step(s, xs):
        u_c, w_c, qk_c, qd_c, kd_c, gl_c = xs
        v_new = u_c - jnp.einsum('bhck,bhkv->bhcv', w_c, s)
        o_c = jnp.einsum('bhck,bhkv->bhcv', qd_c, s) + jnp.einsum('bhij,bhjv->bhiv', qk_c, v_new)
        s = s * gl_c + jnp.einsum('bhck,bhcv->bhkv', kd_c, v_new)
        return s, o_c

    s_final, o = lax.scan(step, s0, (u, w, qk, q_dec, k_dec, g_last))
    o = jnp.moveaxis(o, (0, 2), (1, 3)).reshape(B, L, H, DV)
    return o, s_final


def token_mixer(x, conv_prefix, ssm0, chunk, stick_breaking, w_in, sb_bias, conv_w, a_log, dt_bias, norm_g,
                w_proj_a, w_proj_b, w_out):
    B, L, _ = x.shape
    qkv_a, qkv_b, z_b, b_b, a_b, gate_a, gate_b = split_projection(x @ w_in)
    qa, ka, va = (t.reshape(B, L, SB_HEADS, SB_HEAD_DIM) for t in jnp.split(qkv_a, 3, axis=-1))
    y_a = stick_breaking(qa, ka, va, sb_bias).reshape(B, L, SB_WIDTH).astype(x.dtype)
    ext = jnp.concatenate([conv_prefix.astype(x.dtype), qkv_b], axis=1)
    new_conv = ext[:, L:]
    cq, ck, cv = jnp.split(causal_conv_silu(ext, conv_w), 3, axis=-1)
    q = l2norm(cq.reshape(B, L, GDN_HEADS, GDN_HEAD_DIM)) * GDN_HEAD_DIM ** -0.5
    k = l2norm(ck.reshape(B, L, GDN_HEADS, GDN_HEAD_DIM))
    v = cv.reshape(B, L, GDN_HEADS, GDN_HEAD_DIM)
    beta = jax.nn.sigmoid(b_b.astype(jnp.float32))
    g = -jnp.exp(a_log.astype(jnp.float32)) * jax.nn.softplus(a_b.astype(jnp.float32) + dt_bias.astype(jnp.float32))
    o, ssm_new = gated_delta_chunked(q, k, v, beta, g, ssm0.astype(jnp.float32), chunk)
    o = o * lax.rsqrt(jnp.mean(o * o, axis=-1, keepdims=True) + RMS_EPS) * norm_g.astype(jnp.float32)
    y_b = (o.reshape(B, L, GDN_WIDTH) * jax.nn.silu(z_b.astype(jnp.float32))).astype(x.dtype)
    merged = jax.nn.sigmoid(gate_a) * (y_a @ w_proj_a) + jax.nn.sigmoid(gate_b) * (y_b @ w_proj_b)
    return merged @ w_out, ka, va, new_conv, ssm_new.astype(ssm0.dtype)


def hier_moe(x, rg, rg_b, re, re_b, w_gate, w_up, w_down):
    B, L, D = x.shape
    xt = x.reshape(B * L, D)
    n = xt.shape[0]
    rows = jnp.arange(n)
    g_logits = (xt @ rg + rg_b).astype(jnp.float32)
    g_sel = jnp.argmax(g_logits, axis=-1)
    p_group = jax.nn.softmax(g_logits, axis=-1)[rows, g_sel][:, None]
    e_logits = (jnp.einsum('nd,gde->nge', xt, re) + re_b)[rows, g_sel].astype(jnp.float32)
    top_p, top_i = lax.top_k(jax.nn.softmax(e_logits, axis=-1), TOP_K_IN_GROUP)
    gates = p_group * top_p / jnp.sum(top_p, axis=-1, keepdims=True)
    expert_id = g_sel[:, None] * EXPERTS_PER_GROUP + top_i
    combine = jnp.einsum('nk,nke->ne', gates, jax.nn.one_hot(expert_id, N_EXPERTS, dtype=jnp.float32))
    out = jnp.zeros((n, D), jnp.float32)
    for e in range(N_EXPERTS):
        h = jax.nn.silu(xt @ w_gate[e]) * (xt @ w_up[e])
        out = out + combine[:, e:e + 1] * (h @ w_down[e])
    return out.reshape(B, L, D).astype(x.dtype)


def setup_inputs(seed: int = 0) -> dict:
    key = jax.random.key(seed)
    ks = jax.random.split(key, 27)
    f32 = jnp.float32
    n_pages = PAST_LEN // PAGE_SIZE
    n_used = DEC_BATCH * n_pages
    n_pool = n_used + n_used // 4

    def nrm(k, shape, scale):
        return jax.random.normal(k, shape, f32) * scale

    page_table = jax.random.permutation(ks[4], n_pool)[:n_used].reshape(DEC_BATCH, n_pages).astype(jnp.int32)
    a_log = jnp.log(jax.random.uniform(ks[9], (DEPTH, GDN_HEADS), f32, 1.0, 16.0))
    dt = jnp.exp(jax.random.uniform(ks[10], (DEPTH, GDN_HEADS), f32, math.log(1e-3), math.log(1e-1)))
    dt_bias = dt + jnp.log(-jnp.expm1(-dt))
    return {
        'x_prompt': nrm(ks[0], (BATCH, SEQ, D_MODEL), 1.0),
        'x_sample': nrm(ks[1], (DEC_BATCH, DEC_SEQ, D_MODEL), 1.0),
        'cache_k': nrm(ks[2], (DEPTH, n_pool, PAGE_SIZE, SB_HEADS, SB_HEAD_DIM), 1.0),
        'cache_v': nrm(ks[3], (DEPTH, n_pool, PAGE_SIZE, SB_HEADS, SB_HEAD_DIM), 1.0),
        'page_table': page_table,
        'state_conv': nrm(ks[5], (DEPTH, DEC_BATCH, CONV_WIDTH - 1, GDN_CONV_CH), 1.0),
        'state_ssm': nrm(ks[6], (DEPTH, DEC_BATCH, GDN_HEADS, GDN_HEAD_DIM, GDN_HEAD_DIM), 0.1),
        'w_in': nrm(ks[7], (DEPTH, D_MODEL, PROJ_DIM), D_MODEL ** -0.5),
        'sb_bias': SB_BIAS_INIT + nrm(ks[26], (DEPTH, SB_HEADS), 0.5),
        'conv_w': nrm(ks[8], (DEPTH, CONV_WIDTH, GDN_CONV_CH), CONV_WIDTH ** -0.5),
        'a_log': a_log,
        'dt_bias': dt_bias,
        'gdn_norm_g': 1.0 + nrm(ks[11], (DEPTH, GDN_HEAD_DIM), 0.02),
        'w_proj_a': nrm(ks[12], (DEPTH, SB_WIDTH, D_MODEL), SB_WIDTH ** -0.5 * DEEPNORM_BETA),
        'w_proj_b': nrm(ks[13], (DEPTH, GDN_WIDTH, D_MODEL), GDN_WIDTH ** -0.5 * DEEPNORM_BETA),
        'w_out': nrm(ks[14], (DEPTH, D_MODEL, D_MODEL), D_MODEL ** -0.5 * DEEPNORM_BETA),
        'ln1_g': 1.0 + nrm(ks[15], (DEPTH, D_MODEL), 0.02),
        'ln1_b': nrm(ks[16], (DEPTH, D_MODEL), 0.02),
        'router_group': nrm(ks[17], (DEPTH, D_MODEL, N_GROUPS), D_MODEL ** -0.5),
        'router_group_b': nrm(ks[18], (DEPTH, N_GROUPS), 0.01),
        'router_expert': nrm(ks[19], (DEPTH, N_GROUPS, D_MODEL, EXPERTS_PER_GROUP), D_MODEL ** -0.5),
        'router_expert_b': nrm(ks[20], (DEPTH, N_GROUPS, EXPERTS_PER_GROUP), 0.01),
        'w_gate': nrm(ks[21], (DEPTH, N_EXPERTS, D_MODEL, EXPERT_FF), D_MODEL ** -0.5),
        'w_up': nrm(ks[22], (DEPTH, N_EXPERTS, D_MODEL, EXPERT_FF), D_MODEL ** -0.5),
        'w_down': nrm(ks[23], (DEPTH, N_EXPERTS, EXPERT_FF, D_MODEL), EXPERT_FF ** -0.5 * DEEPNORM_BETA),
        'ln2_g': 1.0 + nrm(ks[24], (DEPTH, D_MODEL), 0.02),
        'ln2_b': nrm(ks[25], (DEPTH, D_MODEL), 0.02),
    }


def reference(x_prompt, x_sample, cache_k, cache_v, page_table, state_conv, state_ssm,
              w_in, sb_bias, conv_w, a_log, dt_bias, gdn_norm_g, w_proj_a, w_proj_b, w_out,
              ln1_g, ln1_b, router_group, router_group_b, router_expert, router_expert_b,
              w_gate, w_up, w_down, ln2_g, ln2_b):
    hp, hs = x_prompt, x_sample
    k_p, v_p, k_s, v_s = [], [], [], []
    c_p, c_s, s_p, s_s = [], [], [], []
    for l in range(DEPTH):
        mixer_w = (w_in[l], sb_bias[l], conv_w[l], a_log[l], dt_bias[l], gdn_norm_g[l],
                   w_proj_a[l], w_proj_b[l], w_out[l])
        moe_w = (router_group[l], router_group_b[l], router_expert[l], router_expert_b[l],
                 w_gate[l], w_up[l], w_down[l])
        conv0 = jnp.zeros((hp.shape[0], CONV_WIDTH - 1, GDN_CONV_CH), hp.dtype)
        ssm0 = jnp.zeros((hp.shape[0], GDN_HEADS, GDN_HEAD_DIM, GDN_HEAD_DIM), jnp.float32)
        mix_p, kp, vp, cp, sp_new = token_mixer(hp, conv0, ssm0, GDN_CHUNK, sb_prompt, *mixer_w)
        sb_s = lambda q, k, v, b, l=l: sb_sample(q, k, v, b, cache_k, cache_v, l, page_table)
        mix_s, ksn, vsn, csn, ssn = token_mixer(hs, state_conv[l], state_ssm[l], hs.shape[1], sb_s, *mixer_w)
        hp = layer_norm(DEEPNORM_ALPHA * hp + mix_p, ln1_g[l], ln1_b[l])
        hs = layer_norm(DEEPNORM_ALPHA * hs + mix_s, ln1_g[l], ln1_b[l])
        hp = layer_norm(DEEPNORM_ALPHA * hp + hier_moe(hp, *moe_w), ln2_g[l], ln2_b[l])
        hs = layer_norm(DEEPNORM_ALPHA * hs + hier_moe(hs, *moe_w), ln2_g[l], ln2_b[l])
        k_p.append(kp); v_p.append(vp); k_s.append(ksn); v_s.append(vsn)
        c_p.append(cp); c_s.append(csn); s_p.append(sp_new); s_s.append(ssn)
    return (hp, hs, jnp.stack(k_p), jnp.stack(v_p), jnp.stack(k_s), jnp.stack(v_s),
            jnp.stack(c_p), jnp.stack(c_s), jnp.stack(s_p), jnp.stack(s_s))
```

```python
import functools

import jax
import jax.numpy as jnp
from jax import lax
from jax.experimental import pallas as pl
from jax.experimental.pallas import tpu as pltpu

F32 = jnp.float32
BF16 = jnp.bfloat16

LANES = 128
SB_HEAD_DIM = 64
GDN_HEAD_DIM = 128
GDN_CHUNK = 128
LN_EPS = 1e-5
RMS_EPS = 1e-6
N_GROUPS = 4
EXPERTS_PER_GROUP = 8
N_EXPERTS = N_GROUPS * EXPERTS_PER_GROUP
NEG_BIG = -1e30
VMEM_LIMIT_BYTES = 58 * 1024 * 1024


def _cparams(*sem):
    return pltpu.CompilerParams(dimension_semantics=sem, vmem_limit_bytes=VMEM_LIMIT_BYTES)


def _pick(n, prefs):
    for p in prefs:
        if n % p == 0:
            return p
    return n


def _dot(a, b, hp=False):
    if hp:
        return jnp.dot(a.astype(F32), b.astype(F32), preferred_element_type=F32, precision=lax.Precision.HIGHEST)
    return jnp.dot(a.astype(BF16), b.astype(BF16), preferred_element_type=F32)


def _dot_nt(a, b):
    return lax.dot_general(a.astype(BF16), b.astype(BF16), (((1,), (1,)), ((), ())), preferred_element_type=F32)


def _dot_tn(a, b):
    return lax.dot_general(a.astype(BF16), b.astype(BF16), (((0,), (0,)), ((), ())), preferred_element_type=F32)


def _split2(a):
    hi = a.astype(BF16)
    lo = (a - hi.astype(F32)).astype(BF16)
    return hi, lo


def _split3(a):
    hi = a.astype(BF16)
    r = a - hi.astype(F32)
    mid = r.astype(BF16)
    lo = (r - mid.astype(F32)).astype(BF16)
    return hi, mid, lo


def _dot3(a, b):
    ah, al = _split2(a)
    bh, bl = _split2(b)
    d = functools.partial(jnp.dot, preferred_element_type=F32)
    return d(ah, bh) + d(ah, bl) + d(al, bh)


def _dot_exact_lhs(a_bf16, b):
    bh, bm, bl = _split3(b)
    d = functools.partial(jnp.dot, preferred_element_type=F32)
    return d(a_bf16, bh) + d(a_bf16, bm) + d(a_bf16, bl)


def _softplus(z):
    return jnp.maximum(z, 0.0) + jnp.log(1.0 + jnp.exp(-jnp.abs(z)))


def _sigmoid(z):
    return 1.0 / (1.0 + jnp.exp(-z))


def _silu(z):
    return z * _sigmoid(z)


def _layer_norm(t, g, b):
    mu = jnp.mean(t, axis=-1, keepdims=True)
    c = t - mu
    var = jnp.mean(c * c, axis=-1, keepdims=True)
    return c * lax.rsqrt(var + LN_EPS) * g + b


def _mm_kernel(x_ref, w_ref, o_ref, *, hp):
    o_ref[...] = _dot(x_ref[...], w_ref[...], hp)


def _matmul(x, w, col_block0, n_cols, tm, tn, hp):
    m, k = x.shape
    return pl.pallas_call(
        functools.partial(_mm_kernel, hp=hp),
        grid=(m // tm, n_cols // tn),
        in_specs=[pl.BlockSpec((tm, k), lambda i, j: (i, 0)),
                  pl.BlockSpec((k, tn), lambda i, j: (0, j + col_block0))],
        out_specs=pl.BlockSpec((tm, tn), lambda i, j: (i, j)),
        out_shape=jax.ShapeDtypeStruct((m, n_cols), F32),
        compiler_params=_cparams("parallel", "arbitrary"),
        name="proj_matmul",
    )(x, w)


def _sbp_kernel(bias_ref, q_ref, k_ref, v_ref, o_ref, *, tq, kb):
    hpair = pl.program_id(1)
    qi = pl.program_id(2)
    q = q_ref[...] * (SB_HEAD_DIM ** -0.5)
    lane = lax.broadcasted_iota(jnp.int32, (1, LANES), 1)
    row = lax.broadcasted_iota(jnp.int32, (tq, kb), 0)
    col = lax.broadcasted_iota(jnp.int32, (tq, kb), 1)
    r2 = lax.broadcasted_iota(jnp.int32, (kb, kb), 0)
    c2 = lax.broadcasted_iota(jnp.int32, (kb, kb), 1)
    after = jnp.where(r2 > c2, 1.0, 0.0).astype(BF16)
    n_diag = tq // kb
    out = jnp.zeros((tq, LANES), F32)
    for hh in range(LANES // SB_HEAD_DIM):
        in_head = (lane >= hh * SB_HEAD_DIM) & (lane < (hh + 1) * SB_HEAD_DIM)
        qm = jnp.where(in_head, q, 0.0).astype(BF16)
        bias = bias_ref[hpair * (LANES // SB_HEAD_DIM) + hh]

        def block(kstart, carry, masked, qm=qm, bias=bias):
            log_surv, acc = carry
            kblk = k_ref[pl.ds(kstart, kb), :]
            vblk = v_ref[pl.ds(kstart, kb), :]
            z = _dot_nt(qm, kblk) + bias
            sp = _softplus(z)
            if masked:
                visible = (kstart + col) < (qi * tq + row)
                sp = jnp.where(visible, sp, 0.0)
            hi, lo = _split2(sp)
            later = (jnp.dot(hi, after, preferred_element_type=F32)
                     + jnp.dot(lo, after, preferred_element_type=F32) + log_surv)
            w = jnp.exp(z - sp - later)
            if masked:
                w = jnp.where(visible, w, 0.0)
            acc = acc + _dot(w, vblk)
            log_surv = log_surv + jnp.sum(sp, axis=1, keepdims=True)
            return log_surv, acc

        carry = (jnp.zeros((tq, 1), F32), jnp.zeros((tq, LANES), F32))
        for d in range(n_diag):
            kstart = pl.multiple_of(qi * tq + (n_diag - 1 - d) * kb, kb)
            carry = block(kstart, carry, True)
        n_before = qi * n_diag

        def body(i, carry, block=block):
            kstart = pl.multiple_of((n_before - 1 - i) * kb, kb)
            return block(kstart, carry, False)

        carry = lax.fori_loop(0, n_before, body, carry)
        out = jnp.where(in_head, carry[1], out)
    o_ref[...] = out.astype(o_ref.dtype)


def _sb_prompt(z_main, sb_bias, batch, seq, sbw, out_dtype):
    tq = _pick(seq, (512, 256, 128))
    kb = _pick(tq, (256, 128))
    nq = seq // tq
    n_pairs = sbw // LANES
    grid_spec = pltpu.PrefetchScalarGridSpec(
        num_scalar_prefetch=1,
        grid=(batch, n_pairs, nq),
        in_specs=[pl.BlockSpec((tq, LANES), lambda b, h, i, bias: (b * nq + i, h)),
                  pl.BlockSpec((seq, LANES), lambda b, h, i, bias: (b, n_pairs + h)),
                  pl.BlockSpec((seq, LANES), lambda b, h, i, bias: (b, 2 * n_pairs + h))],
        out_specs=pl.BlockSpec((tq, LANES), lambda b, h, i, bias: (b * nq + i, h)),
    )
    return pl.pallas_call(
        functools.partial(_sbp_kernel, tq=tq, kb=kb),
        grid_spec=grid_spec,
        out_shape=jax.ShapeDtypeStruct((batch * seq, sbw), out_dtype),
        compiler_params=_cparams("parallel", "parallel", "arbitrary"),
        name="sb_prompt",
    )(sb_bias, z_main, z_main, z_main)


def _sbs_kernel(pt_ref, q_ref, knew_ref, vnew_ref, bias_ref, *refs, n_slots, heads, page):
    k_refs = refs[:n_slots]
    v_refs = refs[n_slots:2 * n_slots]
    o_ref = refs[2 * n_slots]
    ls_ref, acc_ref = refs[2 * n_slots + 1:]
    step = pl.program_id(1)
    q = q_ref[...] * (SB_HEAD_DIM ** -0.5)
    bias = bias_ref[...]

    @pl.when(step == 0)
    def _():
        z_new = jnp.sum(q * knew_ref[...], axis=1, keepdims=True) + bias
        visible = jnp.zeros(z_new.shape, jnp.bool_)
        sp_new = jnp.where(visible, _softplus(z_new), 0.0)
        w_new = jnp.where(visible, jnp.exp(z_new - _softplus(z_new)), 0.0)
        ls_ref[...] = sp_new
        acc_ref[...] = w_new * vnew_ref[...]

    q_hi, q_lo = _split2(q)
    q2 = jnp.concatenate([q_hi, q_lo], axis=0)
    hrow = lax.broadcasted_iota(jnp.int32, (2 * heads, 1), 0) % heads
    r2 = lax.broadcasted_iota(jnp.int32, (page, page), 0)
    c2 = lax.broadcasted_iota(jnp.int32, (page, page), 1)
    after = jnp.where(r2 > c2, 1.0, 0.0).astype(BF16)
    log_surv = ls_ref[...]
    acc = acc_ref[...]
    for s in range(n_slots):
        k_ref, v_ref = k_refs[s], v_refs[s]
        zz = jnp.zeros((2 * heads, page), F32)
        for h in range(heads):
            qsel = jnp.where(hrow == h, q2, jnp.zeros_like(q2))
            zz = zz + lax.dot_general(qsel, k_ref[:, h, :].astype(BF16), (((1,), (1,)), ((), ())),
                                      preferred_element_type=F32)
        z = zz[:heads] + zz[heads:] + bias
        sp = _softplus(z)
        later = _dot_exact_lhs_rhs(sp, after) + log_surv
        w = jnp.exp(z - sp - later)
        w_hi, w_lo = _split2(w)
        w2 = jnp.concatenate([w_hi, w_lo], axis=0)
        oo = jnp.zeros((2 * heads, SB_HEAD_DIM), F32)
        for h in range(heads):
            wsel = jnp.where(hrow == h, w2, jnp.zeros_like(w2))
            oo = oo + jnp.dot(wsel, v_ref[:, h, :].astype(BF16), preferred_element_type=F32)
        acc = acc + oo[:heads] + oo[heads:]
        log_surv = log_surv + jnp.sum(sp, axis=1, keepdims=True)
    ls_ref[...] = log_surv
    acc_ref[...] = acc

    @pl.when(step == pl.num_programs(1) - 1)
    def _():
        o_ref[...] = acc


def _dot_exact_lhs_rhs(a, mask_bf16):
    ah, am, al = _split3(a)
    d = functools.partial(jnp.dot, preferred_element_type=F32)
    return d(ah, mask_bf16) + d(am, mask_bf16) + d(al, mask_bf16)


def _sb_sample(q, k_new, v_new, sb_bias, cache_k, cache_v, layer, page_table):
    bsz, heads, d = q.shape
    page = cache_k.shape[2]
    n_pages = page_table.shape[1]
    n_slots = _pick(n_pages, (4, 2, 1))
    n_steps = n_pages // n_slots

    def page_map(slot):
        def index_map(b, s, pt):
            return (layer, pt[b, n_pages - 1 - (s * n_slots + slot)], 0, 0, 0)
        return index_map

    row_spec = pl.BlockSpec((None, heads, d), lambda b, s, pt: (b, 0, 0))
    page_specs = [pl.BlockSpec((None, None, page, heads, d), page_map(slot)) for slot in range(n_slots)]
    grid_spec = pltpu.PrefetchScalarGridSpec(
        num_scalar_prefetch=1,
        grid=(bsz, n_steps),
        in_specs=[row_spec, row_spec, row_spec, pl.BlockSpec((heads, 1), lambda b, s, pt: (0, 0))]
        + page_specs + page_specs,
        out_specs=row_spec,
        scratch_shapes=[pltpu.VMEM((heads, 1), F32), pltpu.VMEM((heads, d), F32)],
    )
    return pl.pallas_call(
        functools.partial(_sbs_kernel, n_slots=n_slots, heads=heads, page=page),
        grid_spec=grid_spec,
        out_shape=jax.ShapeDtypeStruct((bsz, heads, d), F32),
        compiler_params=_cparams("parallel", "arbitrary"),
        name="sb_sample",
    )(page_table, q, k_new, v_new, sb_bias.reshape(heads, 1), *([cache_k] * n_slots), *([cache_v] * n_slots))


def _conv_heads(c, gh):
    outs = []
    for hh in range(3 * gh):
        x = c[:, hh * LANES:(hh + 1) * LANES]
        if hh < 2 * gh:
            x = x * lax.rsqrt(jnp.sum(x * x, axis=-1, keepdims=True) + RMS_EPS)
            if hh < gh:
                x = x * (GDN_HEAD_DIM ** -0.5)
        outs.append(x)
    return outs


def _gprep_kernel(prev_ref, cur_ref, cw_ref, o_ref, ext_ref, *, tb, gh, width):
    t = pl.program_id(1)
    ext_ref[8:, :] = cur_ref[...]
    ext_ref[0:8, :] = jnp.where(t == 0, 0.0, prev_ref[...])
    off = 8 - (width - 1)
    acc = ext_ref[off:off + tb, :] * cw_ref[0:1, :]
    for i in range(1, width):
        acc = acc + ext_ref[off + i:off + i + tb, :] * cw_ref[i:i + 1, :]
    c = _silu(acc)
    for hh, x in enumerate(_conv_heads(c, gh)):
        o_ref[:, hh * LANES:(hh + 1) * LANES] = x


def _gdn_prep(z_main, conv_w, batch, seq, col_block, gh):
    width, cc = conv_w.shape
    tb = _pick(seq, (256, 128, 64, 32, 16, 8))
    nt = seq // tb
    return pl.pallas_call(
        functools.partial(_gprep_kernel, tb=tb, gh=gh, width=width),
        grid=(batch, nt),
        in_specs=[pl.BlockSpec((8, cc), lambda b, t: (jnp.maximum(b * (seq // 8) + t * (tb // 8) - 1, 0), col_block)),
                  pl.BlockSpec((tb, cc), lambda b, t: (b * nt + t, col_block)),
                  pl.BlockSpec((width, cc), lambda b, t: (0, 0))],
        out_specs=pl.BlockSpec((tb, cc), lambda b, t: (b * nt + t, 0)),
        out_shape=jax.ShapeDtypeStruct((batch * seq, cc), F32),
        scratch_shapes=[pltpu.VMEM((tb + 8, cc), F32)],
        compiler_params=_cparams("parallel", "arbitrary"),
        name="gdn_prep",
    )(z_main, z_main, conv_w)


def _unit_lower_inverse(a):
    n = a.shape[0]
    eye = jnp.where(lax.broadcasted_iota(jnp.int32, (n, n), 0) == lax.broadcasted_iota(jnp.int32, (n, n), 1), 1.0, 0.0)
    p = eye - a
    pw = _dot3(a, a)
    size = 2
    while size < n:
        p = p + _dot3(p, pw)
        size *= 2
        if size < n:
            pw = _dot3(pw, pw)
    return p


def _gdn_kernel(qkv_ref, ba_ref, zb_ref, alog_ref, dtb_ref, ng_ref, y_ref, sfin_ref, s_scr, *, gh):
    c = pl.program_id(1)
    n = GDN_CHUNK
    gw = gh * GDN_HEAD_DIM

    @pl.when(c == 0)
    def _():
        s_scr[...] = jnp.zeros_like(s_scr)

    row = lax.broadcasted_iota(jnp.int32, (n, n), 0)
    col = lax.broadcasted_iota(jnp.int32, (n, n), 1)
    tri = row >= col
    strict = row > col
    lmat = jnp.where(tri, 1.0, 0.0).astype(BF16)
    ba = ba_ref[...]
    for h in range(gh):
        q = qkv_ref[:, h * LANES:(h + 1) * LANES]
        k = qkv_ref[:, gw + h * LANES:gw + (h + 1) * LANES]
        v = qkv_ref[:, 2 * gw + h * LANES:2 * gw + (h + 1) * LANES]
        beta = _sigmoid(ba[:, h:h + 1])
        g = -jnp.exp(alog_ref[0:1, h:h + 1]) * _softplus(ba[:, gh + h:gh + h + 1] + dtb_ref[0:1, h:h + 1])
        g_b = jnp.broadcast_to(g, (n, n))
        gm = _dot_exact_lhs(lmat, jnp.concatenate([jnp.where(strict, g_b, 0.0), g_b], axis=1))
        diff = gm[:, :n]
        gc = gm[:, n:]
        decay = jnp.where(tri, jnp.exp(diff), 0.0)
        e_gc = jnp.exp(gc)
        gc_last = gc[n - 1:n, :]
        e_rest = jnp.exp(gc_last - gc)
        g_last = jnp.exp(gc_last)
        beta_b = jnp.broadcast_to(beta, (n, GDN_HEAD_DIM))
        kbeta = k * beta_b
        a_low = jnp.where(strict, _dot_nt(kbeta, k) * decay, 0.0)
        t_inv = _unit_lower_inverse(a_low)
        sol = _dot3(t_inv, jnp.concatenate([v * beta_b, kbeta * e_gc], axis=1))
        u = sol[:, :GDN_HEAD_DIM]
        w = sol[:, GDN_HEAD_DIM:]
        qk = jnp.where(tri, _dot_nt(q, k) * decay, 0.0)
        q_dec = q * e_gc
        k_dec = k * e_rest
        s = s_scr[h]
        v_new = u - _dot(w, s)
        o = _dot(q_dec, s) + _dot(qk, v_new)
        s_scr[h] = s * g_last + _dot_tn(k_dec, v_new)
        o = o * lax.rsqrt(jnp.mean(o * o, axis=-1, keepdims=True) + RMS_EPS) * ng_ref[...]
        y_ref[:, h * LANES:(h + 1) * LANES] = (o * _silu(zb_ref[:, h * LANES:(h + 1) * LANES])).astype(y_ref.dtype)

    @pl.when(c == pl.num_programs(1) - 1)
    def _():
        sfin_ref[...] = s_scr[...]


def _gdn_prompt(qkv_n, ba, z_main, zb_col_block, a_log, dt_bias, norm_g, batch, seq, gh, out_dtype):
    assert seq % GDN_CHUNK == 0 and GDN_CHUNK == GDN_HEAD_DIM == LANES
    gw = gh * GDN_HEAD_DIM
    nc = seq // GDN_CHUNK
    return pl.pallas_call(
        functools.partial(_gdn_kernel, gh=gh),
        grid=(batch, nc),
        in_specs=[pl.BlockSpec((GDN_CHUNK, 3 * gw), lambda b, c: (b * nc + c, 0)),
                  pl.BlockSpec((GDN_CHUNK, 2 * gh), lambda b, c: (b * nc + c, 0)),
                  pl.BlockSpec((GDN_CHUNK, gw), lambda b, c: (b * nc + c, zb_col_block)),
                  pl.BlockSpec((1, gh), lambda b, c: (0, 0)),
                  pl.BlockSpec((1, gh), lambda b, c: (0, 0)),
                  pl.BlockSpec((1, GDN_HEAD_DIM), lambda b, c: (0, 0))],
        out_specs=[pl.BlockSpec((GDN_CHUNK, gw), lambda b, c: (b * nc + c, 0)),
                   pl.BlockSpec((None, gh, GDN_HEAD_DIM, GDN_HEAD_DIM), lambda b, c: (b, 0, 0, 0))],
        out_shape=[jax.ShapeDtypeStruct((batch * seq, gw), out_dtype),
                   jax.ShapeDtypeStruct((batch, gh, GDN_HEAD_DIM, GDN_HEAD_DIM), F32)],
        scratch_shapes=[pltpu.VMEM((gh, GDN_HEAD_DIM, GDN_HEAD_DIM), F32)],
        compiler_params=_cparams("parallel", "arbitrary"),
        name="gdn_prompt",
    )(qkv_n, ba, z_main, a_log.reshape(1, gh), dt_bias.reshape(1, gh), norm_g.reshape(1, GDN_HEAD_DIM))


def _gdns_kernel(sc_ref, zrow_ref, zb_ref, ba_ref, cw_ref, alog_ref, dtb_ref, ng_ref, s_ref,
                 y_ref, cnew_ref, snew_ref, *, gh, width):
    rows = [sc_ref[i:i + 1, :] for i in range(width - 1)] + [zrow_ref[...]]
    acc = rows[0] * cw_ref[0:1, :]
    for i in range(1, width):
        acc = acc + rows[i] * cw_ref[i:i + 1, :]
    for i in range(width - 1):
        cnew_ref[i:i + 1, :] = rows[i + 1]
    heads = _conv_heads(_silu(acc), gh)
    n = GDN_HEAD_DIM
    eye = lax.broadcasted_iota(jnp.int32, (n, n), 0) == lax.broadcasted_iota(jnp.int32, (n, n), 1)

    def column(r):
        return jnp.sum(jnp.where(eye, jnp.broadcast_to(r, (n, n)), 0.0), axis=1, keepdims=True)

    ba = ba_ref[...]
    for h in range(gh):
        q, k, v = heads[h], heads[gh + h], heads[2 * gh + h]
        beta = _sigmoid(ba[:, h:h + 1])
        g = -jnp.exp(alog_ref[0:1, h:h + 1]) * _softplus(ba[:, gh + h:gh + h + 1] + dtb_ref[0:1, h:h + 1])
        eg = jnp.exp(g)
        kcol = column(k)
        s = s_ref[h]
        ks = jnp.sum(kcol * s, axis=0, keepdims=True)
        v_new = beta * v - (beta * eg) * ks
        s_new = s * eg + kcol * v_new
        snew_ref[h] = s_new
        o = jnp.sum(column(q) * s_new, axis=0, keepdims=True)
        o = o * lax.rsqrt(jnp.mean(o * o, axis=-1, keepdims=True) + RMS_EPS) * ng_ref[...]
        y_ref[:, h * LANES:(h + 1) * LANES] = o * _silu(zb_ref[:, h * LANES:(h + 1) * LANES])


def _gdn_sample(state_conv, z3, ba3, conv_w, a_log, dt_bias, norm_g, state_ssm, qkvb_col_block, zb_col_block, gh):
    bsz, wm1, cc = state_conv.shape
    width = wm1 + 1
    gw = gh * GDN_HEAD_DIM
    small = lambda shape: pl.BlockSpec(shape, lambda b: (0,) * len(shape))
    return pl.pallas_call(
        functools.partial(_gdns_kernel, gh=gh, width=width),
        grid=(bsz,),
        in_specs=[pl.BlockSpec((None, wm1, cc), lambda b: (b, 0, 0)),
                  pl.BlockSpec((None, 1, cc), lambda b: (b, 0, qkvb_col_block)),
                  pl.BlockSpec((None, 1, gw), lambda b: (b, 0, zb_col_block)),
                  pl.BlockSpec((None, 1, 2 * gh), lambda b: (b, 0, 0)),
                  small((width, cc)), small((1, gh)), small((1, gh)), small((1, GDN_HEAD_DIM)),
                  pl.BlockSpec((None, gh, GDN_HEAD_DIM, GDN_HEAD_DIM), lambda b: (b, 0, 0, 0))],
        out_specs=[pl.BlockSpec((None, 1, gw), lambda b: (b, 0, 0)),
                   pl.BlockSpec((None, wm1, cc), lambda b: (b, 0, 0)),
                   pl.BlockSpec((None, gh, GDN_HEAD_DIM, GDN_HEAD_DIM), lambda b: (b, 0, 0, 0))],
        out_shape=[jax.ShapeDtypeStruct((bsz, 1, gw), F32),
                   jax.ShapeDtypeStruct((bsz, wm1, cc), F32),
                   jax.ShapeDtypeStruct((bsz, gh, GDN_HEAD_DIM, GDN_HEAD_DIM), F32)],
        compiler_params=_cparams("parallel"),
        name="gdn_sample",
    )(state_conv, z3, z3, ba3, conv_w, a_log.reshape(1, gh), dt_bias.reshape(1, gh),
      norm_g.reshape(1, GDN_HEAD_DIM), state_ssm)


def _mixa_kernel(x_ref, ya_ref, yb_ref, wga_ref, wgb_ref, wpa_ref, wpb_ref, o_ref, *, hp):
    x = x_ref[...]
    ga = _dot(x, wga_ref[...], hp)
    gb = _dot(x, wgb_ref[...], hp)
    pa = _dot(ya_ref[...], wpa_ref[...], hp)
    pb = _dot(yb_ref[...], wpb_ref[...], hp)
    o_ref[...] = (_sigmoid(ga) * pa + _sigmoid(gb) * pb).astype(o_ref.dtype)


def _mix_merge(x, ya, yb, w_gates, w_pa, w_pb, tm, tn, hp):
    m, dm = x.shape
    nj = dm // tn
    return pl.pallas_call(
        functools.partial(_mixa_kernel, hp=hp),
        grid=(m // tm, nj),
        in_specs=[pl.BlockSpec((tm, dm), lambda i, j: (i, 0)),
                  pl.BlockSpec((tm, ya.shape[1]), lambda i, j: (i, 0)),
                  pl.BlockSpec((tm, yb.shape[1]), lambda i, j: (i, 0)),
                  pl.BlockSpec((dm, tn), lambda i, j: (0, j)),
                  pl.BlockSpec((dm, tn), lambda i, j: (0, nj + j)),
                  pl.BlockSpec((w_pa.shape[0], tn), lambda i, j: (0, j)),
                  pl.BlockSpec((w_pb.shape[0], tn), lambda i, j: (0, j))],
        out_specs=pl.BlockSpec((tm, tn), lambda i, j: (i, j)),
        out_shape=jax.ShapeDtypeStruct((m, dm), F32),
        compiler_params=_cparams("parallel", "arbitrary"),
        name="mix_merge",
    )(x, ya, yb, w_gates, w_gates, w_pa, w_pb)


def _route(logits):
    lane = lax.broadcasted_iota(jnp.int32, logits.shape, 1).astype(F32)
    is_group = (lane >= N_EXPERTS) & (lane < N_EXPERTS + N_GROUPS)
    gl = jnp.where(is_group, logits, NEG_BIG)
    gmax = jnp.max(gl, axis=1, keepdims=True)
    gsel = jnp.min(jnp.where(gl == gmax, lane, 1e9), axis=1, keepdims=True) - N_EXPERTS
    p_group = 1.0 / jnp.sum(jnp.where(is_group, jnp.exp(gl - gmax), 0.0), axis=1, keepdims=True)
    in_group = (lane >= gsel * EXPERTS_PER_GROUP) & (lane < (gsel + 1.0) * EXPERTS_PER_GROUP)
    el = jnp.where(in_group, logits, NEG_BIG)
    m1 = jnp.max(el, axis=1, keepdims=True)
    i1 = jnp.min(jnp.where(el == m1, lane, 1e9), axis=1, keepdims=True)
    el2 = jnp.where(lane == i1, NEG_BIG, el)
    m2 = jnp.max(el2, axis=1, keepdims=True)
    i2 = jnp.min(jnp.where(el2 == m2, lane, 1e9), axis=1, keepdims=True)
    e2 = jnp.exp(m2 - m1)
    g1 = p_group / (1.0 + e2)
    g2 = p_group * e2 / (1.0 + e2)
    return jnp.where(lane == i1, g1, jnp.where(lane == i2, g2, 0.0))


def _mixb_kernel(x_ref, m_ref, wout_ref, g_ref, b_ref, rw_ref, rb_ref, h_ref, comb_ref, *, alpha, hp):
    t = alpha * x_ref[...] + _dot(m_ref[...], wout_ref[...], hp)
    h = _layer_norm(t, g_ref[...], b_ref[...])
    h_ref[...] = h
    if hp:
        logits = _dot(h, rw_ref[...], True) + rb_ref[...]
    else:
        logits = _dot3(h, rw_ref[...]) + rb_ref[...]
    comb_ref[...] = _route(logits)[:, :N_EXPERTS]


def _mix_out(x, merged, w_out, ln_g, ln_b, router_w, router_b, alpha, tm, hp):
    m, dm = x.shape
    const = lambda shape: pl.BlockSpec(shape, lambda i: (0,) * len(shape))
    return pl.pallas_call(
        functools.partial(_mixb_kernel, alpha=alpha, hp=hp),
        grid=(m // tm,),
        in_specs=[pl.BlockSpec((tm, dm), lambda i: (i, 0)),
                  pl.BlockSpec((tm, dm), lambda i: (i, 0)),
                  const((dm, dm)), const((1, dm)), const((1, dm)), const((dm, LANES)), const((1, LANES))],
        out_specs=[pl.BlockSpec((tm, dm), lambda i: (i, 0)),
                   pl.BlockSpec((tm, N_EXPERTS), lambda i: (i, 0))],
        out_shape=[jax.ShapeDtypeStruct((m, dm), F32), jax.ShapeDtypeStruct((m, N_EXPERTS), F32)],
        compiler_params=_cparams("parallel"),
        name="mix_out_ln_route",
    )(x, merged, w_out, ln_g.reshape(1, dm), ln_b.reshape(1, dm), router_w, router_b)


def _moe_kernel(h_ref, comb_ref, wg_ref, wu_ref, wd_ref, g_ref, b_ref, o_ref, hb_scr, *, alpha):
    e = pl.program_id(1)

    @pl.when(e == 0)
    def _():
        hb_scr[...] = h_ref[...].astype(BF16)
        o_ref[...] = jnp.zeros_like(o_ref)

    lane = lax.broadcasted_iota(jnp.int32, comb_ref.shape, 1)
    gate = jnp.sum(jnp.where(lane == e, comb_ref[...], 0.0), axis=1, keepdims=True)
    hb = hb_scr[...]
    a = jnp.dot(hb, wg_ref[...].astype(BF16), preferred_element_type=F32)
    u = jnp.dot(hb, wu_ref[...].astype(BF16), preferred_element_type=F32)
    act = _silu(a) * u * gate
    o_ref[...] += jnp.dot(act.astype(BF16), wd_ref[...].astype(BF16), preferred_element_type=F32)

    @pl.when(e == pl.num_programs(1) - 1)
    def _():
        o_ref[...] = _layer_norm(alpha * h_ref[...] + o_ref[...], g_ref[...], b_ref[...])


def _moe(h, comb, w_gate, w_up, w_down, ln_g, ln_b, alpha, tm):
    m, dm = h.shape
    n_exp, _, ff = w_gate.shape
    return pl.pallas_call(
        functools.partial(_moe_kernel, alpha=alpha),
        grid=(m // tm, n_exp),
        in_specs=[pl.BlockSpec((tm, dm), lambda i, e: (i, 0)),
                  pl.BlockSpec((tm, n_exp), lambda i, e: (i, 0)),
                  pl.BlockSpec((None, dm, ff), lambda i, e: (e, 0, 0)),
                  pl.BlockSpec((None, dm, ff), lambda i, e: (e, 0, 0)),
                  pl.BlockSpec((None, ff, dm), lambda i, e: (e, 0, 0)),
                  pl.BlockSpec((1, dm), lambda i, e: (0, 0)),
                  pl.BlockSpec((1, dm), lambda i, e: (0, 0))],
        out_specs=pl.BlockSpec((tm, dm), lambda i, e: (i, 0)),
        out_shape=jax.ShapeDtypeStruct((m, dm), F32),
        scratch_shapes=[pltpu.VMEM((tm, dm), BF16)],
        compiler_params=_cparams("parallel", "arbitrary"),
        name="moe_experts_ln",
    )(h, comb, w_gate, w_up, w_down, ln_g.reshape(1, dm), ln_b.reshape(1, dm))


def _layer(hp_x, hs_x, cache_k, cache_v, page_table, state_conv, state_ssm, layer, w, alpha):
    batch, seq, dm = hp_x.shape
    dec_batch, dec_seq, _ = hs_x.shape
    assert dec_seq == 1
    sb_heads = w["sb_bias"].shape[0]
    gh = w["a_log"].shape[0]
    sbw = sb_heads * SB_HEAD_DIM
    gw = gh * GDN_HEAD_DIM
    cc = 3 * gw
    n_main = 3 * sbw + cc + gw
    assert sbw % LANES == 0 and 3 * sbw % cc == 0 and n_main % gw == 0 and sbw == gw
    qkvb_block = 3 * sbw // cc
    zb_block = (3 * sbw + cc) // gw
    w_in = w["w_in"]
    w_ba = w_in[:, n_main:n_main + 2 * gh]
    w_gates = w_in[:, n_main + 2 * gh:]
    router_w = jnp.concatenate(
        [jnp.transpose(w["router_expert"], (1, 0, 2)).reshape(dm, N_EXPERTS), w["router_group"],
         jnp.zeros((dm, LANES - N_EXPERTS - N_GROUPS), F32)], axis=1)
    router_b = jnp.concatenate(
        [w["router_expert_b"].reshape(N_EXPERTS), w["router_group_b"],
         jnp.zeros((LANES - N_EXPERTS - N_GROUPS,), F32)]).reshape(1, LANES)
    wg16, wu16, wd16 = (w[n].astype(BF16) for n in ("w_gate", "w_up", "w_down"))

    m = batch * seq
    xp = hp_x.reshape(m, dm)
    xp16 = xp.astype(BF16)
    tm = _pick(m, (1024, 512, 256, 128, 64, 32, 16, 8))
    z_main = _matmul(xp16, w_in, 0, n_main, tm, _pick(n_main, (512, 256, 128)), False)
    ba = _matmul(xp16, w_ba, 0, 2 * gh, tm, 2 * gh, False)
    y_a = _sb_prompt(z_main, w["sb_bias"], batch, seq, sbw, BF16)
    qkv_n = _gdn_prep(z_main, w["conv_w"], batch, seq, qkvb_block, gh)
    y_b, ssm_p = _gdn_prompt(qkv_n, ba, z_main, zb_block, w["a_log"], w["dt_bias"], w["gdn_norm_g"],
                             batch, seq, gh, BF16)
    merged = _mix_merge(xp16, y_a, y_b, w_gates.astype(BF16), w["w_proj_a"].astype(BF16), w["w_proj_b"].astype(BF16),
                        tm, _pick(dm, (512, 256, 128)), False)
    h1, comb = _mix_out(xp, merged, w["w_out"].astype(BF16), w["ln1_g"], w["ln1_b"], router_w, router_b, alpha,
                        _pick(m, (256, 128, 64, 32, 16, 8)), False)
    out_p = _moe(h1, comb, wg16, wu16, wd16, w["ln2_g"], w["ln2_b"], alpha, _pick(m, (512, 256, 128, 64, 32, 16, 8)))
    k_p = z_main[:, sbw:2 * sbw].reshape(batch, seq, sb_heads, SB_HEAD_DIM)
    v_p = z_main[:, 2 * sbw:3 * sbw].reshape(batch, seq, sb_heads, SB_HEAD_DIM)
    conv_p = z_main[:, 3 * sbw:3 * sbw + cc].reshape(batch, seq, cc)[:, seq - (w["conv_w"].shape[0] - 1):, :]

    xs = hs_x.reshape(dec_batch, dm)
    zs = _matmul(xs, w_in, 0, n_main, dec_batch, _pick(n_main, (1024, 512, 256, 128)), True)
    ba_s = _matmul(xs, w_ba, 0, 2 * gh, dec_batch, 2 * gh, True)
    q_s = zs[:, :sbw].reshape(dec_batch, sb_heads, SB_HEAD_DIM)
    k_s = zs[:, sbw:2 * sbw].reshape(dec_batch, sb_heads, SB_HEAD_DIM)
    v_s = zs[:, 2 * sbw:3 * sbw].reshape(dec_batch, sb_heads, SB_HEAD_DIM)
    ya_s = _sb_sample(q_s, k_s, v_s, w["sb_bias"], cache_k, cache_v, layer, page_table).reshape(dec_batch, sbw)
    yb_s, conv_s, ssm_s = _gdn_sample(state_conv, zs.reshape(dec_batch, 1, n_main), ba_s.reshape(dec_batch, 1, 2 * gh),
                                      w["conv_w"], w["a_log"], w["dt_bias"], w["gdn_norm_g"], state_ssm,
                                      qkvb_block, zb_block, gh)
    merged_s = _mix_merge(xs, ya_s, yb_s.reshape(dec_batch, gw), w_gates, w["w_proj_a"], w["w_proj_b"],
                          dec_batch, _pick(dm, (512, 256, 128)), True)
    h1_s, comb_s = _mix_out(xs, merged_s, w["w_out"], w["ln1_g"], w["ln1_b"], router_w, router_b, alpha,
                            dec_batch, True)
    out_s = _moe(h1_s, comb_s, wg16, wu16, wd16, w["ln2_g"], w["ln2_b"], alpha, dec_batch)

    return (out_p.reshape(batch, seq, dm), out_s.reshape(dec_batch, 1, dm), k_p, v_p,
            k_s.reshape(dec_batch, 1, sb_heads, SB_HEAD_DIM), v_s.reshape(dec_batch, 1, sb_heads, SB_HEAD_DIM),
            conv_p, conv_s, ssm_p, ssm_s)


def kernel(x_prompt, x_sample, cache_k, cache_v, page_table, state_conv, state_ssm, w_in, sb_bias, conv_w, a_log,
           dt_bias, gdn_norm_g, w_proj_a, w_proj_b, w_out, ln1_g, ln1_b, router_group, router_group_b,
           router_expert, router_expert_b, w_gate, w_up, w_down, ln2_g, ln2_b):
    depth = w_in.shape[0]
    alpha = (2.0 * depth) ** 0.25
    stacked = dict(w_in=w_in, sb_bias=sb_bias, conv_w=conv_w, a_log=a_log, dt_bias=dt_bias, gdn_norm_g=gdn_norm_g,
                   w_proj_a=w_proj_a, w_proj_b=w_proj_b, w_out=w_out, ln1_g=ln1_g, ln1_b=ln1_b,
                   router_group=router_group, router_group_b=router_group_b, router_expert=router_expert,
                   router_expert_b=router_expert_b, w_gate=w_gate, w_up=w_up, w_down=w_down, ln2_g=ln2_g, ln2_b=ln2_b)
    hp_x, hs_x = x_prompt, x_sample
    per_layer = []
    for layer in range(depth):
        w = {name: t[layer] for name, t in stacked.items()}
        outs = _layer(hp_x, hs_x, cache_k, cache_v, page_table, state_conv[layer], state_ssm[layer], layer, w, alpha)
        hp_x, hs_x = outs[0], outs[1]
        per_layer.append(outs[2:])
    return (hp_x, hs_x) + tuple(jnp.stack([o[i] for o in per_layer]) for i in range(8))
```

```python
import functools

import jax
import jax.numpy as jnp
from jax import lax
from jax.experimental import pallas as pl
from jax.experimental.pallas import tpu as pltpu

F32 = jnp.float32
BF16 = jnp.bfloat16

LANES = 128
SB_HEAD_DIM = 64
GDN_HEAD_DIM = 128
GDN_CHUNK = 128
LN_EPS = 1e-5
RMS_EPS = 1e-6
N_GROUPS = 4
EXPERTS_PER_GROUP = 8
N_EXPERTS = N_GROUPS * EXPERTS_PER_GROUP
NEG_BIG = -1e30
VMEM_LIMIT_BYTES = 58 * 1024 * 1024


def _cparams(*sem):
    return pltpu.CompilerParams(dimension_semantics=sem, vmem_limit_bytes=VMEM_LIMIT_BYTES)


def _pick(n, prefs):
    for p in prefs:
        if n % p == 0:
            return p
    return n


def _dot(a, b, hp=False):
    if hp:
        return jnp.dot(a.astype(F32), b.astype(F32), preferred_element_type=F32, precision=lax.Precision.HIGHEST)
    return jnp.dot(a.astype(BF16), b.astype(BF16), preferred_element_type=F32)


def _dot_nt(a, b):
    return lax.dot_general(a.astype(BF16), b.astype(BF16), (((1,), (1,)), ((), ())), preferred_element_type=F32)


def _dot_tn(a, b):
    return lax.dot_general(a.astype(BF16), b.astype(BF16), (((0,), (0,)), ((), ())), preferred_element_type=F32)


def _split2(a):
    hi = a.astype(BF16)
    lo = (a - hi.astype(F32)).astype(BF16)
    return hi, lo


def _split3(a):
    hi = a.astype(BF16)
    r = a - hi.astype(F32)
    mid = r.astype(BF16)
    lo = (r - mid.astype(F32)).astype(BF16)
    return hi, mid, lo


def _dot3(a, b):
    ah, al = _split2(a)
    bh, bl = _split2(b)
    d = functools.partial(jnp.dot, preferred_element_type=F32)
    return d(ah, bh) + d(ah, bl) + d(al, bh)


def _dot_exact_lhs(a_bf16, b):
    bh, bm, bl = _split3(b)
    d = functools.partial(jnp.dot, preferred_element_type=F32)
    return d(a_bf16, bh) + d(a_bf16, bm) + d(a_bf16, bl)


def _dot_exact_rhs(a, b_bf16):
    ah, am, al = _split3(a)
    d = functools.partial(jnp.dot, preferred_element_type=F32)
    return d(ah, b_bf16) + d(am, b_bf16) + d(al, b_bf16)


def _softplus(z):
    return jnp.maximum(z, 0.0) + jnp.log(1.0 + jnp.exp(-jnp.abs(z)))


def _sigmoid(z):
    return 1.0 / (1.0 + jnp.exp(-z))


def _silu(z):
    return z * _sigmoid(z)


def _layer_norm(t, g, b):
    mu = jnp.mean(t, axis=-1, keepdims=True)
    c = t - mu
    var = jnp.mean(c * c, axis=-1, keepdims=True)
    return c * lax.rsqrt(var + LN_EPS) * g + b


def _mm_kernel(x_ref, w_ref, o_ref, *, hp):
    o_ref[...] = _dot(x_ref[...], w_ref[...], hp)


def _matmul(x, w, col_block0, n_cols, tm, tn, hp):
    m, k = x.shape
    return pl.pallas_call(
        functools.partial(_mm_kernel, hp=hp),
        grid=(m // tm, n_cols // tn),
        in_specs=[pl.BlockSpec((tm, k), lambda i, j: (i, 0)),
                  pl.BlockSpec((k, tn), lambda i, j: (0, j + col_block0))],
        out_specs=pl.BlockSpec((tm, tn), lambda i, j: (i, j)),
        out_shape=jax.ShapeDtypeStruct((m, n_cols), F32),
        compiler_params=_cparams("parallel", "arbitrary"),
        name="proj_matmul",
    )(x, w)


def _sbp_kernel(bias_ref, q_ref, k_ref, v_ref, o_ref, *, tq, kb):
    hpair = pl.program_id(1)
    qi = pl.program_id(2)
    q = q_ref[...] * (SB_HEAD_DIM ** -0.5)
    lane = lax.broadcasted_iota(jnp.int32, (1, LANES), 1)
    row = lax.broadcasted_iota(jnp.int32, (tq, kb), 0)
    col = lax.broadcasted_iota(jnp.int32, (tq, kb), 1)
    r2 = lax.broadcasted_iota(jnp.int32, (kb, kb), 0)
    c2 = lax.broadcasted_iota(jnp.int32, (kb, kb), 1)
    after = jnp.where(r2 > c2, 1.0, 0.0).astype(BF16)
    n_diag = tq // kb
    out = jnp.zeros((tq, LANES), F32)
    for hh in range(LANES // SB_HEAD_DIM):
        in_head = (lane >= hh * SB_HEAD_DIM) & (lane < (hh + 1) * SB_HEAD_DIM)
        qm = jnp.where(in_head, q, 0.0).astype(BF16)
        bias = bias_ref[hpair * (LANES // SB_HEAD_DIM) + hh]

        def block(kstart, carry, masked, qm=qm, bias=bias):
            log_surv, acc = carry
            kblk = k_ref[pl.ds(kstart, kb), :]
            vblk = v_ref[pl.ds(kstart, kb), :]
            z = _dot_nt(qm, kblk) + bias
            sp = _softplus(z)
            if masked:
                visible = (kstart + col) < (qi * tq + row)
                sp = jnp.where(visible, sp, 0.0)
            hi, lo = _split2(sp)
            later = (jnp.dot(hi, after, preferred_element_type=F32)
                     + jnp.dot(lo, after, preferred_element_type=F32) + log_surv)
            w = jnp.exp(z - sp - later)
            if masked:
                w = jnp.where(visible, w, 0.0)
            acc = acc + _dot(w, vblk)
            log_surv = log_surv + jnp.sum(sp, axis=1, keepdims=True)
            return log_surv, acc

        carry = (jnp.zeros((tq, 1), F32), jnp.zeros((tq, LANES), F32))
        for d in range(n_diag):
            kstart = pl.multiple_of(qi * tq + (n_diag - 1 - d) * kb, kb)
            carry = block(kstart, carry, True)
        n_before = qi * n_diag

        def body(i, carry, block=block):
            kstart = pl.multiple_of((n_before - 1 - i) * kb, kb)
            return block(kstart, carry, False)

        carry = lax.fori_loop(0, n_before, body, carry)
        out = jnp.where(in_head, carry[1], out)
    o_ref[...] = out.astype(o_ref.dtype)


def _sb_prompt(z_main, sb_bias, batch, seq, sbw, out_dtype):
    tq = _pick(seq, (512, 256, 128))
    kb = _pick(tq, (256, 128))
    nq = seq // tq
    n_pairs = sbw // LANES
    grid_spec = pltpu.PrefetchScalarGridSpec(
        num_scalar_prefetch=1,
        grid=(batch, n_pairs, nq),
        in_specs=[pl.BlockSpec((tq, LANES), lambda b, h, i, bias: (b * nq + i, h)),
                  pl.BlockSpec((seq, LANES), lambda b, h, i, bias: (b, n_pairs + h)),
                  pl.BlockSpec((seq, LANES), lambda b, h, i, bias: (b, 2 * n_pairs + h))],
        out_specs=pl.BlockSpec((tq, LANES), lambda b, h, i, bias: (b * nq + i, h)),
    )
    return pl.pallas_call(
        functools.partial(_sbp_kernel, tq=tq, kb=kb),
        grid_spec=grid_spec,
        out_shape=jax.ShapeDtypeStruct((batch * seq, sbw), out_dtype),
        compiler_params=_cparams("parallel", "parallel", "arbitrary"),
        name="sb_prompt",
    )(sb_bias, z_main, z_main, z_main)


def _sbs_kernel(pt_ref, q_ref, qb_ref, knew_ref, vnew_ref, bias_ref, *refs, n_slots, heads, page):
    k_refs = refs[:n_slots]
    v_refs = refs[n_slots:2 * n_slots]
    o_ref = refs[2 * n_slots]
    ls_ref, new_ref, acc_ref = refs[2 * n_slots + 1:]
    step = pl.program_id(1)
    d = SB_HEAD_DIM
    sub = 8
    bias = bias_ref[...]

    @pl.when(step == 0)
    def _():
        z_new = jnp.sum(q_ref[...] * knew_ref[...], axis=1, keepdims=True) * (d ** -0.5) + bias
        visible = jnp.zeros(z_new.shape, jnp.bool_)
        ls_ref[...] = jnp.broadcast_to(jnp.where(visible, _softplus(z_new), 0.0), ls_ref.shape)
        new_ref[...] = jnp.where(visible, jnp.exp(z_new - _softplus(z_new)), 0.0) * vnew_ref[...]
        acc_ref[...] = jnp.zeros_like(acc_ref)

    r2 = lax.broadcasted_iota(jnp.int32, (page, 2 * page), 0)
    c2 = lax.broadcasted_iota(jnp.int32, (page, 2 * page), 1)
    after_ones = jnp.where((r2 > c2) | (c2 >= page), 1.0, 0.0).astype(BF16)
    z_rows = [[None] * heads for _ in range(n_slots)]
    for h in range(heads):
        qh = qb_ref[h]
        for s in range(n_slots):
            prod = k_refs[s][h] * qh
            z_rows[s][h] = jnp.sum(prod, axis=0, keepdims=True)
    z = jnp.concatenate([r for rows in z_rows for r in rows], axis=0) + jnp.tile(bias, (n_slots, 1))
    sp = _softplus(z)
    cum_tot = _dot_exact_rhs(sp, after_ones)
    base = z - sp - cum_tot[:, :page]
    log_surv = ls_ref[...]
    w = []
    for s in range(n_slots):
        w.append(jnp.exp(base[s * heads:(s + 1) * heads] - log_surv))
        log_surv = log_surv + cum_tot[s * heads:(s + 1) * heads, page:]
    ls_ref[...] = log_surv
    for h in range(heads):
        a = acc_ref[h].reshape(d // sub, sub, page)
        for s in range(n_slots):
            w_row = jnp.broadcast_to(w[s][h:h + 1, :], (sub, page))
            a = a + v_refs[s][h].reshape(d // sub, sub, page) * w_row[None]
        acc_ref[h] = a.reshape(d, page)

    @pl.when(step == pl.num_programs(1) - 1)
    def _():
        o_ref[...] = new_ref[...] + jnp.sum(acc_ref[...], axis=-1)


def _sb_sample(q, k_new, v_new, sb_bias, cache_k, cache_v, layer, page_table):
    bsz, heads, d = q.shape
    page = cache_k.shape[2]
    assert d == SB_HEAD_DIM and page == LANES
    n_pages = page_table.shape[1]
    n_slots = _pick(n_pages, (8, 4, 2, 1))
    n_steps = n_pages // n_slots
    k_t = jnp.transpose(cache_k, (0, 1, 3, 4, 2))
    v_t = jnp.transpose(cache_v, (0, 1, 3, 4, 2))
    q_lanes = jnp.broadcast_to((q * (d ** -0.5))[..., None], (bsz, heads, d, page))

    def page_map(slot):
        def index_map(b, s, pt):
            return (layer, pt[b, n_pages - 1 - (s * n_slots + slot)], 0, 0, 0)
        return index_map

    row_spec = pl.BlockSpec((None, heads, d), lambda b, s, pt: (b, 0, 0))
    page_specs = [pl.BlockSpec((None, None, heads, d, page), page_map(slot)) for slot in range(n_slots)]
    grid_spec = pltpu.PrefetchScalarGridSpec(
        num_scalar_prefetch=1,
        grid=(bsz, n_steps),
        in_specs=[row_spec, pl.BlockSpec((None, heads, d, page), lambda b, s, pt: (b, 0, 0, 0)), row_spec, row_spec,
                  pl.BlockSpec((heads, 1), lambda b, s, pt: (0, 0))] + page_specs + page_specs,
        out_specs=row_spec,
        scratch_shapes=[pltpu.VMEM((heads, page), F32), pltpu.VMEM((heads, d), F32), pltpu.VMEM((heads, d, page), F32)],
    )
    return pl.pallas_call(
        functools.partial(_sbs_kernel, n_slots=n_slots, heads=heads, page=page),
        grid_spec=grid_spec,
        out_shape=jax.ShapeDtypeStruct((bsz, heads, d), F32),
        compiler_params=_cparams("parallel", "arbitrary"),
        name="sb_sample",
    )(page_table, q, q_lanes, k_new, v_new, sb_bias.reshape(heads, 1), *([k_t] * n_slots), *([v_t] * n_slots))


def _conv_heads(c, gh):
    outs = []
    for hh in range(3 * gh):
        x = c[:, hh * LANES:(hh + 1) * LANES]
        if hh < 2 * gh:
            x = x * lax.rsqrt(jnp.sum(x * x, axis=-1, keepdims=True) + RMS_EPS)
            if hh < gh:
                x = x * (GDN_HEAD_DIM ** -0.5)
        outs.append(x)
    return outs


def _gprep_kernel(prev_ref, cur_ref, cw_ref, o_ref, ext_ref, *, tb, gh, width):
    t = pl.program_id(1)
    ext_ref[8:, :] = cur_ref[...]
    ext_ref[0:8, :] = jnp.where(t == 0, 0.0, prev_ref[...])
    off = 8 - (width - 1)
    acc = ext_ref[off:off + tb, :] * cw_ref[0:1, :]
    for i in range(1, width):
        acc = acc + ext_ref[off + i:off + i + tb, :] * cw_ref[i:i + 1, :]
    c = _silu(acc)
    for hh, x in enumerate(_conv_heads(c, gh)):
        o_ref[:, hh * LANES:(hh + 1) * LANES] = x


def _gdn_prep(z_main, conv_w, batch, seq, col_block, gh):
    width, cc = conv_w.shape
    tb = _pick(seq, (256, 128, 64, 32, 16, 8))
    nt = seq // tb
    return pl.pallas_call(
        functools.partial(_gprep_kernel, tb=tb, gh=gh, width=width),
        grid=(batch, nt),
        in_specs=[pl.BlockSpec((8, cc), lambda b, t: (jnp.maximum(b * (seq // 8) + t * (tb // 8) - 1, 0), col_block)),
                  pl.BlockSpec((tb, cc), lambda b, t: (b * nt + t, col_block)),
                  pl.BlockSpec((width, cc), lambda b, t: (0, 0))],
        out_specs=pl.BlockSpec((tb, cc), lambda b, t: (b * nt + t, 0)),
        out_shape=jax.ShapeDtypeStruct((batch * seq, cc), F32),
        scratch_shapes=[pltpu.VMEM((tb + 8, cc), F32)],
        compiler_params=_cparams("parallel", "arbitrary"),
        name="gdn_prep",
    )(z_main, z_main, conv_w)


def _unit_lower_inverse(a):
    n = a.shape[0]
    eye = jnp.where(lax.broadcasted_iota(jnp.int32, (n, n), 0) == lax.broadcasted_iota(jnp.int32, (n, n), 1), 1.0, 0.0)
    p = eye - a
    pw = _dot3(a, a)
    size = 2
    while size < n:
        p = p + _dot3(p, pw)
        size *= 2
        if size < n:
            pw = _dot3(pw, pw)
    return p


def _gdn_kernel(qkv_ref, ba_ref, zb_ref, alog_ref, dtb_ref, ng_ref, y_ref, sfin_ref, s_scr, *, gh):
    c = pl.program_id(1)
    n = GDN_CHUNK
    gw = gh * GDN_HEAD_DIM

    @pl.when(c == 0)
    def _():
        s_scr[...] = jnp.zeros_like(s_scr)

    row = lax.broadcasted_iota(jnp.int32, (n, n), 0)
    col = lax.broadcasted_iota(jnp.int32, (n, n), 1)
    tri = row >= col
    strict = row > col
    lmat = jnp.where(tri, 1.0, 0.0).astype(BF16)
    ba = ba_ref[...]
    for h in range(gh):
        q = qkv_ref[:, h * LANES:(h + 1) * LANES]
        k = qkv_ref[:, gw + h * LANES:gw + (h + 1) * LANES]
        v = qkv_ref[:, 2 * gw + h * LANES:2 * gw + (h + 1) * LANES]
        beta = _sigmoid(ba[:, h:h + 1])
        g = -jnp.exp(alog_ref[0:1, h:h + 1]) * _softplus(ba[:, gh + h:gh + h + 1] + dtb_ref[0:1, h:h + 1])
        g_b = jnp.broadcast_to(g, (n, n))
        gm = _dot_exact_lhs(lmat, jnp.concatenate([jnp.where(strict, g_b, 0.0), g_b], axis=1))
        diff = gm[:, :n]
        gc = gm[:, n:]
        decay = jnp.where(tri, jnp.exp(diff), 0.0)
        e_gc = jnp.exp(gc)
        gc_last = gc[n - 1:n, :]
        e_rest = jnp.exp(gc_last - gc)
        g_last = jnp.exp(gc_last)
        beta_b = jnp.broadcast_to(beta, (n, GDN_HEAD_DIM))
        kbeta = k * beta_b
        a_low = jnp.where(strict, _dot_nt(kbeta, k) * decay, 0.0)
        t_inv = _unit_lower_inverse(a_low)
        sol = _dot3(t_inv, jnp.concatenate([v * beta_b, kbeta * e_gc], axis=1))
        u = sol[:, :GDN_HEAD_DIM]
        w = sol[:, GDN_HEAD_DIM:]
        qk = jnp.where(tri, _dot_nt(q, k) * decay, 0.0)
        q_dec = q * e_gc
        k_dec = k * e_rest
        s = s_scr[h]
        v_new = u - _dot(w, s)
        o = _dot(q_dec, s) + _dot(qk, v_new)
        s_scr[h] = s * g_last + _dot_tn(k_dec, v_new)
        o = o * lax.rsqrt(jnp.mean(o * o, axis=-1, keepdims=True) + RMS_EPS) * ng_ref[...]
        y_ref[:, h * LANES:(h + 1) * LANES] = (o * _silu(zb_ref[:, h * LANES:(h + 1) * LANES])).astype(y_ref.dtype)

    @pl.when(c == pl.num_programs(1) - 1)
    def _():
        sfin_ref[...] = s_scr[...]


def _gdn_prompt(qkv_n, ba, z_main, zb_col_block, a_log, dt_bias, norm_g, batch, seq, gh, out_dtype):
    assert seq % GDN_CHUNK == 0 and GDN_CHUNK == GDN_HEAD_DIM == LANES
    gw = gh * GDN_HEAD_DIM
    nc = seq // GDN_CHUNK
    return pl.pallas_call(
        functools.partial(_gdn_kernel, gh=gh),
        grid=(batch, nc),
        in_specs=[pl.BlockSpec((GDN_CHUNK, 3 * gw), lambda b, c: (b * nc + c, 0)),
                  pl.BlockSpec((GDN_CHUNK, 2 * gh), lambda b, c: (b * nc + c, 0)),
                  pl.BlockSpec((GDN_CHUNK, gw), lambda b, c: (b * nc + c, zb_col_block)),
                  pl.BlockSpec((1, gh), lambda b, c: (0, 0)),
                  pl.BlockSpec((1, gh), lambda b, c: (0, 0)),
                  pl.BlockSpec((1, GDN_HEAD_DIM), lambda b, c: (0, 0))],
        out_specs=[pl.BlockSpec((GDN_CHUNK, gw), lambda b, c: (b * nc + c, 0)),
                   pl.BlockSpec((None, gh, GDN_HEAD_DIM, GDN_HEAD_DIM), lambda b, c: (b, 0, 0, 0))],
        out_shape=[jax.ShapeDtypeStruct((batch * seq, gw), out_dtype),
                   jax.ShapeDtypeStruct((batch, gh, GDN_HEAD_DIM, GDN_HEAD_DIM), F32)],
        scratch_shapes=[pltpu.VMEM((gh, GDN_HEAD_DIM, GDN_HEAD_DIM), F32)],
        compiler_params=_cparams("parallel", "arbitrary"),
        name="gdn_prompt",
    )(qkv_n, ba, z_main, a_log.reshape(1, gh), dt_bias.reshape(1, gh), norm_g.reshape(1, GDN_HEAD_DIM))


def _gdns_kernel(sc_ref, zrow_ref, zb_ref, ba_ref, cw_ref, alog_ref, dtb_ref, ng_ref, s_ref,
                 y_ref, cnew_ref, snew_ref, *, gh, width):
    rows = [sc_ref[i:i + 1, :] for i in range(width - 1)] + [zrow_ref[...]]
    acc = rows[0] * cw_ref[0:1, :]
    for i in range(1, width):
        acc = acc + rows[i] * cw_ref[i:i + 1, :]
    for i in range(width - 1):
        cnew_ref[i:i + 1, :] = rows[i + 1]
    heads = _conv_heads(_silu(acc), gh)
    n = GDN_HEAD_DIM
    eye = lax.broadcasted_iota(jnp.int32, (n, n), 0) == lax.broadcasted_iota(jnp.int32, (n, n), 1)

    def column(r):
        return jnp.sum(jnp.where(eye, jnp.broadcast_to(r, (n, n)), 0.0), axis=1, keepdims=True)

    ba = ba_ref[...]
    for h in range(gh):
        q, k, v = heads[h], heads[gh + h], heads[2 * gh + h]
        beta = _sigmoid(ba[:, h:h + 1])
        g = -jnp.exp(alog_ref[0:1, h:h + 1]) * _softplus(ba[:, gh + h:gh + h + 1] + dtb_ref[0:1, h:h + 1])
        eg = jnp.exp(g)
        kcol = column(k)
        s = s_ref[h]
        ks = jnp.sum(kcol * s, axis=0, keepdims=True)
        v_new = beta * v - (beta * eg) * ks
        s_new = s * eg + kcol * v_new
        snew_ref[h] = s_new
        o = jnp.sum(column(q) * s_new, axis=0, keepdims=True)
        o = o * lax.rsqrt(jnp.mean(o * o, axis=-1, keepdims=True) + RMS_EPS) * ng_ref[...]
        y_ref[:, h * LANES:(h + 1) * LANES] = o * _silu(zb_ref[:, h * LANES:(h + 1) * LANES])


def _gdn_sample(state_conv, z3, ba3, conv_w, a_log, dt_bias, norm_g, state_ssm, qkvb_col_block, zb_col_block, gh):
    bsz, wm1, cc = state_conv.shape
    width = wm1 + 1
    gw = gh * GDN_HEAD_DIM
    small = lambda shape: pl.BlockSpec(shape, lambda b: (0,) * len(shape))
    return pl.pallas_call(
        functools.partial(_gdns_kernel, gh=gh, width=width),
        grid=(bsz,),
        in_specs=[pl.BlockSpec((None, wm1, cc), lambda b: (b, 0, 0)),
                  pl.BlockSpec((None, 1, cc), lambda b: (b, 0, qkvb_col_block)),
                  pl.BlockSpec((None, 1, gw), lambda b: (b, 0, zb_col_block)),
                  pl.BlockSpec((None, 1, 2 * gh), lambda b: (b, 0, 0)),
                  small((width, cc)), small((1, gh)), small((1, gh)), small((1, GDN_HEAD_DIM)),
                  pl.BlockSpec((None, gh, GDN_HEAD_DIM, GDN_HEAD_DIM), lambda b: (b, 0, 0, 0))],
        out_specs=[pl.BlockSpec((None, 1, gw), lambda b: (b, 0, 0)),
                   pl.BlockSpec((None, wm1, cc), lambda b: (b, 0, 0)),
                   pl.BlockSpec((None, gh, GDN_HEAD_DIM, GDN_HEAD_DIM), lambda b: (b, 0, 0, 0))],
        out_shape=[jax.ShapeDtypeStruct((bsz, 1, gw), F32),
                   jax.ShapeDtypeStruct((bsz, wm1, cc), F32),
                   jax.ShapeDtypeStruct((bsz, gh, GDN_HEAD_DIM, GDN_HEAD_DIM), F32)],
        compiler_params=_cparams("parallel"),
        name="gdn_sample",
    )(state_conv, z3, z3, ba3, conv_w, a_log.reshape(1, gh), dt_bias.reshape(1, gh),
      norm_g.reshape(1, GDN_HEAD_DIM), state_ssm)


def _mixa_kernel(x_ref, ya_ref, yb_ref, wga_ref, wgb_ref, wpa_ref, wpb_ref, o_ref, *, hp):
    x = x_ref[...]
    ga = _dot(x, wga_ref[...], hp)
    gb = _dot(x, wgb_ref[...], hp)
    pa = _dot(ya_ref[...], wpa_ref[...], hp)
    pb = _dot(yb_ref[...], wpb_ref[...], hp)
    o_ref[...] = (_sigmoid(ga) * pa + _sigmoid(gb) * pb).astype(o_ref.dtype)


def _mix_merge(x, ya, yb, w_gates, w_pa, w_pb, tm, tn, hp):
    m, dm = x.shape
    nj = dm // tn
    return pl.pallas_call(
        functools.partial(_mixa_kernel, hp=hp),
        grid=(m // tm, nj),
        in_specs=[pl.BlockSpec((tm, dm), lambda i, j: (i, 0)),
                  pl.BlockSpec((tm, ya.shape[1]), lambda i, j: (i, 0)),
                  pl.BlockSpec((tm, yb.shape[1]), lambda i, j: (i, 0)),
                  pl.BlockSpec((dm, tn), lambda i, j: (0, j)),
                  pl.BlockSpec((dm, tn), lambda i, j: (0, nj + j)),
                  pl.BlockSpec((w_pa.shape[0], tn), lambda i, j: (0, j)),
                  pl.BlockSpec((w_pb.shape[0], tn), lambda i, j: (0, j))],
        out_specs=pl.BlockSpec((tm, tn), lambda i, j: (i, j)),
        out_shape=jax.ShapeDtypeStruct((m, dm), F32),
        compiler_params=_cparams("parallel", "arbitrary"),
        name="mix_merge",
    )(x, ya, yb, w_gates, w_gates, w_pa, w_pb)


def _route(logits):
    lane = lax.broadcasted_iota(jnp.int32, logits.shape, 1).astype(F32)
    is_group = (lane >= N_EXPERTS) & (lane < N_EXPERTS + N_GROUPS)
    gl = jnp.where(is_group, logits, NEG_BIG)
    gmax = jnp.max(gl, axis=1, keepdims=True)
    gsel = jnp.min(jnp.where(gl == gmax, lane, 1e9), axis=1, keepdims=True) - N_EXPERTS
    p_group = 1.0 / jnp.sum(jnp.where(is_group, jnp.exp(gl - gmax), 0.0), axis=1, keepdims=True)
    in_group = (lane >= gsel * EXPERTS_PER_GROUP) & (lane < (gsel + 1.0) * EXPERTS_PER_GROUP)
    el = jnp.where(in_group, logits, NEG_BIG)
    m1 = jnp.max(el, axis=1, keepdims=True)
    i1 = jnp.min(jnp.where(el == m1, lane, 1e9), axis=1, keepdims=True)
    el2 = jnp.where(lane == i1, NEG_BIG, el)
    m2 = jnp.max(el2, axis=1, keepdims=True)
    i2 = jnp.min(jnp.where(el2 == m2, lane, 1e9), axis=1, keepdims=True)
    e2 = jnp.exp(m2 - m1)
    g1 = p_group / (1.0 + e2)
    g2 = p_group * e2 / (1.0 + e2)
    return jnp.where(lane == i1, g1, jnp.where(lane == i2, g2, 0.0))


def _mixb_kernel(x_ref, m_ref, wout_ref, g_ref, b_ref, rw_ref, rb_ref, h_ref, comb_ref, *, alpha, hp):
    t = alpha * x_ref[...] + _dot(m_ref[...], wout_ref[...], hp)
    h = _layer_norm(t, g_ref[...], b_ref[...])
    h_ref[...] = h
    if hp:
        logits = _dot(h, rw_ref[...], True) + rb_ref[...]
    else:
        logits = _dot3(h, rw_ref[...]) + rb_ref[...]
    comb_ref[...] = _route(logits)[:, :N_EXPERTS]


def _mix_out(x, merged, w_out, ln_g, ln_b, router_w, router_b, alpha, tm, hp):
    m, dm = x.shape
    const = lambda shape: pl.BlockSpec(shape, lambda i: (0,) * len(shape))
    return pl.pallas_call(
        functools.partial(_mixb_kernel, alpha=alpha, hp=hp),
        grid=(m // tm,),
        in_specs=[pl.BlockSpec((tm, dm), lambda i: (i, 0)),
                  pl.BlockSpec((tm, dm), lambda i: (i, 0)),
                  const((dm, dm)), const((1, dm)), const((1, dm)), const((dm, LANES)), const((1, LANES))],
        out_specs=[pl.BlockSpec((tm, dm), lambda i: (i, 0)),
                   pl.BlockSpec((tm, N_EXPERTS), lambda i: (i, 0))],
        out_shape=[jax.ShapeDtypeStruct((m, dm), F32), jax.ShapeDtypeStruct((m, N_EXPERTS), F32)],
        compiler_params=_cparams("parallel"),
        name="mix_out_ln_route",
    )(x, merged, w_out, ln_g.reshape(1, dm), ln_b.reshape(1, dm), router_w, router_b)


def _moe_kernel(h_ref, comb_ref, wg_ref, wu_ref, wd_ref, g_ref, b_ref, o_ref, hb_scr, *, alpha):
    e = pl.program_id(1)

    @pl.when(e == 0)
    def _():
        hb_scr[...] = h_ref[...].astype(BF16)
        o_ref[...] = jnp.zeros_like(o_ref)

    lane = lax.broadcasted_iota(jnp.int32, comb_ref.shape, 1)
    gate = jnp.sum(jnp.where(lane == e, comb_ref[...], 0.0), axis=1, keepdims=True)
    hb = hb_scr[...]
    a = jnp.dot(hb, wg_ref[...].astype(BF16), preferred_element_type=F32)
    u = jnp.dot(hb, wu_ref[...].astype(BF16), preferred_element_type=F32)
    act = _silu(a) * u * gate
    o_ref[...] += jnp.dot(act.astype(BF16), wd_ref[...].astype(BF16), preferred_element_type=F32)

    @pl.when(e == pl.num_programs(1) - 1)
    def _():
        o_ref[...] = _layer_norm(alpha * h_ref[...] + o_ref[...], g_ref[...], b_ref[...])


def _moe(h, comb, w_gate, w_up, w_down, ln_g, ln_b, alpha, tm):
    m, dm = h.shape
    n_exp, _, ff = w_gate.shape
    return pl.pallas_call(
        functools.partial(_moe_kernel, alpha=alpha),
        grid=(m // tm, n_exp),
        in_specs=[pl.BlockSpec((tm, dm), lambda i, e: (i, 0)),
                  pl.BlockSpec((tm, n_exp), lambda i, e: (i, 0)),
                  pl.BlockSpec((None, dm, ff), lambda i, e: (e, 0, 0)),
                  pl.BlockSpec((None, dm, ff), lambda i, e: (e, 0, 0)),
                  pl.BlockSpec((None, ff, dm), lambda i, e: (e, 0, 0)),
                  pl.BlockSpec((1, dm), lambda i, e: (0, 0)),
                  pl.BlockSpec((1, dm), lambda i, e: (0, 0))],
        out_specs=pl.BlockSpec((tm, dm), lambda i, e: (i, 0)),
        out_shape=jax.ShapeDtypeStruct((m, dm), F32),
        scratch_shapes=[pltpu.VMEM((tm, dm), BF16)],
        compiler_params=_cparams("parallel", "arbitrary"),
        name="moe_experts_ln",
    )(h, comb, w_gate, w_up, w_down, ln_g.reshape(1, dm), ln_b.reshape(1, dm))


def _layer(hp_x, hs_x, cache_k, cache_v, page_table, state_conv, state_ssm, layer, w, alpha):
    batch, seq, dm = hp_x.shape
    dec_batch, dec_seq, _ = hs_x.shape
    assert dec_seq == 1
    sb_heads = w["sb_bias"].shape[0]
    gh = w["a_log"].shape[0]
    sbw = sb_heads * SB_HEAD_DIM
    gw = gh * GDN_HEAD_DIM
    cc = 3 * gw
    n_main = 3 * sbw + cc + gw
    assert sbw % LANES == 0 and 3 * sbw % cc == 0 and n_main % gw == 0 and sbw == gw
    qkvb_block = 3 * sbw // cc
    zb_block = (3 * sbw + cc) // gw
    w_in = w["w_in"]
    w_ba = w_in[:, n_main:n_main + 2 * gh]
    w_gates = w_in[:, n_main + 2 * gh:]
    router_w = jnp.concatenate(
        [jnp.transpose(w["router_expert"], (1, 0, 2)).reshape(dm, N_EXPERTS), w["router_group"],
         jnp.zeros((dm, LANES - N_EXPERTS - N_GROUPS), F32)], axis=1)
    router_b = jnp.concatenate(
        [w["router_expert_b"].reshape(N_EXPERTS), w["router_group_b"],
         jnp.zeros((LANES - N_EXPERTS - N_GROUPS,), F32)]).reshape(1, LANES)
    wg16, wu16, wd16 = (w[n].astype(BF16) for n in ("w_gate", "w_up", "w_down"))

    m = batch * seq
    xp = hp_x.reshape(m, dm)
    xp16 = xp.astype(BF16)
    tm = _pick(m, (1024, 512, 256, 128, 64, 32, 16, 8))
    z_main = _matmul(xp16, w_in, 0, n_main, tm, _pick(n_main, (512, 256, 128)), False)
    ba = _matmul(xp16, w_ba, 0, 2 * gh, tm, 2 * gh, False)
    y_a = _sb_prompt(z_main, w["sb_bias"], batch, seq, sbw, BF16)
    qkv_n = _gdn_prep(z_main, w["conv_w"], batch, seq, qkvb_block, gh)
    y_b, ssm_p = _gdn_prompt(qkv_n, ba, z_main, zb_block, w["a_log"], w["dt_bias"], w["gdn_norm_g"],
                             batch, seq, gh, BF16)
    merged = _mix_merge(xp16, y_a, y_b, w_gates.astype(BF16), w["w_proj_a"].astype(BF16), w["w_proj_b"].astype(BF16),
                        tm, _pick(dm, (512, 256, 128)), False)
    h1, comb = _mix_out(xp, merged, w["w_out"].astype(BF16), w["ln1_g"], w["ln1_b"], router_w, router_b, alpha,
                        _pick(m, (256, 128, 64, 32, 16, 8)), False)
    out_p = _moe(h1, comb, wg16, wu16, wd16, w["ln2_g"], w["ln2_b"], alpha, _pick(m, (512, 256, 128, 64, 32, 16, 8)))
    k_p = z_main[:, sbw:2 * sbw].reshape(batch, seq, sb_heads, SB_HEAD_DIM)
    v_p = z_main[:, 2 * sbw:3 * sbw].reshape(batch, seq, sb_heads, SB_HEAD_DIM)
    conv_p = z_main[:, 3 * sbw:3 * sbw + cc].reshape(batch, seq, cc)[:, seq - (w["conv_w"].shape[0] - 1):, :]

    xs = hs_x.reshape(dec_batch, dm)
    zs = _matmul(xs, w_in, 0, n_main, dec_batch, _pick(n_main, (1024, 512, 256, 128)), True)
    ba_s = _matmul(xs, w_ba, 0, 2 * gh, dec_batch, 2 * gh, True)
    q_s = zs[:, :sbw].reshape(dec_batch, sb_heads, SB_HEAD_DIM)
    k_s = zs[:, sbw:2 * sbw].reshape(dec_batch, sb_heads, SB_HEAD_DIM)
    v_s = zs[:, 2 * sbw:3 * sbw].reshape(dec_batch, sb_heads, SB_HEAD_DIM)
    ya_s = _sb_sample(q_s, k_s, v_s, w["sb_bias"], cache_k, cache_v, layer, page_table).reshape(dec_batch, sbw)
    yb_s, conv_s, ssm_s = _gdn_sample(state_conv, zs.reshape(dec_batch, 1, n_main), ba_s.reshape(dec_batch, 1, 2 * gh),
                                      w["conv_w"], w["a_log"], w["dt_bias"], w["gdn_norm_g"], state_ssm,
                                      qkvb_block, zb_block, gh)
    merged_s = _mix_merge(xs, ya_s, yb_s.reshape(dec_batch, gw), w_gates, w["w_proj_a"], w["w_proj_b"],
                          dec_batch, _pick(dm, (512, 256, 128)), True)
    h1_s, comb_s = _mix_out(xs, merged_s, w["w_out"], w["ln1_g"], w["ln1_b"], router_w, router_b, alpha,
                            dec_batch, True)
    out_s = _moe(h1_s, comb_s, wg16, wu16, wd16, w["ln2_g"], w["ln2_b"], alpha, dec_batch)

    return (out_p.reshape(batch, seq, dm), out_s.reshape(dec_batch, 1, dm), k_p, v_p,
            k_s.reshape(dec_batch, 1, sb_heads, SB_HEAD_DIM), v_s.reshape(dec_batch, 1, sb_heads, SB_HEAD_DIM),
            conv_p, conv_s, ssm_p, ssm_s)


def kernel(x_prompt, x_sample, cache_k, cache_v, page_table, state_conv, state_ssm, w_in, sb_bias, conv_w, a_log,
           dt_bias, gdn_norm_g, w_proj_a, w_proj_b, w_out, ln1_g, ln1_b, router_group, router_group_b,
           router_expert, router_expert_b, w_gate, w_up, w_down, ln2_g, ln2_b):
    depth = w_in.shape[0]
    alpha = (2.0 * depth) ** 0.25
    stacked = dict(w_in=w_in, sb_bias=sb_bias, conv_w=conv_w, a_log=a_log, dt_bias=dt_bias, gdn_norm_g=gdn_norm_g,
                   w_proj_a=w_proj_a, w_proj_b=w_proj_b, w_out=w_out, ln1_g=ln1_g, ln1_b=ln1_b,
                   router_group=router_group, router_group_b=router_group_b, router_expert=router_expert,
                   router_expert_b=router_expert_b, w_gate=w_gate, w_up=w_up, w_down=w_down, ln2_g=ln2_g, ln2_b=ln2_b)
    hp_x, hs_x = x_prompt, x_sample
    per_layer = []
    for layer in range(depth):
        w = {name: t[layer] for name, t in stacked.items()}
        outs = _layer(hp_x, hs_x, cache_k, cache_v, page_table, state_conv[layer], state_ssm[layer], layer, w, alpha)
        hp_x, hs_x = outs[0], outs[1]
        per_layer.append(outs[2:])
    return (hp_x, hs_x) + tuple(jnp.stack([o[i] for o in per_layer]) for i in range(8))
```

```python
import functools

import jax
import jax.numpy as jnp
from jax import lax
from jax.experimental import pallas as pl
from jax.experimental.pallas import tpu as pltpu

F32 = jnp.float32
BF16 = jnp.bfloat16

LANES = 128
SB_HEAD_DIM = 64
GDN_HEAD_DIM = 128
GDN_CHUNK = 128
LN_EPS = 1e-5
RMS_EPS = 1e-6
N_GROUPS = 4
EXPERTS_PER_GROUP = 8
N_EXPERTS = N_GROUPS * EXPERTS_PER_GROUP
NEG_BIG = -1e30
VMEM_LIMIT_BYTES = 58 * 1024 * 1024


def _cparams(*sem):
    return pltpu.CompilerParams(dimension_semantics=sem, vmem_limit_bytes=VMEM_LIMIT_BYTES)


def _pick(n, prefs):
    for p in prefs:
        if n % p == 0:
            return p
    return n


def _dot(a, b, hp=False):
    if hp:
        return jnp.dot(a.astype(F32), b.astype(F32), preferred_element_type=F32, precision=lax.Precision.HIGHEST)
    return jnp.dot(a.astype(BF16), b.astype(BF16), preferred_element_type=F32)


def _dot_nt(a, b):
    return lax.dot_general(a.astype(BF16), b.astype(BF16), (((1,), (1,)), ((), ())), preferred_element_type=F32)


def _dot_tn(a, b):
    return lax.dot_general(a.astype(BF16), b.astype(BF16), (((0,), (0,)), ((), ())), preferred_element_type=F32)


def _split2(a):
    hi = a.astype(BF16)
    lo = (a - hi.astype(F32)).astype(BF16)
    return hi, lo


def _split3(a):
    hi = a.astype(BF16)
    r = a - hi.astype(F32)
    mid = r.astype(BF16)
    lo = (r - mid.astype(F32)).astype(BF16)
    return hi, mid, lo


def _dot3(a, b):
    ah, al = _split2(a)
    bh, bl = _split2(b)
    d = functools.partial(jnp.dot, preferred_element_type=F32)
    return d(ah, bh) + d(ah, bl) + d(al, bh)


def _dot_exact_lhs(a_bf16, b):
    bh, bm, bl = _split3(b)
    d = functools.partial(jnp.dot, preferred_element_type=F32)
    return d(a_bf16, bh) + d(a_bf16, bm) + d(a_bf16, bl)


def _dot_exact_rhs(a, b_bf16):
    ah, am, al = _split3(a)
    d = functools.partial(jnp.dot, preferred_element_type=F32)
    return d(ah, b_bf16) + d(am, b_bf16) + d(al, b_bf16)


def _softplus(z):
    return jnp.maximum(z, 0.0) + jnp.log(1.0 + jnp.exp(-jnp.abs(z)))


def _sigmoid(z):
    return 1.0 / (1.0 + jnp.exp(-z))


def _silu(z):
    return z * _sigmoid(z)


def _layer_norm(t, g, b):
    mu = jnp.mean(t, axis=-1, keepdims=True)
    c = t - mu
    var = jnp.mean(c * c, axis=-1, keepdims=True)
    return c * lax.rsqrt(var + LN_EPS) * g + b


def _mm_kernel(x_ref, w_ref, o_ref, *, hp):
    o_ref[...] = _dot(x_ref[...], w_ref[...], hp)


def _matmul(x, w, col_block0, n_cols, tm, tn, hp):
    m, k = x.shape
    return pl.pallas_call(
        functools.partial(_mm_kernel, hp=hp),
        grid=(m // tm, n_cols // tn),
        in_specs=[pl.BlockSpec((tm, k), lambda i, j: (i, 0)),
                  pl.BlockSpec((k, tn), lambda i, j: (0, j + col_block0))],
        out_specs=pl.BlockSpec((tm, tn), lambda i, j: (i, j)),
        out_shape=jax.ShapeDtypeStruct((m, n_cols), F32),
        compiler_params=_cparams("parallel", "arbitrary"),
        name="proj_matmul",
    )(x, w)


def _sbp_kernel(bias_ref, q_ref, k_ref, v_ref, o_ref, *, tq, kb):
    hpair = pl.program_id(1)
    qi = pl.program_id(2)
    q = q_ref[...] * (SB_HEAD_DIM ** -0.5)
    lane = lax.broadcasted_iota(jnp.int32, (1, LANES), 1)
    row = lax.broadcasted_iota(jnp.int32, (tq, kb), 0)
    col = lax.broadcasted_iota(jnp.int32, (tq, kb), 1)
    r2 = lax.broadcasted_iota(jnp.int32, (kb, kb), 0)
    c2 = lax.broadcasted_iota(jnp.int32, (kb, kb), 1)
    after = jnp.where(r2 > c2, 1.0, 0.0).astype(BF16)
    n_diag = tq // kb
    out = jnp.zeros((tq, LANES), F32)
    for hh in range(LANES // SB_HEAD_DIM):
        in_head = (lane >= hh * SB_HEAD_DIM) & (lane < (hh + 1) * SB_HEAD_DIM)
        qm = jnp.where(in_head, q, 0.0).astype(BF16)
        bias = bias_ref[hpair * (LANES // SB_HEAD_DIM) + hh]

        def block(kstart, carry, masked, qm=qm, bias=bias):
            log_surv, acc = carry
            kblk = k_ref[pl.ds(kstart, kb), :]
            vblk = v_ref[pl.ds(kstart, kb), :]
            z = _dot_nt(qm, kblk) + bias
            sp = _softplus(z)
            if masked:
                visible = (kstart + col) < (qi * tq + row)
                sp = jnp.where(visible, sp, 0.0)
            hi, lo = _split2(sp)
            later = (jnp.dot(hi, after, preferred_element_type=F32)
                     + jnp.dot(lo, after, preferred_element_type=F32) + log_surv)
            w = jnp.exp(z - sp - later)
            if masked:
                w = jnp.where(visible, w, 0.0)
            acc = acc + _dot(w, vblk)
            log_surv = log_surv + jnp.sum(sp, axis=1, keepdims=True)
            return log_surv, acc

        carry = (jnp.zeros((tq, 1), F32), jnp.zeros((tq, LANES), F32))
        for d in range(n_diag):
            kstart = pl.multiple_of(qi * tq + (n_diag - 1 - d) * kb, kb)
            carry = block(kstart, carry, True)
        n_before = qi * n_diag

        def body(i, carry, block=block):
            kstart = pl.multiple_of((n_before - 1 - i) * kb, kb)
            return block(kstart, carry, False)

        carry = lax.fori_loop(0, n_before, body, carry)
        out = jnp.where(in_head, carry[1], out)
    o_ref[...] = out.astype(o_ref.dtype)


def _sb_prompt(z_main, sb_bias, batch, seq, sbw, out_dtype):
    tq = _pick(seq, (512, 256, 128))
    kb = _pick(tq, (256, 128))
    nq = seq // tq
    n_pairs = sbw // LANES
    grid_spec = pltpu.PrefetchScalarGridSpec(
        num_scalar_prefetch=1,
        grid=(batch, n_pairs, nq),
        in_specs=[pl.BlockSpec((tq, LANES), lambda b, h, i, bias: (b * nq + i, h)),
                  pl.BlockSpec((seq, LANES), lambda b, h, i, bias: (b, n_pairs + h)),
                  pl.BlockSpec((seq, LANES), lambda b, h, i, bias: (b, 2 * n_pairs + h))],
        out_specs=pl.BlockSpec((tq, LANES), lambda b, h, i, bias: (b * nq + i, h)),
    )
    return pl.pallas_call(
        functools.partial(_sbp_kernel, tq=tq, kb=kb),
        grid_spec=grid_spec,
        out_shape=jax.ShapeDtypeStruct((batch * seq, sbw), out_dtype),
        compiler_params=_cparams("parallel", "parallel", "arbitrary"),
        name="sb_prompt",
    )(sb_bias, z_main, z_main, z_main)


def _sbs_kernel(pt_ref, q_ref, qb_ref, knew_ref, vnew_ref, bias_ref, *refs, n_slots, heads, page):
    k_refs = refs[:n_slots]
    v_refs = refs[n_slots:2 * n_slots]
    o_ref = refs[2 * n_slots]
    ls_ref, new_ref, acc_ref = refs[2 * n_slots + 1:]
    step = pl.program_id(1)
    d = SB_HEAD_DIM
    sub = 8
    bias = bias_ref[...]

    @pl.when(step == 0)
    def _():
        z_new = jnp.sum(q_ref[...] * knew_ref[...], axis=1, keepdims=True) * (d ** -0.5) + bias
        visible = jnp.zeros(z_new.shape, jnp.bool_)
        ls_ref[...] = jnp.broadcast_to(jnp.where(visible, _softplus(z_new), 0.0), ls_ref.shape)
        new_ref[...] = jnp.where(visible, jnp.exp(z_new - _softplus(z_new)), 0.0) * vnew_ref[...]
        acc_ref[...] = jnp.zeros_like(acc_ref)

    r2 = lax.broadcasted_iota(jnp.int32, (page, 2 * page), 0)
    c2 = lax.broadcasted_iota(jnp.int32, (page, 2 * page), 1)
    after_ones = jnp.where((r2 > c2) | (c2 >= page), 1.0, 0.0).astype(BF16)
    z_rows = [[None] * heads for _ in range(n_slots)]
    for h in range(heads):
        qh = qb_ref[h]
        for s in range(n_slots):
            prod = k_refs[s][h] * qh
            z_rows[s][h] = jnp.sum(prod, axis=0, keepdims=True)
    z = jnp.concatenate([r for rows in z_rows for r in rows], axis=0) + jnp.tile(bias, (n_slots, 1))
    sp = _softplus(z)
    cum_tot = _dot_exact_rhs(sp, after_ones)
    base = z - sp - cum_tot[:, :page]
    log_surv = ls_ref[...]
    w = []
    for s in range(n_slots):
        w.append(jnp.exp(base[s * heads:(s + 1) * heads] - log_surv))
        log_surv = log_surv + cum_tot[s * heads:(s + 1) * heads, page:]
    ls_ref[...] = log_surv
    for h in range(heads):
        a = acc_ref[h].reshape(d // sub, sub, page)
        for s in range(n_slots):
            w_row = jnp.broadcast_to(w[s][h:h + 1, :], (sub, page))
            a = a + v_refs[s][h].reshape(d // sub, sub, page) * w_row[None]
        acc_ref[h] = a.reshape(d, page)

    @pl.when(step == pl.num_programs(1) - 1)
    def _():
        o_ref[...] = new_ref[...] + jnp.sum(acc_ref[...], axis=-1)


def _sb_sample(q, k_new, v_new, sb_bias, cache_k, cache_v, layer, page_table):
    bsz, heads, d = q.shape
    page = cache_k.shape[2]
    assert d == SB_HEAD_DIM and page == LANES
    n_pages = page_table.shape[1]
    n_slots = _pick(n_pages, (8, 4, 2, 1))
    n_steps = n_pages // n_slots
    k_t = jnp.transpose(cache_k, (0, 1, 3, 4, 2))
    v_t = jnp.transpose(cache_v, (0, 1, 3, 4, 2))
    q_lanes = jnp.broadcast_to((q * (d ** -0.5))[..., None], (bsz, heads, d, page))

    def page_map(slot):
        def index_map(b, s, pt):
            return (layer, pt[b, n_pages - 1 - (s * n_slots + slot)], 0, 0, 0)
        return index_map

    row_spec = pl.BlockSpec((None, heads, d), lambda b, s, pt: (b, 0, 0))
    page_specs = [pl.BlockSpec((None, None, heads, d, page), page_map(slot)) for slot in range(n_slots)]
    grid_spec = pltpu.PrefetchScalarGridSpec(
        num_scalar_prefetch=1,
        grid=(bsz, n_steps),
        in_specs=[row_spec, pl.BlockSpec((None, heads, d, page), lambda b, s, pt: (b, 0, 0, 0)), row_spec, row_spec,
                  pl.BlockSpec((heads, 1), lambda b, s, pt: (0, 0))] + page_specs + page_specs,
        out_specs=row_spec,
        scratch_shapes=[pltpu.VMEM((heads, page), F32), pltpu.VMEM((heads, d), F32), pltpu.VMEM((heads, d, page), F32)],
    )
    return pl.pallas_call(
        functools.partial(_sbs_kernel, n_slots=n_slots, heads=heads, page=page),
        grid_spec=grid_spec,
        out_shape=jax.ShapeDtypeStruct((bsz, heads, d), F32),
        compiler_params=_cparams("parallel", "arbitrary"),
        name="sb_sample",
    )(page_table, q, q_lanes, k_new, v_new, sb_bias.reshape(heads, 1), *([k_t] * n_slots), *([v_t] * n_slots))


def _conv_heads(c, gh):
    outs = []
    for hh in range(3 * gh):
        x = c[:, hh * LANES:(hh + 1) * LANES]
        if hh < 2 * gh:
            x = x * lax.rsqrt(jnp.sum(x * x, axis=-1, keepdims=True) + RMS_EPS)
            if hh < gh:
                x = x * (GDN_HEAD_DIM ** -0.5)
        outs.append(x)
    return outs


def _gprep_kernel(prev_ref, cur_ref, cw_ref, o_ref, ext_ref, *, tb, gh, width):
    t = pl.program_id(1)
    ext_ref[8:, :] = cur_ref[...]
    ext_ref[0:8, :] = jnp.where(t == 0, 0.0, prev_ref[...])
    off = 8 - (width - 1)
    acc = ext_ref[off:off + tb, :] * cw_ref[0:1, :]
    for i in range(1, width):
        acc = acc + ext_ref[off + i:off + i + tb, :] * cw_ref[i:i + 1, :]
    c = _silu(acc)
    for hh, x in enumerate(_conv_heads(c, gh)):
        o_ref[:, hh * LANES:(hh + 1) * LANES] = x


def _gdn_prep(z_main, conv_w, batch, seq, col_block, gh):
    width, cc = conv_w.shape
    tb = _pick(seq, (256, 128, 64, 32, 16, 8))
    nt = seq // tb
    return pl.pallas_call(
        functools.partial(_gprep_kernel, tb=tb, gh=gh, width=width),
        grid=(batch, nt),
        in_specs=[pl.BlockSpec((8, cc), lambda b, t: (jnp.maximum(b * (seq // 8) + t * (tb // 8) - 1, 0), col_block)),
                  pl.BlockSpec((tb, cc), lambda b, t: (b * nt + t, col_block)),
                  pl.BlockSpec((width, cc), lambda b, t: (0, 0))],
        out_specs=pl.BlockSpec((tb, cc), lambda b, t: (b * nt + t, 0)),
        out_shape=jax.ShapeDtypeStruct((batch * seq, cc), F32),
        scratch_shapes=[pltpu.VMEM((tb + 8, cc), F32)],
        compiler_params=_cparams("parallel", "arbitrary"),
        name="gdn_prep",
    )(z_main, z_main, conv_w)


def _unit_lower_inverse(a):
    n = a.shape[0]
    eye = jnp.where(lax.broadcasted_iota(jnp.int32, (n, n), 0) == lax.broadcasted_iota(jnp.int32, (n, n), 1), 1.0, 0.0)
    p = eye - a
    pw = _dot3(a, a)
    size = 2
    while size < n:
        p = p + _dot3(p, pw)
        size *= 2
        if size < n:
            pw = _dot3(pw, pw)
    return p


def _gdn_kernel(qkv_ref, ba_ref, zb_ref, alog_ref, dtb_ref, ng_ref, y_ref, sfin_ref, s_scr, *, gh):
    c = pl.program_id(1)
    n = GDN_CHUNK
    gw = gh * GDN_HEAD_DIM

    @pl.when(c == 0)
    def _():
        s_scr[...] = jnp.zeros_like(s_scr)

    row = lax.broadcasted_iota(jnp.int32, (n, n), 0)
    col = lax.broadcasted_iota(jnp.int32, (n, n), 1)
    tri = row >= col
    strict = row > col
    lmat = jnp.where(tri, 1.0, 0.0).astype(BF16)
    ba = ba_ref[...]
    for h in range(gh):
        q = qkv_ref[:, h * LANES:(h + 1) * LANES]
        k = qkv_ref[:, gw + h * LANES:gw + (h + 1) * LANES]
        v = qkv_ref[:, 2 * gw + h * LANES:2 * gw + (h + 1) * LANES]
        beta = _sigmoid(ba[:, h:h + 1])
        g = -jnp.exp(alog_ref[0:1, h:h + 1]) * _softplus(ba[:, gh + h:gh + h + 1] + dtb_ref[0:1, h:h + 1])
        g_b = jnp.broadcast_to(g, (n, n))
        gm = _dot_exact_lhs(lmat, jnp.concatenate([jnp.where(strict, g_b, 0.0), g_b], axis=1))
        diff = gm[:, :n]
        gc = gm[:, n:]
        decay = jnp.where(tri, jnp.exp(diff), 0.0)
        e_gc = jnp.exp(gc)
        gc_last = gc[n - 1:n, :]
        e_rest = jnp.exp(gc_last - gc)
        g_last = jnp.exp(gc_last)
        beta_b = jnp.broadcast_to(beta, (n, GDN_HEAD_DIM))
        kbeta = k * beta_b
        a_low = jnp.where(strict, _dot_nt(kbeta, k) * decay, 0.0)
        t_inv = _unit_lower_inverse(a_low)
        sol = _dot3(t_inv, jnp.concatenate([v * beta_b, kbeta * e_gc], axis=1))
        u = sol[:, :GDN_HEAD_DIM]
        w = sol[:, GDN_HEAD_DIM:]
        qk = jnp.where(tri, _dot_nt(q, k) * decay, 0.0)
        q_dec = q * e_gc
        k_dec = k * e_rest
        s = s_scr[h]
        v_new = u - _dot(w, s)
        o = _dot(q_dec, s) + _dot(qk, v_new)
        s_scr[h] = s * g_last + _dot_tn(k_dec, v_new)
        o = o * lax.rsqrt(jnp.mean(o * o, axis=-1, keepdims=True) + RMS_EPS) * ng_ref[...]
        y_ref[:, h * LANES:(h + 1) * LANES] = (o * _silu(zb_ref[:, h * LANES:(h + 1) * LANES])).astype(y_ref.dtype)

    @pl.when(c == pl.num_programs(1) - 1)
    def _():
        sfin_ref[...] = s_scr[...]


def _gdn_prompt(qkv_n, ba, z_main, zb_col_block, a_log, dt_bias, norm_g, batch, seq, gh, out_dtype):
    assert seq % GDN_CHUNK == 0 and GDN_CHUNK == GDN_HEAD_DIM == LANES
    gw = gh * GDN_HEAD_DIM
    nc = seq // GDN_CHUNK
    return pl.pallas_call(
        functools.partial(_gdn_kernel, gh=gh),
        grid=(batch, nc),
        in_specs=[pl.BlockSpec((GDN_CHUNK, 3 * gw), lambda b, c: (b * nc + c, 0)),
                  pl.BlockSpec((GDN_CHUNK, 2 * gh), lambda b, c: (b * nc + c, 0)),
                  pl.BlockSpec((GDN_CHUNK, gw), lambda b, c: (b * nc + c, zb_col_block)),
                  pl.BlockSpec((1, gh), lambda b, c: (0, 0)),
                  pl.BlockSpec((1, gh), lambda b, c: (0, 0)),
                  pl.BlockSpec((1, GDN_HEAD_DIM), lambda b, c: (0, 0))],
        out_specs=[pl.BlockSpec((GDN_CHUNK, gw), lambda b, c: (b * nc + c, 0)),
                   pl.BlockSpec((None, gh, GDN_HEAD_DIM, GDN_HEAD_DIM), lambda b, c: (b, 0, 0, 0))],
        out_shape=[jax.ShapeDtypeStruct((batch * seq, gw), out_dtype),
                   jax.ShapeDtypeStruct((batch, gh, GDN_HEAD_DIM, GDN_HEAD_DIM), F32)],
        scratch_shapes=[pltpu.VMEM((gh, GDN_HEAD_DIM, GDN_HEAD_DIM), F32)],
        compiler_params=_cparams("parallel", "arbitrary"),
        name="gdn_prompt",
    )(qkv_n, ba, z_main, a_log.reshape(1, gh), dt_bias.reshape(1, gh), norm_g.reshape(1, GDN_HEAD_DIM))


def _gdns_kernel(sc_ref, zrow_ref, zb_ref, ba_ref, cw_ref, alog_ref, dtb_ref, ng_ref, s_ref,
                 y_ref, cnew_ref, snew_ref, *, gh, width):
    rows = [sc_ref[i:i + 1, :] for i in range(width - 1)] + [zrow_ref[...]]
    acc = rows[0] * cw_ref[0:1, :]
    for i in range(1, width):
        acc = acc + rows[i] * cw_ref[i:i + 1, :]
    for i in range(width - 1):
        cnew_ref[i:i + 1, :] = rows[i + 1]
    heads = _conv_heads(_silu(acc), gh)
    n = GDN_HEAD_DIM
    eye = lax.broadcasted_iota(jnp.int32, (n, n), 0) == lax.broadcasted_iota(jnp.int32, (n, n), 1)

    def column(r):
        return jnp.sum(jnp.where(eye, jnp.broadcast_to(r, (n, n)), 0.0), axis=1, keepdims=True)

    ba = ba_ref[...]
    for h in range(gh):
        q, k, v = heads[h], heads[gh + h], heads[2 * gh + h]
        beta = _sigmoid(ba[:, h:h + 1])
        g = -jnp.exp(alog_ref[0:1, h:h + 1]) * _softplus(ba[:, gh + h:gh + h + 1] + dtb_ref[0:1, h:h + 1])
        eg = jnp.exp(g)
        kcol = column(k)
        s = s_ref[h]
        ks = jnp.sum(kcol * s, axis=0, keepdims=True)
        v_new = beta * v - (beta * eg) * ks
        s_new = s * eg + kcol * v_new
        snew_ref[h] = s_new
        o = jnp.sum(column(q) * s_new, axis=0, keepdims=True)
        o = o * lax.rsqrt(jnp.mean(o * o, axis=-1, keepdims=True) + RMS_EPS) * ng_ref[...]
        y_ref[:, h * LANES:(h + 1) * LANES] = o * _silu(zb_ref[:, h * LANES:(h + 1) * LANES])


def _gdn_sample(state_conv, z3, ba3, conv_w, a_log, dt_bias, norm_g, state_ssm, qkvb_col_block, zb_col_block, gh):
    bsz, wm1, cc = state_conv.shape
    width = wm1 + 1
    gw = gh * GDN_HEAD_DIM
    small = lambda shape: pl.BlockSpec(shape, lambda b: (0,) * len(shape))
    return pl.pallas_call(
        functools.partial(_gdns_kernel, gh=gh, width=width),
        grid=(bsz,),
        in_specs=[pl.BlockSpec((None, wm1, cc), lambda b: (b, 0, 0)),
                  pl.BlockSpec((None, 1, cc), lambda b: (b, 0, qkvb_col_block)),
                  pl.BlockSpec((None, 1, gw), lambda b: (b, 0, zb_col_block)),
                  pl.BlockSpec((None, 1, 2 * gh), lambda b: (b, 0, 0)),
                  small((width, cc)), small((1, gh)), small((1, gh)), small((1, GDN_HEAD_DIM)),
                  pl.BlockSpec((None, gh, GDN_HEAD_DIM, GDN_HEAD_DIM), lambda b: (b, 0, 0, 0))],
        out_specs=[pl.BlockSpec((None, 1, gw), lambda b: (b, 0, 0)),
                   pl.BlockSpec((None, wm1, cc), lambda b: (b, 0, 0)),
                   pl.BlockSpec((None, gh, GDN_HEAD_DIM, GDN_HEAD_DIM), lambda b: (b, 0, 0, 0))],
        out_shape=[jax.ShapeDtypeStruct((bsz, 1, gw), F32),
                   jax.ShapeDtypeStruct((bsz, wm1, cc), F32),
                   jax.ShapeDtypeStruct((bsz, gh, GDN_HEAD_DIM, GDN_HEAD_DIM), F32)],
        compiler_params=_cparams("parallel"),
        name="gdn_sample",
    )(state_conv, z3, z3, ba3, conv_w, a_log.reshape(1, gh), dt_bias.reshape(1, gh),
      norm_g.reshape(1, GDN_HEAD_DIM), state_ssm)


def _mixa_kernel(x_ref, ya_ref, yb_ref, wga_ref, wgb_ref, wpa_ref, wpb_ref, o_ref, *, hp):
    x = x_ref[...]
    ga = _dot(x, wga_ref[...], hp)
    gb = _dot(x, wgb_ref[...], hp)
    pa = _dot(ya_ref[...], wpa_ref[...], hp)
    pb = _dot(yb_ref[...], wpb_ref[...], hp)
    o_ref[...] = (_sigmoid(ga) * pa + _sigmoid(gb) * pb).astype(o_ref.dtype)


def _mix_merge(x, ya, yb, w_gates, w_pa, w_pb, tm, tn, hp):
    m, dm = x.shape
    nj = dm // tn
    return pl.pallas_call(
        functools.partial(_mixa_kernel, hp=hp),
        grid=(m // tm, nj),
        in_specs=[pl.BlockSpec((tm, dm), lambda i, j: (i, 0)),
                  pl.BlockSpec((tm, ya.shape[1]), lambda i, j: (i, 0)),
                  pl.BlockSpec((tm, yb.shape[1]), lambda i, j: (i, 0)),
                  pl.BlockSpec((dm, tn), lambda i, j: (0, j)),
                  pl.BlockSpec((dm, tn), lambda i, j: (0, nj + j)),
                  pl.BlockSpec((w_pa.shape[0], tn), lambda i, j: (0, j)),
                  pl.BlockSpec((w_pb.shape[0], tn), lambda i, j: (0, j))],
        out_specs=pl.BlockSpec((tm, tn), lambda i, j: (i, j)),
        out_shape=jax.ShapeDtypeStruct((m, dm), F32),
        compiler_params=_cparams("parallel", "arbitrary"),
        name="mix_merge",
    )(x, ya, yb, w_gates, w_gates, w_pa, w_pb)


def _route(logits):
    lane = lax.broadcasted_iota(jnp.int32, logits.shape, 1).astype(F32)
    is_group = (lane >= N_EXPERTS) & (lane < N_EXPERTS + N_GROUPS)
    gl = jnp.where(is_group, logits, NEG_BIG)
    gmax = jnp.max(gl, axis=1, keepdims=True)
    gsel = jnp.min(jnp.where(gl == gmax, lane, 1e9), axis=1, keepdims=True) - N_EXPERTS
    p_group = 1.0 / jnp.sum(jnp.where(is_group, jnp.exp(gl - gmax), 0.0), axis=1, keepdims=True)
    in_group = (lane >= gsel * EXPERTS_PER_GROUP) & (lane < (gsel + 1.0) * EXPERTS_PER_GROUP)
    el = jnp.where(in_group, logits, NEG_BIG)
    m1 = jnp.max(el, axis=1, keepdims=True)
    i1 = jnp.min(jnp.where(el == m1, lane, 1e9), axis=1, keepdims=True)
    el2 = jnp.where(lane == i1, NEG_BIG, el)
    m2 = jnp.max(el2, axis=1, keepdims=True)
    i2 = jnp.min(jnp.where(el2 == m2, lane, 1e9), axis=1, keepdims=True)
    e2 = jnp.exp(m2 - m1)
    g1 = p_group / (1.0 + e2)
    g2 = p_group * e2 / (1.0 + e2)
    return jnp.where(lane == 0.0, i1, jnp.where(lane == 1.0, i2, jnp.where(lane == 2.0, g1,
                                                                            jnp.where(lane == 3.0, g2, 0.0))))


HI16 = 0xFFFF0000


def _pack_bf16_pair(lo, hi):
    lo_bits = lax.bitcast_convert_type(lo.astype(BF16).astype(F32), jnp.uint32)
    hi_bits = lax.bitcast_convert_type(hi.astype(BF16).astype(F32), jnp.uint32)
    return hi_bits | (lo_bits >> 16)


def _unpack_bf16_pair(u):
    lo = lax.bitcast_convert_type(u << 16, F32)
    hi = lax.bitcast_convert_type(u & jnp.uint32(HI16), F32)
    return lo, hi


def _mixb_kernel(x_ref, m_ref, wout_ref, g_ref, b_ref, rw_ref, rb_ref, h_ref, hpk_ref, route_ref, *, alpha, hp):
    t = alpha * x_ref[...] + _dot(m_ref[...], wout_ref[...], hp)
    h = _layer_norm(t, g_ref[...], b_ref[...])
    h_ref[...] = h
    half = h.shape[1] // 2
    hpk_ref[...] = _pack_bf16_pair(h[:, :half], h[:, half:])
    if hp:
        logits = _dot(h, rw_ref[...], True) + rb_ref[...]
    else:
        logits = _dot3(h, rw_ref[...]) + rb_ref[...]
    route_ref[...] = _route(logits)


def _mix_out(x, merged, w_out, ln_g, ln_b, router_w, router_b, alpha, tm, hp):
    m, dm = x.shape
    const = lambda shape: pl.BlockSpec(shape, lambda i: (0,) * len(shape))
    return pl.pallas_call(
        functools.partial(_mixb_kernel, alpha=alpha, hp=hp),
        grid=(m // tm,),
        in_specs=[pl.BlockSpec((tm, dm), lambda i: (i, 0)),
                  pl.BlockSpec((tm, dm), lambda i: (i, 0)),
                  const((dm, dm)), const((1, dm)), const((1, dm)), const((dm, LANES)), const((1, LANES))],
        out_specs=[pl.BlockSpec((tm, dm), lambda i: (i, 0)),
                   pl.BlockSpec((tm, dm // 2), lambda i: (i, 0)),
                   pl.BlockSpec((tm, LANES), lambda i: (i, 0))],
        out_shape=[jax.ShapeDtypeStruct((m, dm), F32), jax.ShapeDtypeStruct((m, dm // 2), jnp.uint32),
                   jax.ShapeDtypeStruct((m, LANES), F32)],
        compiler_params=_cparams("parallel"),
        name="mix_out_ln_route",
    )(x, merged, w_out, ln_g.reshape(1, dm), ln_b.reshape(1, dm), router_w, router_b)


MOE_TILE = 128
ROW_UNROLL = 8


def _moe_plan(route, n_tok):
    eid = route[:, :2].astype(jnp.int32).reshape(-1)
    onehot = (eid[:, None] == jnp.arange(N_EXPERTS, dtype=jnp.int32)[None, :]).astype(jnp.int32)
    csum = jnp.cumsum(onehot, axis=0)
    pos = jnp.sum(onehot * csum, axis=1) - 1
    padded = (csum[-1] + MOE_TILE - 1) // MOE_TILE * MOE_TILE
    ends = jnp.cumsum(padded)
    dest = (ends - padded)[eid] + pos
    n_rows = (2 * n_tok + MOE_TILE - 1) // MOE_TILE * MOE_TILE + N_EXPERTS * MOE_TILE
    src = jnp.zeros((n_rows,), jnp.int32).at[dest].set(jnp.arange(2 * n_tok, dtype=jnp.int32) // 2)
    tile_start = jnp.arange(n_rows // MOE_TILE, dtype=jnp.int32) * MOE_TILE
    tile_expert = jnp.minimum(jnp.searchsorted(ends, tile_start, side="right"), N_EXPERTS - 1).astype(jnp.int32)
    n_valid = (ends[-1] // MOE_TILE).astype(jnp.int32).reshape(1)
    return src, dest.astype(jnp.int32), tile_expert, n_valid


def _gather_kernel(src_ref, x_ref, o_ref, *, rows):
    base = pl.program_id(0) * rows

    def body(r, carry):
        o_ref[pl.ds(r, 1), :] = x_ref[pl.ds(src_ref[base + r], 1), :]
        return carry

    lax.fori_loop(0, rows, body, 0, unroll=ROW_UNROLL)


def _gather_rows(src, x_packed):
    n_rows = src.shape[0]
    rows = _pick(n_rows, (256, 128))
    grid_spec = pltpu.PrefetchScalarGridSpec(
        num_scalar_prefetch=1,
        grid=(n_rows // rows,),
        in_specs=[pl.BlockSpec(memory_space=pltpu.VMEM)],
        out_specs=pl.BlockSpec((rows, x_packed.shape[1]), lambda i, src: (i, 0)),
    )
    return pl.pallas_call(
        functools.partial(_gather_kernel, rows=rows),
        grid_spec=grid_spec,
        out_shape=jax.ShapeDtypeStruct((n_rows, x_packed.shape[1]), jnp.uint32),
        compiler_params=_cparams("arbitrary"),
        name="moe_gather",
    )(src, x_packed)


def _experts_kernel(te_ref, nv_ref, x_ref, wg_ref, wu_ref, wd_ref, ylo_ref, yhi_ref, wg_scr, wu_scr, wd_scr):
    i = pl.program_id(0)

    @pl.when((i == 0) | (te_ref[i] != te_ref[jnp.maximum(i - 1, 0)]))
    def _():
        wg_scr[...] = wg_ref[...].astype(BF16)
        wu_scr[...] = wu_ref[...].astype(BF16)
        wd_scr[...] = wd_ref[...].astype(BF16)

    @pl.when(i < nv_ref[0])
    def _():
        lo, hi = _unpack_bf16_pair(x_ref[...])
        x = jnp.concatenate([lo, hi], axis=1).astype(BF16)
        a = jnp.dot(x, wg_scr[...], preferred_element_type=F32)
        u = jnp.dot(x, wu_scr[...], preferred_element_type=F32)
        y = jnp.dot((_silu(a) * u).astype(BF16), wd_scr[...], preferred_element_type=F32)
        q = y.shape[1] // 4
        ylo_ref[...] = _pack_bf16_pair(y[:, :q], y[:, q:2 * q])
        yhi_ref[...] = _pack_bf16_pair(y[:, 2 * q:3 * q], y[:, 3 * q:])

    @pl.when(i >= nv_ref[0])
    def _():
        ylo_ref[...] = jnp.zeros_like(ylo_ref)
        yhi_ref[...] = jnp.zeros_like(yhi_ref)


def _experts(x_sorted, tile_expert, n_valid, w_gate, w_up, w_down):
    n_rows, half = x_sorted.shape
    dm = 2 * half
    _, _, ff = w_gate.shape
    grid_spec = pltpu.PrefetchScalarGridSpec(
        num_scalar_prefetch=2,
        grid=(n_rows // MOE_TILE,),
        in_specs=[pl.BlockSpec((MOE_TILE, half), lambda i, te, nv: (i, 0)),
                  pl.BlockSpec((None, dm, ff), lambda i, te, nv: (te[i], 0, 0)),
                  pl.BlockSpec((None, dm, ff), lambda i, te, nv: (te[i], 0, 0)),
                  pl.BlockSpec((None, ff, dm), lambda i, te, nv: (te[i], 0, 0))],
        out_specs=[pl.BlockSpec((MOE_TILE, dm // 4), lambda i, te, nv: (i, 0)),
                   pl.BlockSpec((MOE_TILE, dm // 4), lambda i, te, nv: (i, 0))],
        scratch_shapes=[pltpu.VMEM((dm, ff), BF16), pltpu.VMEM((dm, ff), BF16), pltpu.VMEM((ff, dm), BF16)],
    )
    return pl.pallas_call(
        _experts_kernel,
        grid_spec=grid_spec,
        out_shape=[jax.ShapeDtypeStruct((n_rows, dm // 4), jnp.uint32)] * 2,
        compiler_params=_cparams("arbitrary"),
        name="moe_experts",
    )(tile_expert, n_valid, x_sorted, w_gate, w_up, w_down)


def _combine_kernel(dest_ref, y_ref, route_ref, o_ref, u1_scr, u2_scr, *, rows):
    base = pl.program_id(0) * rows

    def body(t, carry):
        u1_scr[pl.ds(t, 1), :] = y_ref[pl.ds(dest_ref[2 * (base + t)], 1), :]
        u2_scr[pl.ds(t, 1), :] = y_ref[pl.ds(dest_ref[2 * (base + t) + 1], 1), :]
        return carry

    lax.fori_loop(0, rows, body, 0, unroll=ROW_UNROLL)
    route = route_ref[...]
    y1 = jnp.concatenate(_unpack_bf16_pair(u1_scr[...]), axis=1)
    y2 = jnp.concatenate(_unpack_bf16_pair(u2_scr[...]), axis=1)
    o_ref[...] = route[:, 2:3] * y1 + route[:, 3:4] * y2


def _combine(dest, y_half, route):
    n_tok = route.shape[0]
    rows = _pick(n_tok, (256, 128, 64, 32))
    width = y_half.shape[1]
    grid_spec = pltpu.PrefetchScalarGridSpec(
        num_scalar_prefetch=1,
        grid=(n_tok // rows,),
        in_specs=[pl.BlockSpec(memory_space=pltpu.VMEM),
                  pl.BlockSpec((rows, LANES), lambda i, dest: (i, 0))],
        out_specs=pl.BlockSpec((rows, 2 * width), lambda i, dest: (i, 0)),
        scratch_shapes=[pltpu.VMEM((rows, width), jnp.uint32), pltpu.VMEM((rows, width), jnp.uint32)],
    )
    return pl.pallas_call(
        functools.partial(_combine_kernel, rows=rows),
        grid_spec=grid_spec,
        out_shape=jax.ShapeDtypeStruct((n_tok, 2 * width), F32),
        compiler_params=_cparams("arbitrary"),
        name="moe_combine",
    )(dest, y_half, route)


def _ln2_kernel(h_ref, mlo_ref, mhi_ref, g_ref, b_ref, o_ref, *, alpha):
    moe = jnp.concatenate([mlo_ref[...], mhi_ref[...]], axis=1)
    o_ref[...] = _layer_norm(alpha * h_ref[...] + moe, g_ref[...], b_ref[...])


def _ln2(h, moe_lo, moe_hi, row_block0, ln_g, ln_b, alpha, tm):
    m, dm = h.shape
    return pl.pallas_call(
        functools.partial(_ln2_kernel, alpha=alpha),
        grid=(m // tm,),
        in_specs=[pl.BlockSpec((tm, dm), lambda i: (i, 0)),
                  pl.BlockSpec((tm, dm // 2), lambda i: (i + row_block0, 0)),
                  pl.BlockSpec((tm, dm // 2), lambda i: (i + row_block0, 0)),
                  pl.BlockSpec((1, dm), lambda i: (0, 0)),
                  pl.BlockSpec((1, dm), lambda i: (0, 0))],
        out_specs=pl.BlockSpec((tm, dm), lambda i: (i, 0)),
        out_shape=jax.ShapeDtypeStruct((m, dm), F32),
        compiler_params=_cparams("parallel"),
        name="moe_ln2",
    )(h, moe_lo, moe_hi, ln_g.reshape(1, dm), ln_b.reshape(1, dm))


def _moe(h_p, hpk_p, route_p, h_s, hpk_s, route_s, w_gate, w_up, w_down, ln_g, ln_b, alpha):
    m_p, m_s = h_p.shape[0], h_s.shape[0]
    n_tok = m_p + m_s
    assert m_p % m_s == 0 and m_s % 8 == 0
    n_pad = (n_tok + 255) // 256 * 256
    route = jnp.concatenate([route_p, route_s], axis=0)
    src, dest, tile_expert, n_valid = _moe_plan(route, n_tok)
    x_sorted = _gather_rows(src, jnp.concatenate([hpk_p, hpk_s], axis=0))
    y_lo, y_hi = _experts(x_sorted, tile_expert, n_valid, w_gate, w_up, w_down)
    route_pad = jnp.concatenate([route, jnp.zeros((n_pad - n_tok, LANES), F32)], axis=0)
    dest_pad = jnp.concatenate([dest, jnp.zeros((2 * (n_pad - n_tok),), jnp.int32)])
    moe_lo = _combine(dest_pad, y_lo, route_pad)
    moe_hi = _combine(dest_pad, y_hi, route_pad)
    out_p = _ln2(h_p, moe_lo, moe_hi, 0, ln_g, ln_b, alpha, _pick(m_p, (256, 128, 64, 32, 16, 8)))
    out_s = _ln2(h_s, moe_lo, moe_hi, m_p // m_s, ln_g, ln_b, alpha, m_s)
    return out_p, out_s


def _layer(hp_x, hs_x, cache_k, cache_v, page_table, state_conv, state_ssm, layer, w, alpha):
    batch, seq, dm = hp_x.shape
    dec_batch, dec_seq, _ = hs_x.shape
    assert dec_seq == 1
    sb_heads = w["sb_bias"].shape[0]
    gh = w["a_log"].shape[0]
    sbw = sb_heads * SB_HEAD_DIM
    gw = gh * GDN_HEAD_DIM
    cc = 3 * gw
    n_main = 3 * sbw + cc + gw
    assert sbw % LANES == 0 and 3 * sbw % cc == 0 and n_main % gw == 0 and sbw == gw
    qkvb_block = 3 * sbw // cc
    zb_block = (3 * sbw + cc) // gw
    w_in = w["w_in"]
    w_ba = w_in[:, n_main:n_main + 2 * gh]
    w_gates = w_in[:, n_main + 2 * gh:]
    router_w = jnp.concatenate(
        [jnp.transpose(w["router_expert"], (1, 0, 2)).reshape(dm, N_EXPERTS), w["router_group"],
         jnp.zeros((dm, LANES - N_EXPERTS - N_GROUPS), F32)], axis=1)
    router_b = jnp.concatenate(
        [w["router_expert_b"].reshape(N_EXPERTS), w["router_group_b"],
         jnp.zeros((LANES - N_EXPERTS - N_GROUPS,), F32)]).reshape(1, LANES)

    m = batch * seq
    xp = hp_x.reshape(m, dm)
    xp16 = xp.astype(BF16)
    tm = _pick(m, (1024, 512, 256, 128, 64, 32, 16, 8))
    z_main = _matmul(xp16, w_in, 0, n_main, tm, _pick(n_main, (512, 256, 128)), False)
    ba = _matmul(xp16, w_ba, 0, 2 * gh, tm, 2 * gh, False)
    y_a = _sb_prompt(z_main, w["sb_bias"], batch, seq, sbw, BF16)
    qkv_n = _gdn_prep(z_main, w["conv_w"], batch, seq, qkvb_block, gh)
    y_b, ssm_p = _gdn_prompt(qkv_n, ba, z_main, zb_block, w["a_log"], w["dt_bias"], w["gdn_norm_g"],
                             batch, seq, gh, BF16)
    merged = _mix_merge(xp16, y_a, y_b, w_gates.astype(BF16), w["w_proj_a"].astype(BF16), w["w_proj_b"].astype(BF16),
                        tm, _pick(dm, (512, 256, 128)), False)
    h1, hpk, route = _mix_out(xp, merged, w["w_out"].astype(BF16), w["ln1_g"], w["ln1_b"], router_w, router_b, alpha,
                              _pick(m, (256, 128, 64, 32, 16, 8)), False)
    k_p = z_main[:, sbw:2 * sbw].reshape(batch, seq, sb_heads, SB_HEAD_DIM)
    v_p = z_main[:, 2 * sbw:3 * sbw].reshape(batch, seq, sb_heads, SB_HEAD_DIM)
    conv_p = z_main[:, 3 * sbw:3 * sbw + cc].reshape(batch, seq, cc)[:, seq - (w["conv_w"].shape[0] - 1):, :]

    xs = hs_x.reshape(dec_batch, dm)
    zs = _matmul(xs, w_in, 0, n_main, dec_batch, _pick(n_main, (1024, 512, 256, 128)), True)
    ba_s = _matmul(xs, w_ba, 0, 2 * gh, dec_batch, 2 * gh, True)
    q_s = zs[:, :sbw].reshape(dec_batch, sb_heads, SB_HEAD_DIM)
    k_s = zs[:, sbw:2 * sbw].reshape(dec_batch, sb_heads, SB_HEAD_DIM)
    v_s = zs[:, 2 * sbw:3 * sbw].reshape(dec_batch, sb_heads, SB_HEAD_DIM)
    ya_s = _sb_sample(q_s, k_s, v_s, w["sb_bias"], cache_k, cache_v, layer, page_table).reshape(dec_batch, sbw)
    yb_s, conv_s, ssm_s = _gdn_sample(state_conv, zs.reshape(dec_batch, 1, n_main), ba_s.reshape(dec_batch, 1, 2 * gh),
                                      w["conv_w"], w["a_log"], w["dt_bias"], w["gdn_norm_g"], state_ssm,
                                      qkvb_block, zb_block, gh)
    merged_s = _mix_merge(xs, ya_s, yb_s.reshape(dec_batch, gw), w_gates, w["w_proj_a"], w["w_proj_b"],
                          dec_batch, _pick(dm, (512, 256, 128)), True)
    h1_s, hpk_s, route_s = _mix_out(xs, merged_s, w["w_out"], w["ln1_g"], w["ln1_b"], router_w, router_b, alpha,
                                    dec_batch, True)

    out_p, out_s = _moe(h1, hpk, route, h1_s, hpk_s, route_s, w["w_gate"], w["w_up"], w["w_down"],
                        w["ln2_g"], w["ln2_b"], alpha)

    return (out_p.reshape(batch, seq, dm), out_s.reshape(dec_batch, 1, dm), k_p, v_p,
            k_s.reshape(dec_batch, 1, sb_heads, SB_HEAD_DIM), v_s.reshape(dec_batch, 1, sb_heads, SB_HEAD_DIM),
            conv_p, conv_s, ssm_p, ssm_s)


def kernel(x_prompt, x_sample, cache_k, cache_v, page_table, state_conv, state_ssm, w_in, sb_bias, conv_w, a_log,
           dt_bias, gdn_norm_g, w_proj_a, w_proj_b, w_out, ln1_g, ln1_b, router_group, router_group_b,
           router_expert, router_expert_b, w_gate, w_up, w_down, ln2_g, ln2_b):
    depth = w_in.shape[0]
    alpha = (2.0 * depth) ** 0.25
    stacked = dict(w_in=w_in, sb_bias=sb_bias, conv_w=conv_w, a_log=a_log, dt_bias=dt_bias, gdn_norm_g=gdn_norm_g,
                   w_proj_a=w_proj_a, w_proj_b=w_proj_b, w_out=w_out, ln1_g=ln1_g, ln1_b=ln1_b,
                   router_group=router_group, router_group_b=router_group_b, router_expert=router_expert,
                   router_expert_b=router_expert_b, w_gate=w_gate, w_up=w_up, w_down=w_down, ln2_g=ln2_g, ln2_b=ln2_b)
    hp_x, hs_x = x_prompt, x_sample
    per_layer = []
    for layer in range(depth):
        w = {name: t[layer] for name, t in stacked.items()}
        outs = _layer(hp_x, hs_x, cache_k, cache_v, page_table, state_conv[layer], state_ssm[layer], layer, w, alpha)
        hp_x, hs_x = outs[0], outs[1]
        per_layer.append(outs[2:])
    return (hp_x, hs_x) + tuple(jnp.stack([o[i] for o in per_layer]) for i in range(8))
```

```python
import functools

import jax
import jax.numpy as jnp
from jax import lax
from jax.experimental import pallas as pl
from jax.experimental.pallas import tpu as pltpu

F32 = jnp.float32
BF16 = jnp.bfloat16

LANES = 128
SB_HEAD_DIM = 64
GDN_HEAD_DIM = 128
GDN_CHUNK = 128
LN_EPS = 1e-5
RMS_EPS = 1e-6
N_GROUPS = 4
EXPERTS_PER_GROUP = 8
N_EXPERTS = N_GROUPS * EXPERTS_PER_GROUP
NEG_BIG = -1e30
LOG2E = 1.4426950408889634
VMEM_LIMIT_BYTES = 58 * 1024 * 1024


def _cparams(*sem):
    return pltpu.CompilerParams(dimension_semantics=sem, vmem_limit_bytes=VMEM_LIMIT_BYTES)


def _pick(n, prefs):
    for p in prefs:
        if n % p == 0:
            return p
    return n


def _dot(a, b, hp=False):
    if hp:
        return jnp.dot(a.astype(F32), b.astype(F32), preferred_element_type=F32, precision=lax.Precision.HIGHEST)
    return jnp.dot(a.astype(BF16), b.astype(BF16), preferred_element_type=F32)


def _dot_nt(a, b):
    return lax.dot_general(a.astype(BF16), b.astype(BF16), (((1,), (1,)), ((), ())), preferred_element_type=F32)


def _dot_tn(a, b):
    return lax.dot_general(a.astype(BF16), b.astype(BF16), (((0,), (0,)), ((), ())), preferred_element_type=F32)


def _split2(a):
    hi = a.astype(BF16)
    lo = (a - hi.astype(F32)).astype(BF16)
    return hi, lo


def _split3(a):
    hi = a.astype(BF16)
    r = a - hi.astype(F32)
    mid = r.astype(BF16)
    lo = (r - mid.astype(F32)).astype(BF16)
    return hi, mid, lo


def _dot3(a, b):
    ah, al = _split2(a)
    bh, bl = _split2(b)
    d = functools.partial(jnp.dot, preferred_element_type=F32)
    return d(ah, bh) + d(ah, bl) + d(al, bh)


def _dot_exact_lhs(a_bf16, b):
    bh, bm, bl = _split3(b)
    d = functools.partial(jnp.dot, preferred_element_type=F32)
    return d(a_bf16, bh) + d(a_bf16, bm) + d(a_bf16, bl)


def _dot_exact_rhs(a, b_bf16):
    ah, am, al = _split3(a)
    d = functools.partial(jnp.dot, preferred_element_type=F32)
    return d(ah, b_bf16) + d(am, b_bf16) + d(al, b_bf16)


def _softplus(z):
    return jnp.maximum(z, 0.0) + jnp.log(1.0 + jnp.exp(-jnp.abs(z)))


def _sigmoid(z):
    return 1.0 / (1.0 + jnp.exp(-z))


def _silu(z):
    return z * _sigmoid(z)


def _layer_norm(t, g, b):
    mu = jnp.mean(t, axis=-1, keepdims=True)
    c = t - mu
    var = jnp.mean(c * c, axis=-1, keepdims=True)
    return c * lax.rsqrt(var + LN_EPS) * g + b


def _mm_kernel(x_ref, w_ref, o_ref, *, hp):
    o_ref[...] = _dot(x_ref[...], w_ref[...], hp)


def _matmul(x, w, col_block0, n_cols, tm, tn, hp):
    m, k = x.shape
    return pl.pallas_call(
        functools.partial(_mm_kernel, hp=hp),
        grid=(m // tm, n_cols // tn),
        in_specs=[pl.BlockSpec((tm, k), lambda i, j: (i, 0)),
                  pl.BlockSpec((k, tn), lambda i, j: (0, j + col_block0))],
        out_specs=pl.BlockSpec((tm, tn), lambda i, j: (i, j)),
        out_shape=jax.ShapeDtypeStruct((m, n_cols), F32),
        compiler_params=_cparams("parallel", "arbitrary"),
        name="proj_matmul",
    )(x, w)


def _sbp_kernel(bias_ref, q_ref, k_ref, v_ref, o_ref, *, tq, kb):
    hpair = pl.program_id(1)
    qi = pl.program_id(2)
    q = q_ref[...] * (SB_HEAD_DIM ** -0.5 * LOG2E)
    lane = lax.broadcasted_iota(jnp.int32, (1, LANES), 1)
    row = lax.broadcasted_iota(jnp.int32, (tq, kb), 0)
    col = lax.broadcasted_iota(jnp.int32, (tq, kb), 1)
    r2 = lax.broadcasted_iota(jnp.int32, (kb, kb), 0)
    c2 = lax.broadcasted_iota(jnp.int32, (kb, kb), 1)
    after = jnp.where(r2 > c2, 1.0, 0.0).astype(BF16)
    n_diag = tq // kb
    n_heads = LANES // SB_HEAD_DIM
    head_lanes = [(lane >= h * SB_HEAD_DIM) & (lane < (h + 1) * SB_HEAD_DIM) for h in range(n_heads)]
    q_stack = jnp.concatenate([jnp.where(m, q, 0.0) for m in head_lanes], axis=0).astype(BF16)
    bias = jnp.concatenate([jnp.full((tq, 1), bias_ref[hpair * n_heads + h] * LOG2E, F32) for h in range(n_heads)],
                           axis=0)
    row = jnp.concatenate([row] * n_heads, axis=0)
    col = jnp.concatenate([col] * n_heads, axis=0)

    def block(kstart, carry, masked):
        log_surv, acc = carry
        kblk = k_ref[pl.ds(kstart, kb), :]
        vblk = v_ref[pl.ds(kstart, kb), :]
        z = _dot_nt(q_stack, kblk) + bias
        sp = jnp.maximum(z, 0.0) + jnp.log2(1.0 + jnp.exp2(-jnp.abs(z)))
        if masked:
            visible = (kstart + col) < (qi * tq + row)
            sp = jnp.where(visible, sp, 0.0)
        later = jnp.dot(sp.astype(BF16), after, preferred_element_type=F32) + log_surv
        w = jnp.exp2(z - sp - later)
        if masked:
            w = jnp.where(visible, w, 0.0)
        w = w.astype(BF16)
        for h in range(n_heads):
            v_h = jnp.where(head_lanes[h], vblk, 0.0).astype(BF16)
            acc = acc + jnp.dot(w[h * tq:(h + 1) * tq], v_h, preferred_element_type=F32)
        log_surv = log_surv + jnp.sum(sp, axis=1, keepdims=True)
        return log_surv, acc

    carry = (jnp.zeros((n_heads * tq, 1), F32), jnp.zeros((tq, LANES), F32))
    for d in range(n_diag):
        kstart = pl.multiple_of(qi * tq + (n_diag - 1 - d) * kb, kb)
        carry = block(kstart, carry, True)
    n_before = qi * n_diag

    def body(i, carry):
        kstart = pl.multiple_of((n_before - 1 - i) * kb, kb)
        return block(kstart, carry, False)

    carry = lax.fori_loop(0, n_before, body, carry)
    o_ref[...] = carry[1].astype(o_ref.dtype)


def _sb_prompt(z_main, sb_bias, batch, seq, sbw, out_dtype):
    tq = _pick(seq, (512, 256, 128))
    kb = _pick(tq, (256, 128))
    nq = seq // tq
    n_pairs = sbw // LANES
    grid_spec = pltpu.PrefetchScalarGridSpec(
        num_scalar_prefetch=1,
        grid=(batch, n_pairs, nq),
        in_specs=[pl.BlockSpec((tq, LANES), lambda b, h, i, bias: (b * nq + i, h)),
                  pl.BlockSpec((seq, LANES), lambda b, h, i, bias: (b, n_pairs + h)),
                  pl.BlockSpec((seq, LANES), lambda b, h, i, bias: (b, 2 * n_pairs + h))],
        out_specs=pl.BlockSpec((tq, LANES), lambda b, h, i, bias: (b * nq + i, h)),
    )
    return pl.pallas_call(
        functools.partial(_sbp_kernel, tq=tq, kb=kb),
        grid_spec=grid_spec,
        out_shape=jax.ShapeDtypeStruct((batch * seq, sbw), out_dtype),
        compiler_params=_cparams("parallel", "parallel", "arbitrary"),
        name="sb_prompt",
    )(sb_bias, z_main, z_main, z_main)


def _sbs_kernel(pt_ref, q_ref, qb_ref, knew_ref, vnew_ref, bias_ref, *refs, n_slots, heads, page):
    k_refs = refs[:n_slots]
    v_refs = refs[n_slots:2 * n_slots]
    o_ref = refs[2 * n_slots]
    ls_ref, new_ref, acc_ref = refs[2 * n_slots + 1:]
    step = pl.program_id(1)
    d = SB_HEAD_DIM
    sub = 8
    bias = bias_ref[...]

    @pl.when(step == 0)
    def _():
        z_new = jnp.sum(q_ref[...] * knew_ref[...], axis=1, keepdims=True) * (d ** -0.5) + bias
        visible = jnp.zeros(z_new.shape, jnp.bool_)
        ls_ref[...] = jnp.broadcast_to(jnp.where(visible, _softplus(z_new), 0.0), ls_ref.shape)
        new_ref[...] = jnp.where(visible, jnp.exp(z_new - _softplus(z_new)), 0.0) * vnew_ref[...]
        acc_ref[...] = jnp.zeros_like(acc_ref)

    r2 = lax.broadcasted_iota(jnp.int32, (page, 2 * page), 0)
    c2 = lax.broadcasted_iota(jnp.int32, (page, 2 * page), 1)
    after_ones = jnp.where((r2 > c2) | (c2 >= page), 1.0, 0.0).astype(BF16)
    z_rows = [[None] * heads for _ in range(n_slots)]
    for h in range(heads):
        qh = qb_ref[h]
        for s in range(n_slots):
            prod = k_refs[s][h] * qh
            z_rows[s][h] = jnp.sum(prod, axis=0, keepdims=True)
    z = jnp.concatenate([r for rows in z_rows for r in rows], axis=0) + jnp.tile(bias, (n_slots, 1))
    sp = _softplus(z)
    cum_tot = _dot_exact_rhs(sp, after_ones)
    base = z - sp - cum_tot[:, :page]
    log_surv = ls_ref[...]
    w = []
    for s in range(n_slots):
        w.append(jnp.exp(base[s * heads:(s + 1) * heads] - log_surv))
        log_surv = log_surv + cum_tot[s * heads:(s + 1) * heads, page:]
    ls_ref[...] = log_surv
    for h in range(heads):
        a = acc_ref[h].reshape(d // sub, sub, page)
        for s in range(n_slots):
            w_row = jnp.broadcast_to(w[s][h:h + 1, :], (sub, page))
            a = a + v_refs[s][h].reshape(d // sub, sub, page) * w_row[None]
        acc_ref[h] = a.reshape(d, page)

    @pl.when(step == pl.num_programs(1) - 1)
    def _():
        o_ref[...] = new_ref[...] + jnp.sum(acc_ref[...], axis=-1)


def _sb_sample(q, k_new, v_new, sb_bias, cache_k, cache_v, layer, page_table):
    bsz, heads, d = q.shape
    page = cache_k.shape[2]
    assert d == SB_HEAD_DIM and page == LANES
    n_pages = page_table.shape[1]
    n_slots = _pick(n_pages, (8, 4, 2, 1))
    n_steps = n_pages // n_slots
    k_t = jnp.transpose(cache_k, (0, 1, 3, 4, 2))
    v_t = jnp.transpose(cache_v, (0, 1, 3, 4, 2))
    q_lanes = jnp.broadcast_to((q * (d ** -0.5))[..., None], (bsz, heads, d, page))

    def page_map(slot):
        def index_map(b, s, pt):
            return (layer, pt[b, n_pages - 1 - (s * n_slots + slot)], 0, 0, 0)
        return index_map

    row_spec = pl.BlockSpec((None, heads, d), lambda b, s, pt: (b, 0, 0))
    page_specs = [pl.BlockSpec((None, None, heads, d, page), page_map(slot)) for slot in range(n_slots)]
    grid_spec = pltpu.PrefetchScalarGridSpec(
        num_scalar_prefetch=1,
        grid=(bsz, n_steps),
        in_specs=[row_spec, pl.BlockSpec((None, heads, d, page), lambda b, s, pt: (b, 0, 0, 0)), row_spec, row_spec,
                  pl.BlockSpec((heads, 1), lambda b, s, pt: (0, 0))] + page_specs + page_specs,
        out_specs=row_spec,
        scratch_shapes=[pltpu.VMEM((heads, page), F32), pltpu.VMEM((heads, d), F32), pltpu.VMEM((heads, d, page), F32)],
    )
    return pl.pallas_call(
        functools.partial(_sbs_kernel, n_slots=n_slots, heads=heads, page=page),
        grid_spec=grid_spec,
        out_shape=jax.ShapeDtypeStruct((bsz, heads, d), F32),
        compiler_params=_cparams("parallel", "arbitrary"),
        name="sb_sample",
    )(page_table, q, q_lanes, k_new, v_new, sb_bias.reshape(heads, 1), *([k_t] * n_slots), *([v_t] * n_slots))


def _conv_heads(c, gh):
    outs = []
    for hh in range(3 * gh):
        x = c[:, hh * LANES:(hh + 1) * LANES]
        if hh < 2 * gh:
            x = x * lax.rsqrt(jnp.sum(x * x, axis=-1, keepdims=True) + RMS_EPS)
            if hh < gh:
                x = x * (GDN_HEAD_DIM ** -0.5)
        outs.append(x)
    return outs


def _gprep_kernel(prev_ref, cur_ref, cw_ref, o_ref, ext_ref, *, tb, gh, width):
    t = pl.program_id(1)
    ext_ref[8:, :] = cur_ref[...]
    ext_ref[0:8, :] = jnp.where(t == 0, 0.0, prev_ref[...])
    off = 8 - (width - 1)
    acc = ext_ref[off:off + tb, :] * cw_ref[0:1, :]
    for i in range(1, width):
        acc = acc + ext_ref[off + i:off + i + tb, :] * cw_ref[i:i + 1, :]
    c = _silu(acc)
    for hh, x in enumerate(_conv_heads(c, gh)):
        o_ref[:, hh * LANES:(hh + 1) * LANES] = x


def _gdn_prep(z_main, conv_w, batch, seq, col_block, gh):
    width, cc = conv_w.shape
    tb = _pick(seq, (256, 128, 64, 32, 16, 8))
    nt = seq // tb
    return pl.pallas_call(
        functools.partial(_gprep_kernel, tb=tb, gh=gh, width=width),
        grid=(batch, nt),
        in_specs=[pl.BlockSpec((8, cc), lambda b, t: (jnp.maximum(b * (seq // 8) + t * (tb // 8) - 1, 0), col_block)),
                  pl.BlockSpec((tb, cc), lambda b, t: (b * nt + t, col_block)),
                  pl.BlockSpec((width, cc), lambda b, t: (0, 0))],
        out_specs=pl.BlockSpec((tb, cc), lambda b, t: (b * nt + t, 0)),
        out_shape=jax.ShapeDtypeStruct((batch * seq, cc), F32),
        scratch_shapes=[pltpu.VMEM((tb + 8, cc), F32)],
        compiler_params=_cparams("parallel", "arbitrary"),
        name="gdn_prep",
    )(z_main, z_main, conv_w)


def _unit_lower_inverses(mats):
    n = mats[0].shape[0]
    eye = jnp.where(lax.broadcasted_iota(jnp.int32, (n, n), 0) == lax.broadcasted_iota(jnp.int32, (n, n), 1), 1.0, 0.0)
    ps = [eye - a for a in mats]
    pws = [_dot3(a, a) for a in mats]
    size = 2
    while size < n:
        ps = [p + _dot3(p, pw) for p, pw in zip(ps, pws)]
        size *= 2
        if size < n:
            pws = [_dot3(pw, pw) for pw in pws]
    return ps


def _gdn_kernel(qkv_ref, ba_ref, zb_ref, alog_ref, dtb_ref, ng_ref, y_ref, sfin_ref, s_scr, *, gh):
    c = pl.program_id(1)
    n = GDN_CHUNK
    gw = gh * GDN_HEAD_DIM

    @pl.when(c == 0)
    def _():
        s_scr[...] = jnp.zeros_like(s_scr)

    row = lax.broadcasted_iota(jnp.int32, (n, n), 0)
    col = lax.broadcasted_iota(jnp.int32, (n, n), 1)
    tri = row >= col
    strict = row > col
    lmat = jnp.where(tri, 1.0, 0.0).astype(BF16)
    ba = ba_ref[...]
    heads = range(gh)
    q = [qkv_ref[:, h * LANES:(h + 1) * LANES] for h in heads]
    k = [qkv_ref[:, gw + h * LANES:gw + (h + 1) * LANES] for h in heads]
    v = [qkv_ref[:, 2 * gw + h * LANES:2 * gw + (h + 1) * LANES] for h in heads]
    beta_b = [jnp.broadcast_to(_sigmoid(ba[:, h:h + 1]), (n, GDN_HEAD_DIM)) for h in heads]
    g_b = [jnp.broadcast_to(-jnp.exp(alog_ref[0:1, h:h + 1])
                            * _softplus(ba[:, gh + h:gh + h + 1] + dtb_ref[0:1, h:h + 1]), (n, n)) for h in heads]
    gm = [_dot_exact_lhs(lmat, jnp.concatenate([jnp.where(strict, g, 0.0), g], axis=1)) for g in g_b]
    decay = [jnp.where(tri, jnp.exp(m[:, :n]), 0.0) for m in gm]
    gc = [m[:, n:] for m in gm]
    e_gc = [jnp.exp(x) for x in gc]
    e_rest = [jnp.exp(x[n - 1:n, :] - x) for x in gc]
    g_last = [jnp.exp(x[n - 1:n, :]) for x in gc]
    kbeta = [a * b for a, b in zip(k, beta_b)]
    a_low = [jnp.where(strict, _dot_nt(kb_, k_) * d, 0.0) for kb_, k_, d in zip(kbeta, k, decay)]
    qk = [jnp.where(tri, _dot_nt(q_, k_) * d, 0.0) for q_, k_, d in zip(q, k, decay)]
    t_inv = _unit_lower_inverses(a_low)
    sol = [_dot3(t, jnp.concatenate([v_ * b, kb_ * e], axis=1))
           for t, v_, b, kb_, e in zip(t_inv, v, beta_b, kbeta, e_gc)]
    s = [s_scr[h] for h in heads]
    v_new = [x[:, :GDN_HEAD_DIM] - _dot(x[:, GDN_HEAD_DIM:], s_) for x, s_ in zip(sol, s)]
    o = [_dot(q_ * e, s_) + _dot(qk_, vn) for q_, e, s_, qk_, vn in zip(q, e_gc, s, qk, v_new)]
    for h in heads:
        s_scr[h] = s[h] * g_last[h] + _dot_tn(k[h] * e_rest[h], v_new[h])
    for h in heads:
        on = o[h] * lax.rsqrt(jnp.mean(o[h] * o[h], axis=-1, keepdims=True) + RMS_EPS) * ng_ref[...]
        y_ref[:, h * LANES:(h + 1) * LANES] = (on * _silu(zb_ref[:, h * LANES:(h + 1) * LANES])).astype(y_ref.dtype)

    @pl.when(c == pl.num_programs(1) - 1)
    def _():
        sfin_ref[...] = s_scr[...]


def _gdn_prompt(qkv_n, ba, z_main, zb_col_block, a_log, dt_bias, norm_g, batch, seq, gh, out_dtype):
    assert seq % GDN_CHUNK == 0 and GDN_CHUNK == GDN_HEAD_DIM == LANES
    gw = gh * GDN_HEAD_DIM
    nc = seq // GDN_CHUNK
    return pl.pallas_call(
        functools.partial(_gdn_kernel, gh=gh),
        grid=(batch, nc),
        in_specs=[pl.BlockSpec((GDN_CHUNK, 3 * gw), lambda b, c: (b * nc + c, 0)),
                  pl.BlockSpec((GDN_CHUNK, 2 * gh), lambda b, c: (b * nc + c, 0)),
                  pl.BlockSpec((GDN_CHUNK, gw), lambda b, c: (b * nc + c, zb_col_block)),
                  pl.BlockSpec((1, gh), lambda b, c: (0, 0)),
                  pl.BlockSpec((1, gh), lambda b, c: (0, 0)),
                  pl.BlockSpec((1, GDN_HEAD_DIM), lambda b, c: (0, 0))],
        out_specs=[pl.BlockSpec((GDN_CHUNK, gw), lambda b, c: (b * nc + c, 0)),
                   pl.BlockSpec((None, gh, GDN_HEAD_DIM, GDN_HEAD_DIM), lambda b, c: (b, 0, 0, 0))],
        out_shape=[jax.ShapeDtypeStruct((batch * seq, gw), out_dtype),
                   jax.ShapeDtypeStruct((batch, gh, GDN_HEAD_DIM, GDN_HEAD_DIM), F32)],
        scratch_shapes=[pltpu.VMEM((gh, GDN_HEAD_DIM, GDN_HEAD_DIM), F32)],
        compiler_params=_cparams("parallel", "arbitrary"),
        name="gdn_prompt",
    )(qkv_n, ba, z_main, a_log.reshape(1, gh), dt_bias.reshape(1, gh), norm_g.reshape(1, GDN_HEAD_DIM))


def _gdns_kernel(sc_ref, zrow_ref, zb_ref, ba_ref, cw_ref, alog_ref, dtb_ref, ng_ref, s_ref,
                 y_ref, cnew_ref, snew_ref, *, gh, width):
    rows = [sc_ref[i:i + 1, :] for i in range(width - 1)] + [zrow_ref[...]]
    acc = rows[0] * cw_ref[0:1, :]
    for i in range(1, width):
        acc = acc + rows[i] * cw_ref[i:i + 1, :]
    for i in range(width - 1):
        cnew_ref[i:i + 1, :] = rows[i + 1]
    heads = _conv_heads(_silu(acc), gh)
    n = GDN_HEAD_DIM
    eye = lax.broadcasted_iota(jnp.int32, (n, n), 0) == lax.broadcasted_iota(jnp.int32, (n, n), 1)

    def column(r):
        return jnp.sum(jnp.where(eye, jnp.broadcast_to(r, (n, n)), 0.0), axis=1, keepdims=True)

    ba = ba_ref[...]
    for h in range(gh):
        q, k, v = heads[h], heads[gh + h], heads[2 * gh + h]
        beta = _sigmoid(ba[:, h:h + 1])
        g = -jnp.exp(alog_ref[0:1, h:h + 1]) * _softplus(ba[:, gh + h:gh + h + 1] + dtb_ref[0:1, h:h + 1])
        eg = jnp.exp(g)
        kcol = column(k)
        s = s_ref[h]
        ks = jnp.sum(kcol * s, axis=0, keepdims=True)
        v_new = beta * v - (beta * eg) * ks
        s_new = s * eg + kcol * v_new
        snew_ref[h] = s_new
        o = jnp.sum(column(q) * s_new, axis=0, keepdims=True)
        o = o * lax.rsqrt(jnp.mean(o * o, axis=-1, keepdims=True) + RMS_EPS) * ng_ref[...]
        y_ref[:, h * LANES:(h + 1) * LANES] = o * _silu(zb_ref[:, h * LANES:(h + 1) * LANES])


def _gdn_sample(state_conv, z3, ba3, conv_w, a_log, dt_bias, norm_g, state_ssm, qkvb_col_block, zb_col_block, gh):
    bsz, wm1, cc = state_conv.shape
    width = wm1 + 1
    gw = gh * GDN_HEAD_DIM
    small = lambda shape: pl.BlockSpec(shape, lambda b: (0,) * len(shape))
    return pl.pallas_call(
        functools.partial(_gdns_kernel, gh=gh, width=width),
        grid=(bsz,),
        in_specs=[pl.BlockSpec((None, wm1, cc), lambda b: (b, 0, 0)),
                  pl.BlockSpec((None, 1, cc), lambda b: (b, 0, qkvb_col_block)),
                  pl.BlockSpec((None, 1, gw), lambda b: (b, 0, zb_col_block)),
                  pl.BlockSpec((None, 1, 2 * gh), lambda b: (b, 0, 0)),
                  small((width, cc)), small((1, gh)), small((1, gh)), small((1, GDN_HEAD_DIM)),
                  pl.BlockSpec((None, gh, GDN_HEAD_DIM, GDN_HEAD_DIM), lambda b: (b, 0, 0, 0))],
        out_specs=[pl.BlockSpec((None, 1, gw), lambda b: (b, 0, 0)),
                   pl.BlockSpec((None, wm1, cc), lambda b: (b, 0, 0)),
                   pl.BlockSpec((None, gh, GDN_HEAD_DIM, GDN_HEAD_DIM), lambda b: (b, 0, 0, 0))],
        out_shape=[jax.ShapeDtypeStruct((bsz, 1, gw), F32),
                   jax.ShapeDtypeStruct((bsz, wm1, cc), F32),
                   jax.ShapeDtypeStruct((bsz, gh, GDN_HEAD_DIM, GDN_HEAD_DIM), F32)],
        compiler_params=_cparams("parallel"),
        name="gdn_sample",
    )(state_conv, z3, z3, ba3, conv_w, a_log.reshape(1, gh), dt_bias.reshape(1, gh),
      norm_g.reshape(1, GDN_HEAD_DIM), state_ssm)


def _mixa_kernel(x_ref, ya_ref, yb_ref, wga_ref, wgb_ref, wpa_ref, wpb_ref, o_ref, *, hp):
    x = x_ref[...]
    ga = _dot(x, wga_ref[...], hp)
    gb = _dot(x, wgb_ref[...], hp)
    pa = _dot(ya_ref[...], wpa_ref[...], hp)
    pb = _dot(yb_ref[...], wpb_ref[...], hp)
    o_ref[...] = (_sigmoid(ga) * pa + _sigmoid(gb) * pb).astype(o_ref.dtype)


def _mix_merge(x, ya, yb, w_gates, w_pa, w_pb, tm, tn, hp):
    m, dm = x.shape
    nj = dm // tn
    return pl.pallas_call(
        functools.partial(_mixa_kernel, hp=hp),
        grid=(m // tm, nj),
        in_specs=[pl.BlockSpec((tm, dm), lambda i, j: (i, 0)),
                  pl.BlockSpec((tm, ya.shape[1]), lambda i, j: (i, 0)),
                  pl.BlockSpec((tm, yb.shape[1]), lambda i, j: (i, 0)),
                  pl.BlockSpec((dm, tn), lambda i, j: (0, j)),
                  pl.BlockSpec((dm, tn), lambda i, j: (0, nj + j)),
                  pl.BlockSpec((w_pa.shape[0], tn), lambda i, j: (0, j)),
                  pl.BlockSpec((w_pb.shape[0], tn), lambda i, j: (0, j))],
        out_specs=pl.BlockSpec((tm, tn), lambda i, j: (i, j)),
        out_shape=jax.ShapeDtypeStruct((m, dm), F32),
        compiler_params=_cparams("parallel", "arbitrary"),
        name="mix_merge",
    )(x, ya, yb, w_gates, w_gates, w_pa, w_pb)


def _route(logits):
    lane = lax.broadcasted_iota(jnp.int32, logits.shape, 1).astype(F32)
    is_group = (lane >= N_EXPERTS) & (lane < N_EXPERTS + N_GROUPS)
    gl = jnp.where(is_group, logits, NEG_BIG)
    gmax = jnp.max(gl, axis=1, keepdims=True)
    gsel = jnp.min(jnp.where(gl == gmax, lane, 1e9), axis=1, keepdims=True) - N_EXPERTS
    p_group = 1.0 / jnp.sum(jnp.where(is_group, jnp.exp(gl - gmax), 0.0), axis=1, keepdims=True)
    in_group = (lane >= gsel * EXPERTS_PER_GROUP) & (lane < (gsel + 1.0) * EXPERTS_PER_GROUP)
    el = jnp.where(in_group, logits, NEG_BIG)
    m1 = jnp.max(el, axis=1, keepdims=True)
    i1 = jnp.min(jnp.where(el == m1, lane, 1e9), axis=1, keepdims=True)
    el2 = jnp.where(lane == i1, NEG_BIG, el)
    m2 = jnp.max(el2, axis=1, keepdims=True)
    i2 = jnp.min(jnp.where(el2 == m2, lane, 1e9), axis=1, keepdims=True)
    e2 = jnp.exp(m2 - m1)
    g1 = p_group / (1.0 + e2)
    g2 = p_group * e2 / (1.0 + e2)
    return jnp.where(lane == 0.0, i1, jnp.where(lane == 1.0, i2, jnp.where(lane == 2.0, g1,
                                                                            jnp.where(lane == 3.0, g2, 0.0))))


HI16 = 0xFFFF0000


def _pack_bf16_pair(lo, hi):
    lo_bits = lax.bitcast_convert_type(lo.astype(BF16).astype(F32), jnp.uint32)
    hi_bits = lax.bitcast_convert_type(hi.astype(BF16).astype(F32), jnp.uint32)
    return hi_bits | (lo_bits >> 16)


def _unpack_bf16_pair(u):
    lo = lax.bitcast_convert_type(u << 16, F32)
    hi = lax.bitcast_convert_type(u & jnp.uint32(HI16), F32)
    return lo, hi


def _mixb_kernel(x_ref, m_ref, wout_ref, g_ref, b_ref, rw_ref, rb_ref, h_ref, hpk_ref, route_ref, *, alpha, hp):
    t = alpha * x_ref[...] + _dot(m_ref[...], wout_ref[...], hp)
    h = _layer_norm(t, g_ref[...], b_ref[...])
    h_ref[...] = h
    half = h.shape[1] // 2
    hpk_ref[...] = _pack_bf16_pair(h[:, :half], h[:, half:])
    if hp:
        logits = _dot(h, rw_ref[...], True) + rb_ref[...]
    else:
        logits = _dot3(h, rw_ref[...]) + rb_ref[...]
    route_ref[...] = _route(logits)


def _mix_out(x, merged, w_out, ln_g, ln_b, router_w, router_b, alpha, tm, hp):
    m, dm = x.shape
    const = lambda shape: pl.BlockSpec(shape, lambda i: (0,) * len(shape))
    return pl.pallas_call(
        functools.partial(_mixb_kernel, alpha=alpha, hp=hp),
        grid=(m // tm,),
        in_specs=[pl.BlockSpec((tm, dm), lambda i: (i, 0)),
                  pl.BlockSpec((tm, dm), lambda i: (i, 0)),
                  const((dm, dm)), const((1, dm)), const((1, dm)), const((dm, LANES)), const((1, LANES))],
        out_specs=[pl.BlockSpec((tm, dm), lambda i: (i, 0)),
                   pl.BlockSpec((tm, dm // 2), lambda i: (i, 0)),
                   pl.BlockSpec((tm, LANES), lambda i: (i, 0))],
        out_shape=[jax.ShapeDtypeStruct((m, dm), F32), jax.ShapeDtypeStruct((m, dm // 2), jnp.uint32),
                   jax.ShapeDtypeStruct((m, LANES), F32)],
        compiler_params=_cparams("parallel"),
        name="mix_out_ln_route",
    )(x, merged, w_out, ln_g.reshape(1, dm), ln_b.reshape(1, dm), router_w, router_b)


MOE_TILE = 128
ROW_UNROLL = 8


def _moe_plan(route, n_tok):
    eid = route[:, :2].astype(jnp.int32).reshape(-1)
    onehot = (eid[:, None] == jnp.arange(N_EXPERTS, dtype=jnp.int32)[None, :]).astype(jnp.int32)
    csum = jnp.cumsum(onehot, axis=0)
    pos = jnp.sum(onehot * csum, axis=1) - 1
    padded = (csum[-1] + MOE_TILE - 1) // MOE_TILE * MOE_TILE
    ends = jnp.cumsum(padded)
    dest = (ends - padded)[eid] + pos
    n_rows = (2 * n_tok + MOE_TILE - 1) // MOE_TILE * MOE_TILE + N_EXPERTS * MOE_TILE
    src = jnp.zeros((n_rows,), jnp.int32).at[dest].set(jnp.arange(2 * n_tok, dtype=jnp.int32) // 2)
    tile_start = jnp.arange(n_rows // MOE_TILE, dtype=jnp.int32) * MOE_TILE
    tile_expert = jnp.minimum(jnp.searchsorted(ends, tile_start, side="right"), N_EXPERTS - 1).astype(jnp.int32)
    n_valid = (ends[-1] // MOE_TILE).astype(jnp.int32).reshape(1)
    return src, dest.astype(jnp.int32), tile_expert, n_valid


def _gather_kernel(src_ref, x_ref, o_ref, *, rows):
    base = pl.program_id(0) * rows

    def body(r, carry):
        o_ref[pl.ds(r, 1), :] = x_ref[pl.ds(src_ref[base + r], 1), :]
        return carry

    lax.fori_loop(0, rows, body, 0, unroll=ROW_UNROLL)


def _gather_rows(src, x_packed):
    n_rows = src.shape[0]
    rows = _pick(n_rows, (256, 128))
    grid_spec = pltpu.PrefetchScalarGridSpec(
        num_scalar_prefetch=1,
        grid=(n_rows // rows,),
        in_specs=[pl.BlockSpec(memory_space=pltpu.VMEM)],
        out_specs=pl.BlockSpec((rows, x_packed.shape[1]), lambda i, src: (i, 0)),
    )
    return pl.pallas_call(
        functools.partial(_gather_kernel, rows=rows),
        grid_spec=grid_spec,
        out_shape=jax.ShapeDtypeStruct((n_rows, x_packed.shape[1]), jnp.uint32),
        compiler_params=_cparams("arbitrary"),
        name="moe_gather",
    )(src, x_packed)


def _experts_kernel(te_ref, nv_ref, x_ref, wg_ref, wu_ref, wd_ref, ylo_ref, yhi_ref, wg_scr, wu_scr, wd_scr):
    i = pl.program_id(0)

    @pl.when((i == 0) | (te_ref[i] != te_ref[jnp.maximum(i - 1, 0)]))
    def _():
        wg_scr[...] = wg_ref[...].astype(BF16)
        wu_scr[...] = wu_ref[...].astype(BF16)
        wd_scr[...] = wd_ref[...].astype(BF16)

    @pl.when(i < nv_ref[0])
    def _():
        lo, hi = _unpack_bf16_pair(x_ref[...])
        x = jnp.concatenate([lo, hi], axis=1).astype(BF16)
        a = jnp.dot(x, wg_scr[...], preferred_element_type=F32)
        u = jnp.dot(x, wu_scr[...], preferred_element_type=F32)
        y = jnp.dot((_silu(a) * u).astype(BF16), wd_scr[...], preferred_element_type=F32)
        q = y.shape[1] // 4
        ylo_ref[...] = _pack_bf16_pair(y[:, :q], y[:, q:2 * q])
        yhi_ref[...] = _pack_bf16_pair(y[:, 2 * q:3 * q], y[:, 3 * q:])

    @pl.when(i >= nv_ref[0])
    def _():
        ylo_ref[...] = jnp.zeros_like(ylo_ref)
        yhi_ref[...] = jnp.zeros_like(yhi_ref)


def _experts(x_sorted, tile_expert, n_valid, w_gate, w_up, w_down):
    n_rows, half = x_sorted.shape
    dm = 2 * half
    _, _, ff = w_gate.shape
    grid_spec = pltpu.PrefetchScalarGridSpec(
        num_scalar_prefetch=2,
        grid=(n_rows // MOE_TILE,),
        in_specs=[pl.BlockSpec((MOE_TILE, half), lambda i, te, nv: (i, 0)),
                  pl.BlockSpec((None, dm, ff), lambda i, te, nv: (te[i], 0, 0)),
                  pl.BlockSpec((None, dm, ff), lambda i, te, nv: (te[i], 0, 0)),
                  pl.BlockSpec((None, ff, dm), lambda i, te, nv: (te[i], 0, 0))],
        out_specs=[pl.BlockSpec((MOE_TILE, dm // 4), lambda i, te, nv: (i, 0)),
                   pl.BlockSpec((MOE_TILE, dm // 4), lambda i, te, nv: (i, 0))],
        scratch_shapes=[pltpu.VMEM((dm, ff), BF16), pltpu.VMEM((dm, ff), BF16), pltpu.VMEM((ff, dm), BF16)],
    )
    return pl.pallas_call(
        _experts_kernel,
        grid_spec=grid_spec,
        out_shape=[jax.ShapeDtypeStruct((n_rows, dm // 4), jnp.uint32)] * 2,
        compiler_params=_cparams("arbitrary"),
        name="moe_experts",
    )(tile_expert, n_valid, x_sorted, w_gate, w_up, w_down)


def _combine_kernel(dest_ref, y_ref, route_ref, o_ref, u1_scr, u2_scr, *, rows):
    base = pl.program_id(0) * rows

    def body(t, carry):
        u1_scr[pl.ds(t, 1), :] = y_ref[pl.ds(dest_ref[2 * (base + t)], 1), :]
        u2_scr[pl.ds(t, 1), :] = y_ref[pl.ds(dest_ref[2 * (base + t) + 1], 1), :]
        return carry

    lax.fori_loop(0, rows, body, 0, unroll=ROW_UNROLL)
    route = route_ref[...]
    y1 = jnp.concatenate(_unpack_bf16_pair(u1_scr[...]), axis=1)
    y2 = jnp.concatenate(_unpack_bf16_pair(u2_scr[...]), axis=1)
    o_ref[...] = route[:, 2:3] * y1 + route[:, 3:4] * y2


def _combine(dest, y_half, route):
    n_tok = route.shape[0]
    rows = _pick(n_tok, (256, 128, 64, 32))
    width = y_half.shape[1]
    grid_spec = pltpu.PrefetchScalarGridSpec(
        num_scalar_prefetch=1,
        grid=(n_tok // rows,),
        in_specs=[pl.BlockSpec(memory_space=pltpu.VMEM),
                  pl.BlockSpec((rows, LANES), lambda i, dest: (i, 0))],
        out_specs=pl.BlockSpec((rows, 2 * width), lambda i, dest: (i, 0)),
        scratch_shapes=[pltpu.VMEM((rows, width), jnp.uint32), pltpu.VMEM((rows, width), jnp.uint32)],
    )
    return pl.pallas_call(
        functools.partial(_combine_kernel, rows=rows),
        grid_spec=grid_spec,
        out_shape=jax.ShapeDtypeStruct((n_tok, 2 * width), F32),
        compiler_params=_cparams("arbitrary"),
        name="moe_combine",
    )(dest, y_half, route)


def _ln2_kernel(h_ref, mlo_ref, mhi_ref, g_ref, b_ref, o_ref, *, alpha):
    moe = jnp.concatenate([mlo_ref[...], mhi_ref[...]], axis=1)
    o_ref[...] = _layer_norm(alpha * h_ref[...] + moe, g_ref[...], b_ref[...])


def _ln2(h, moe_lo, moe_hi, row_block0, ln_g, ln_b, alpha, tm):
    m, dm = h.shape
    return pl.pallas_call(
        functools.partial(_ln2_kernel, alpha=alpha),
        grid=(m // tm,),
        in_specs=[pl.BlockSpec((tm, dm), lambda i: (i, 0)),
                  pl.BlockSpec((tm, dm // 2), lambda i: (i + row_block0, 0)),
                  pl.BlockSpec((tm, dm // 2), lambda i: (i + row_block0, 0)),
                  pl.BlockSpec((1, dm), lambda i: (0, 0)),
                  pl.BlockSpec((1, dm), lambda i: (0, 0))],
        out_specs=pl.BlockSpec((tm, dm), lambda i: (i, 0)),
        out_shape=jax.ShapeDtypeStruct((m, dm), F32),
        compiler_params=_cparams("parallel"),
        name="moe_ln2",
    )(h, moe_lo, moe_hi, ln_g.reshape(1, dm), ln_b.reshape(1, dm))


def _moe(h_p, hpk_p, route_p, h_s, hpk_s, route_s, w_gate, w_up, w_down, ln_g, ln_b, alpha):
    m_p, m_s = h_p.shape[0], h_s.shape[0]
    n_tok = m_p + m_s
    assert m_p % m_s == 0 and m_s % 8 == 0
    n_pad = (n_tok + 255) // 256 * 256
    route = jnp.concatenate([route_p, route_s], axis=0)
    src, dest, tile_expert, n_valid = _moe_plan(route, n_tok)
    x_sorted = _gather_rows(src, jnp.concatenate([hpk_p, hpk_s], axis=0))
    y_lo, y_hi = _experts(x_sorted, tile_expert, n_valid, w_gate, w_up, w_down)
    route_pad = jnp.concatenate([route, jnp.zeros((n_pad - n_tok, LANES), F32)], axis=0)
    dest_pad = jnp.concatenate([dest, jnp.zeros((2 * (n_pad - n_tok),), jnp.int32)])
    moe_lo = _combine(dest_pad, y_lo, route_pad)
    moe_hi = _combine(dest_pad, y_hi, route_pad)
    out_p = _ln2(h_p, moe_lo, moe_hi, 0, ln_g, ln_b, alpha, _pick(m_p, (256, 128, 64, 32, 16, 8)))
    out_s = _ln2(h_s, moe_lo, moe_hi, m_p // m_s, ln_g, ln_b, alpha, m_s)
    return out_p, out_s


def _layer(hp_x, hs_x, cache_k, cache_v, page_table, state_conv, state_ssm, layer, w, alpha):
    batch, seq, dm = hp_x.shape
    dec_batch, dec_seq, _ = hs_x.shape
    assert dec_seq == 1
    sb_heads = w["sb_bias"].shape[0]
    gh = w["a_log"].shape[0]
    sbw = sb_heads * SB_HEAD_DIM
    gw = gh * GDN_HEAD_DIM
    cc = 3 * gw
    n_main = 3 * sbw + cc + gw
    assert sbw % LANES == 0 and 3 * sbw % cc == 0 and n_main % gw == 0 and sbw == gw
    qkvb_block = 3 * sbw // cc
    zb_block = (3 * sbw + cc) // gw
    w_in = w["w_in"]
    w_ba = w_in[:, n_main:n_main + 2 * gh]
    w_gates = w_in[:, n_main + 2 * gh:]
    router_w = jnp.concatenate(
        [jnp.transpose(w["router_expert"], (1, 0, 2)).reshape(dm, N_EXPERTS), w["router_group"],
         jnp.zeros((dm, LANES - N_EXPERTS - N_GROUPS), F32)], axis=1)
    router_b = jnp.concatenate(
        [w["router_expert_b"].reshape(N_EXPERTS), w["router_group_b"],
         jnp.zeros((LANES - N_EXPERTS - N_GROUPS,), F32)]).reshape(1, LANES)

    m = batch * seq
    xp = hp_x.reshape(m, dm)
    xp16 = xp.astype(BF16)
    tm = _pick(m, (1024, 512, 256, 128, 64, 32, 16, 8))
    z_main = _matmul(xp16, w_in, 0, n_main, tm, _pick(n_main, (512, 256, 128)), False)
    ba = _matmul(xp16, w_ba, 0, 2 * gh, tm, 2 * gh, False)
    y_a = _sb_prompt(z_main, w["sb_bias"], batch, seq, sbw, BF16)
    qkv_n = _gdn_prep(z_main, w["conv_w"], batch, seq, qkvb_block, gh)
    y_b, ssm_p = _gdn_prompt(qkv_n, ba, z_main, zb_block, w["a_log"], w["dt_bias"], w["gdn_norm_g"],
                             batch, seq, gh, BF16)
    merged = _mix_merge(xp16, y_a, y_b, w_gates.astype(BF16), w["w_proj_a"].astype(BF16), w["w_proj_b"].astype(BF16),
                        tm, _pick(dm, (512, 256, 128)), False)
    h1, hpk, route = _mix_out(xp, merged, w["w_out"].astype(BF16), w["ln1_g"], w["ln1_b"], router_w, router_b, alpha,
                              _pick(m, (256, 128, 64, 32, 16, 8)), False)
    k_p = z_main[:, sbw:2 * sbw].reshape(batch, seq, sb_heads, SB_HEAD_DIM)
    v_p = z_main[:, 2 * sbw:3 * sbw].reshape(batch, seq, sb_heads, SB_HEAD_DIM)
    conv_p = z_main[:, 3 * sbw:3 * sbw + cc].reshape(batch, seq, cc)[:, seq - (w["conv_w"].shape[0] - 1):, :]

    xs = hs_x.reshape(dec_batch, dm)
    zs = _matmul(xs, w_in, 0, n_main, dec_batch, _pick(n_main, (1024, 512, 256, 128)), True)
    ba_s = _matmul(xs, w_ba, 0, 2 * gh, dec_batch, 2 * gh, True)
    q_s = zs[:, :sbw].reshape(dec_batch, sb_heads, SB_HEAD_DIM)
    k_s = zs[:, sbw:2 * sbw].reshape(dec_batch, sb_heads, SB_HEAD_DIM)
    v_s = zs[:, 2 * sbw:3 * sbw].reshape(dec_batch, sb_heads, SB_HEAD_DIM)
    ya_s = _sb_sample(q_s, k_s, v_s, w["sb_bias"], cache_k, cache_v, layer, page_table).reshape(dec_batch, sbw)
    yb_s, conv_s, ssm_s = _gdn_sample(state_conv, zs.reshape(dec_batch, 1, n_main), ba_s.reshape(dec_batch, 1, 2 * gh),
                                      w["conv_w"], w["a_log"], w["dt_bias"], w["gdn_norm_g"], state_ssm,
                                      qkvb_block, zb_block, gh)
    merged_s = _mix_merge(xs, ya_s, yb_s.reshape(dec_batch, gw), w_gates, w["w_proj_a"], w["w_proj_b"],
                          dec_batch, _pick(dm, (512, 256, 128)), True)
    h1_s, hpk_s, route_s = _mix_out(xs, merged_s, w["w_out"], w["ln1_g"], w["ln1_b"], router_w, router_b, alpha,
                                    dec_batch, True)

    out_p, out_s = _moe(h1, hpk, route, h1_s, hpk_s, route_s, w["w_gate"], w["w_up"], w["w_down"],
                        w["ln2_g"], w["ln2_b"], alpha)

    return (out_p.reshape(batch, seq, dm), out_s.reshape(dec_batch, 1, dm), k_p, v_p,
            k_s.reshape(dec_batch, 1, sb_heads, SB_HEAD_DIM), v_s.reshape(dec_batch, 1, sb_heads, SB_HEAD_DIM),
            conv_p, conv_s, ssm_p, ssm_s)


def kernel(x_prompt, x_sample, cache_k, cache_v, page_table, state_conv, state_ssm, w_in, sb_bias, conv_w, a_log,
           dt_bias, gdn_norm_g, w_proj_a, w_proj_b, w_out, ln1_g, ln1_b, router_group, router_group_b,
           router_expert, router_expert_b, w_gate, w_up, w_down, ln2_g, ln2_b):
    depth = w_in.shape[0]
    alpha = (2.0 * depth) ** 0.25
    stacked = dict(w_in=w_in, sb_bias=sb_bias, conv_w=conv_w, a_log=a_log, dt_bias=dt_bias, gdn_norm_g=gdn_norm_g,
                   w_proj_a=w_proj_a, w_proj_b=w_proj_b, w_out=w_out, ln1_g=ln1_g, ln1_b=ln1_b,
                   router_group=router_group, router_group_b=router_group_b, router_expert=router_expert,
                   router_expert_b=router_expert_b, w_gate=w_gate, w_up=w_up, w_down=w_down, ln2_g=ln2_g, ln2_b=ln2_b)
    hp_x, hs_x = x_prompt, x_sample
    per_layer = []
    for layer in range(depth):
        w = {name: t[layer] for name, t in stacked.items()}
        outs = _layer(hp_x, hs_x, cache_k, cache_v, page_table, state_conv[layer], state_ssm[layer], layer, w, alpha)
        hp_x, hs_x = outs[0], outs[1]
        per_layer.append(outs[2:])
    return (hp_x, hs_x) + tuple(jnp.stack([o[i] for o in per_layer]) for i in range(8))
```

```python
import functools

import jax
import jax.numpy as jnp
from jax import lax
from jax.experimental import pallas as pl
from jax.experimental.pallas import tpu as pltpu

F32 = jnp.float32
BF16 = jnp.bfloat16

LANES = 128
SB_HEAD_DIM = 64
GDN_HEAD_DIM = 128
GDN_CHUNK = 128
LN_EPS = 1e-5
RMS_EPS = 1e-6
N_GROUPS = 4
EXPERTS_PER_GROUP = 8
N_EXPERTS = N_GROUPS * EXPERTS_PER_GROUP
NEG_BIG = -1e30
LOG2E = 1.4426950408889634
VMEM_LIMIT_BYTES = 58 * 1024 * 1024


def _cparams(*sem):
    return pltpu.CompilerParams(dimension_semantics=sem, vmem_limit_bytes=VMEM_LIMIT_BYTES)


def _pick(n, prefs):
    for p in prefs:
        if n % p == 0:
            return p
    return n


def _dot(a, b, hp=False):
    if hp:
        return jnp.dot(a.astype(F32), b.astype(F32), preferred_element_type=F32, precision=lax.Precision.HIGHEST)
    return jnp.dot(a.astype(BF16), b.astype(BF16), preferred_element_type=F32)


def _dot_nt(a, b, hp=False):
    dims = (((1,), (1,)), ((), ()))
    if hp:
        return lax.dot_general(a.astype(F32), b.astype(F32), dims, preferred_element_type=F32,
                               precision=lax.Precision.HIGHEST)
    return lax.dot_general(a.astype(BF16), b.astype(BF16), dims, preferred_element_type=F32)


def _dot_tn(a, b):
    return lax.dot_general(a.astype(BF16), b.astype(BF16), (((0,), (0,)), ((), ())), preferred_element_type=F32)


def _split2(a):
    hi = a.astype(BF16)
    lo = (a - hi.astype(F32)).astype(BF16)
    return hi, lo


def _split3(a):
    hi = a.astype(BF16)
    r = a - hi.astype(F32)
    mid = r.astype(BF16)
    lo = (r - mid.astype(F32)).astype(BF16)
    return hi, mid, lo


def _dot3(a, b):
    ah, al = _split2(a)
    bh, bl = _split2(b)
    d = functools.partial(jnp.dot, preferred_element_type=F32)
    return d(ah, bh) + d(ah, bl) + d(al, bh)


def _dot_exact_lhs(a_bf16, b):
    bh, bm, bl = _split3(b)
    d = functools.partial(jnp.dot, preferred_element_type=F32)
    return d(a_bf16, bh) + d(a_bf16, bm) + d(a_bf16, bl)


def _dot_exact_rhs(a, b_bf16):
    ah, am, al = _split3(a)
    d = functools.partial(jnp.dot, preferred_element_type=F32)
    return d(ah, b_bf16) + d(am, b_bf16) + d(al, b_bf16)


def _softplus(z):
    return jnp.maximum(z, 0.0) + jnp.log(1.0 + jnp.exp(-jnp.abs(z)))


def _sigmoid(z):
    return 1.0 / (1.0 + jnp.exp(-z))


def _silu(z):
    return z * _sigmoid(z)


def _layer_norm(t, g, b):
    mu = jnp.mean(t, axis=-1, keepdims=True)
    c = t - mu
    var = jnp.mean(c * c, axis=-1, keepdims=True)
    return c * lax.rsqrt(var + LN_EPS) * g + b


def _mm_kernel(x_ref, wt_ref, o_ref, *, hp):
    o_ref[...] = _dot_nt(x_ref[...], wt_ref[...], hp)


def _matmul(x, w_t, n_cols, tm, tn, hp):
    m, k = x.shape
    return pl.pallas_call(
        functools.partial(_mm_kernel, hp=hp),
        grid=(m // tm, n_cols // tn),
        in_specs=[pl.BlockSpec((tm, k), lambda i, j: (i, 0)),
                  pl.BlockSpec((tn, k), lambda i, j: (j, 0))],
        out_specs=pl.BlockSpec((tm, tn), lambda i, j: (i, j)),
        out_shape=jax.ShapeDtypeStruct((m, n_cols), F32),
        compiler_params=_cparams("parallel", "arbitrary"),
        name="proj_matmul",
    )(x, w_t)


def _sbp_kernel(bias_ref, q_ref, k_ref, v_ref, o_ref, kt_ref, vt_ref, *, tq, kb):
    hpair = pl.program_id(1)
    qi = pl.program_id(2)
    q = q_ref[...] * (SB_HEAD_DIM ** -0.5 * LOG2E)
    lane = lax.broadcasted_iota(jnp.int32, (1, LANES), 1)
    row = lax.broadcasted_iota(jnp.int32, (tq, kb), 0)
    col = lax.broadcasted_iota(jnp.int32, (tq, kb), 1)
    r2 = lax.broadcasted_iota(jnp.int32, (kb, kb), 0)
    c2 = lax.broadcasted_iota(jnp.int32, (kb, kb), 1)
    after = jnp.where(r2 > c2, 1.0, 0.0).astype(BF16)
    n_diag = tq // kb
    n_heads = LANES // SB_HEAD_DIM
    head_lanes = [(lane >= h * SB_HEAD_DIM) & (lane < (h + 1) * SB_HEAD_DIM) for h in range(n_heads)]
    q_stack = jnp.concatenate([jnp.where(m, q, 0.0) for m in head_lanes], axis=0).astype(BF16)
    bias = jnp.concatenate([jnp.full((tq, 1), bias_ref[hpair * n_heads + h] * LOG2E, F32) for h in range(n_heads)],
                           axis=0)
    row = jnp.concatenate([row] * n_heads, axis=0)
    col = jnp.concatenate([col] * n_heads, axis=0)

    def block(kstart, carry, masked):
        log_surv, acc = carry
        kblk = k_ref[pl.ds(kstart, kb), :]
        vblk = v_ref[pl.ds(kstart, kb), :]
        z = _dot_nt(q_stack, kblk) + bias
        sp = jnp.maximum(z, 0.0) + jnp.log2(1.0 + jnp.exp2(-jnp.abs(z)))
        if masked:
            visible = (kstart + col) < (qi * tq + row)
            sp = jnp.where(visible, sp, 0.0)
        later = jnp.dot(sp.astype(BF16), after, preferred_element_type=F32) + log_surv
        w = jnp.exp2(z - sp - later)
        if masked:
            w = jnp.where(visible, w, 0.0)
        w = w.astype(BF16)
        for h in range(n_heads):
            v_h = jnp.where(head_lanes[h], vblk, 0.0).astype(BF16)
            acc = acc + jnp.dot(w[h * tq:(h + 1) * tq], v_h, preferred_element_type=F32)
        log_surv = log_surv + jnp.sum(sp, axis=1, keepdims=True)
        return log_surv, acc

    carry = (jnp.zeros((n_heads * tq, 1), F32), jnp.zeros((tq, LANES), F32))
    for d in range(n_diag):
        kstart = pl.multiple_of(qi * tq + (n_diag - 1 - d) * kb, kb)
        carry = block(kstart, carry, True)
    n_before = qi * n_diag

    def body(i, carry):
        kstart = pl.multiple_of((n_before - 1 - i) * kb, kb)
        return block(kstart, carry, False)

    carry = lax.fori_loop(0, n_before, body, carry)
    o_ref[...] = carry[1].astype(o_ref.dtype)

    @pl.when(qi == 0)
    def _():
        for start in range(0, k_ref.shape[0], LANES):
            kt_ref[:, start:start + LANES] = k_ref[start:start + LANES, :].T
            vt_ref[:, start:start + LANES] = v_ref[start:start + LANES, :].T


def _sb_prompt(z_main, sb_bias, batch, seq, sbw, out_dtype):
    tq = _pick(seq, (512, 256, 128))
    kb = _pick(tq, (256, 128))
    nq = seq // tq
    n_pairs = sbw // LANES
    grid_spec = pltpu.PrefetchScalarGridSpec(
        num_scalar_prefetch=1,
        grid=(batch, n_pairs, nq),
        in_specs=[pl.BlockSpec((tq, LANES), lambda b, h, i, bias: (b * nq + i, h)),
                  pl.BlockSpec((seq, LANES), lambda b, h, i, bias: (b, n_pairs + h)),
                  pl.BlockSpec((seq, LANES), lambda b, h, i, bias: (b, 2 * n_pairs + h))],
        out_specs=[pl.BlockSpec((tq, LANES), lambda b, h, i, bias: (b * nq + i, h)),
                   pl.BlockSpec((None, LANES, seq), lambda b, h, i, bias: (b, h, 0)),
                   pl.BlockSpec((None, LANES, seq), lambda b, h, i, bias: (b, h, 0))],
    )
    return pl.pallas_call(
        functools.partial(_sbp_kernel, tq=tq, kb=kb),
        grid_spec=grid_spec,
        out_shape=[jax.ShapeDtypeStruct((batch * seq, sbw), out_dtype),
                   jax.ShapeDtypeStruct((batch, sbw, seq), F32),
                   jax.ShapeDtypeStruct((batch, sbw, seq), F32)],
        compiler_params=_cparams("parallel", "parallel", "arbitrary"),
        name="sb_prompt",
    )(sb_bias, z_main, z_main, z_main)


def _sbs_kernel(pt_ref, q_ref, qb_ref, knew_ref, vnew_ref, bias_ref, *refs, n_slots, heads, page):
    k_refs = refs[:n_slots]
    v_refs = refs[n_slots:2 * n_slots]
    o_ref = refs[2 * n_slots]
    ls_ref, new_ref, acc_ref = refs[2 * n_slots + 1:]
    step = pl.program_id(1)
    d = SB_HEAD_DIM
    sub = 8
    bias = bias_ref[...]

    @pl.when(step == 0)
    def _():
        z_new = jnp.sum(q_ref[...] * knew_ref[...], axis=1, keepdims=True) * (d ** -0.5) + bias
        visible = jnp.zeros(z_new.shape, jnp.bool_)
        ls_ref[...] = jnp.broadcast_to(jnp.where(visible, _softplus(z_new), 0.0), ls_ref.shape)
        new_ref[...] = jnp.where(visible, jnp.exp(z_new - _softplus(z_new)), 0.0) * vnew_ref[...]
        acc_ref[...] = jnp.zeros_like(acc_ref)

    r2 = lax.broadcasted_iota(jnp.int32, (page, 2 * page), 0)
    c2 = lax.broadcasted_iota(jnp.int32, (page, 2 * page), 1)
    after_ones = jnp.where((r2 > c2) | (c2 >= page), 1.0, 0.0).astype(BF16)
    z_rows = [[None] * heads for _ in range(n_slots)]
    for h in range(heads):
        qh = qb_ref[h]
        for s in range(n_slots):
            prod = k_refs[s][h] * qh
            z_rows[s][h] = jnp.sum(prod, axis=0, keepdims=True)
    z = jnp.concatenate([r for rows in z_rows for r in rows], axis=0) + jnp.tile(bias, (n_slots, 1))
    sp = _softplus(z)
    cum_tot = _dot_exact_rhs(sp, after_ones)
    base = z - sp - cum_tot[:, :page]
    log_surv = ls_ref[...]
    w = []
    for s in range(n_slots):
        w.append(jnp.exp(base[s * heads:(s + 1) * heads] - log_surv))
        log_surv = log_surv + cum_tot[s * heads:(s + 1) * heads, page:]
    ls_ref[...] = log_surv
    for h in range(heads):
        a = acc_ref[h].reshape(d // sub, sub, page)
        for s in range(n_slots):
            w_row = jnp.broadcast_to(w[s][h:h + 1, :], (sub, page))
            a = a + v_refs[s][h].reshape(d // sub, sub, page) * w_row[None]
        acc_ref[h] = a.reshape(d, page)

    @pl.when(step == pl.num_programs(1) - 1)
    def _():
        o_ref[...] = new_ref[...] + jnp.sum(acc_ref[...], axis=-1)


def _sb_sample(q, k_new, v_new, sb_bias, cache_k, cache_v, layer, page_table):
    bsz, heads, d = q.shape
    page = cache_k.shape[2]
    assert d == SB_HEAD_DIM and page == LANES
    n_pages = page_table.shape[1]
    n_slots = _pick(n_pages, (8, 4, 2, 1))
    n_steps = n_pages // n_slots
    k_t = jnp.transpose(cache_k, (0, 1, 3, 4, 2))
    v_t = jnp.transpose(cache_v, (0, 1, 3, 4, 2))
    q_lanes = jnp.broadcast_to((q * (d ** -0.5))[..., None], (bsz, heads, d, page))

    def page_map(slot):
        def index_map(b, s, pt):
            return (layer, pt[b, n_pages - 1 - (s * n_slots + slot)], 0, 0, 0)
        return index_map

    row_spec = pl.BlockSpec((None, heads, d), lambda b, s, pt: (b, 0, 0))
    page_specs = [pl.BlockSpec((None, None, heads, d, page), page_map(slot)) for slot in range(n_slots)]
    grid_spec = pltpu.PrefetchScalarGridSpec(
        num_scalar_prefetch=1,
        grid=(bsz, n_steps),
        in_specs=[row_spec, pl.BlockSpec((None, heads, d, page), lambda b, s, pt: (b, 0, 0, 0)), row_spec, row_spec,
                  pl.BlockSpec((heads, 1), lambda b, s, pt: (0, 0))] + page_specs + page_specs,
        out_specs=row_spec,
        scratch_shapes=[pltpu.VMEM((heads, page), F32), pltpu.VMEM((heads, d), F32), pltpu.VMEM((heads, d, page), F32)],
    )
    return pl.pallas_call(
        functools.partial(_sbs_kernel, n_slots=n_slots, heads=heads, page=page),
        grid_spec=grid_spec,
        out_shape=jax.ShapeDtypeStruct((bsz, heads, d), F32),
        compiler_params=_cparams("parallel", "arbitrary"),
        name="sb_sample",
    )(page_table, q, q_lanes, k_new, v_new, sb_bias.reshape(heads, 1), *([k_t] * n_slots), *([v_t] * n_slots))


def _conv_heads(c, gh):
    outs = []
    for hh in range(3 * gh):
        x = c[:, hh * LANES:(hh + 1) * LANES]
        if hh < 2 * gh:
            x = x * lax.rsqrt(jnp.sum(x * x, axis=-1, keepdims=True) + RMS_EPS)
            if hh < gh:
                x = x * (GDN_HEAD_DIM ** -0.5)
        outs.append(x)
    return outs


def _gprep_kernel(prev_ref, cur_ref, cw_ref, o_ref, ext_ref, *, tb, gh, width):
    t = pl.program_id(1)
    ext_ref[8:, :] = cur_ref[...]
    ext_ref[0:8, :] = jnp.where(t == 0, 0.0, prev_ref[...])
    off = 8 - (width - 1)
    acc = ext_ref[off:off + tb, :] * cw_ref[0:1, :]
    for i in range(1, width):
        acc = acc + ext_ref[off + i:off + i + tb, :] * cw_ref[i:i + 1, :]
    c = _silu(acc)
    for hh, x in enumerate(_conv_heads(c, gh)):
        o_ref[:, hh * LANES:(hh + 1) * LANES] = x


def _gdn_prep(z_main, conv_w, batch, seq, col_block, gh):
    width, cc = conv_w.shape
    tb = _pick(seq, (256, 128, 64, 32, 16, 8))
    nt = seq // tb
    return pl.pallas_call(
        functools.partial(_gprep_kernel, tb=tb, gh=gh, width=width),
        grid=(batch, nt),
        in_specs=[pl.BlockSpec((8, cc), lambda b, t: (jnp.maximum(b * (seq // 8) + t * (tb // 8) - 1, 0), col_block)),
                  pl.BlockSpec((tb, cc), lambda b, t: (b * nt + t, col_block)),
                  pl.BlockSpec((width, cc), lambda b, t: (0, 0))],
        out_specs=pl.BlockSpec((tb, cc), lambda b, t: (b * nt + t, 0)),
        out_shape=jax.ShapeDtypeStruct((batch * seq, cc), F32),
        scratch_shapes=[pltpu.VMEM((tb + 8, cc), F32)],
        compiler_params=_cparams("parallel", "arbitrary"),
        name="gdn_prep",
    )(z_main, z_main, conv_w)


def _unit_lower_inverses(mats):
    n = mats[0].shape[0]
    eye = jnp.where(lax.broadcasted_iota(jnp.int32, (n, n), 0) == lax.broadcasted_iota(jnp.int32, (n, n), 1), 1.0, 0.0)
    ps = [eye - a for a in mats]
    pws = [_dot3(a, a) for a in mats]
    size = 2
    while size < n:
        ps = [p + _dot3(p, pw) for p, pw in zip(ps, pws)]
        size *= 2
        if size < n:
            pws = [_dot3(pw, pw) for pw in pws]
    return ps


def _gdn_kernel(qkv_ref, ba_ref, zb_ref, alog_ref, dtb_ref, ng_ref, y_ref, sfin_ref, s_scr, *, gh):
    c = pl.program_id(1)
    n = GDN_CHUNK
    gw = gh * GDN_HEAD_DIM

    @pl.when(c == 0)
    def _():
        s_scr[...] = jnp.zeros_like(s_scr)

    row = lax.broadcasted_iota(jnp.int32, (n, n), 0)
    col = lax.broadcasted_iota(jnp.int32, (n, n), 1)
    tri = row >= col
    strict = row > col
    lmat = jnp.where(tri, 1.0, 0.0).astype(BF16)
    ba = ba_ref[...]
    heads = range(gh)
    q = [qkv_ref[:, h * LANES:(h + 1) * LANES] for h in heads]
    k = [qkv_ref[:, gw + h * LANES:gw + (h + 1) * LANES] for h in heads]
    v = [qkv_ref[:, 2 * gw + h * LANES:2 * gw + (h + 1) * LANES] for h in heads]
    beta_b = [jnp.broadcast_to(_sigmoid(ba[:, h:h + 1]), (n, GDN_HEAD_DIM)) for h in heads]
    g_b = [jnp.broadcast_to(-jnp.exp(alog_ref[0:1, h:h + 1])
                            * _softplus(ba[:, gh + h:gh + h + 1] + dtb_ref[0:1, h:h + 1]), (n, n)) for h in heads]
    gm = [_dot_exact_lhs(lmat, jnp.concatenate([jnp.where(strict, g, 0.0), g], axis=1)) for g in g_b]
    decay = [jnp.where(tri, jnp.exp(m[:, :n]), 0.0) for m in gm]
    gc = [m[:, n:] for m in gm]
    e_gc = [jnp.exp(x) for x in gc]
    e_rest = [jnp.exp(x[n - 1:n, :] - x) for x in gc]
    g_last = [jnp.exp(x[n - 1:n, :]) for x in gc]
    kbeta = [a * b for a, b in zip(k, beta_b)]
    a_low = [jnp.where(strict, _dot_nt(kb_, k_) * d, 0.0) for kb_, k_, d in zip(kbeta, k, decay)]
    qk = [jnp.where(tri, _dot_nt(q_, k_) * d, 0.0) for q_, k_, d in zip(q, k, decay)]
    t_inv = _unit_lower_inverses(a_low)
    sol = [_dot3(t, jnp.concatenate([v_ * b, kb_ * e], axis=1))
           for t, v_, b, kb_, e in zip(t_inv, v, beta_b, kbeta, e_gc)]
    s = [s_scr[h] for h in heads]
    v_new = [x[:, :GDN_HEAD_DIM] - _dot(x[:, GDN_HEAD_DIM:], s_) for x, s_ in zip(sol, s)]
    o = [_dot(q_ * e, s_) + _dot(qk_, vn) for q_, e, s_, qk_, vn in zip(q, e_gc, s, qk, v_new)]
    for h in heads:
        s_scr[h] = s[h] * g_last[h] + _dot_tn(k[h] * e_rest[h], v_new[h])
    for h in heads:
        on = o[h] * lax.rsqrt(jnp.mean(o[h] * o[h], axis=-1, keepdims=True) + RMS_EPS) * ng_ref[...]
        y_ref[:, h * LANES:(h + 1) * LANES] = (on * _silu(zb_ref[:, h * LANES:(h + 1) * LANES])).astype(y_ref.dtype)

    @pl.when(c == pl.num_programs(1) - 1)
    def _():
        sfin_ref[...] = s_scr[...]


def _gdn_prompt(qkv_n, ba, z_main, zb_col_block, a_log, dt_bias, norm_g, batch, seq, gh, out_dtype):
    assert seq % GDN_CHUNK == 0 and GDN_CHUNK == GDN_HEAD_DIM == LANES
    gw = gh * GDN_HEAD_DIM
    nc = seq // GDN_CHUNK
    return pl.pallas_call(
        functools.partial(_gdn_kernel, gh=gh),
        grid=(batch, nc),
        in_specs=[pl.BlockSpec((GDN_CHUNK, 3 * gw), lambda b, c: (b * nc + c, 0)),
                  pl.BlockSpec((GDN_CHUNK, 2 * gh), lambda b, c: (b * nc + c, 0)),
                  pl.BlockSpec((GDN_CHUNK, gw), lambda b, c: (b * nc + c, zb_col_block)),
                  pl.BlockSpec((1, gh), lambda b, c: (0, 0)),
                  pl.BlockSpec((1, gh), lambda b, c: (0, 0)),
                  pl.BlockSpec((1, GDN_HEAD_DIM), lambda b, c: (0, 0))],
        out_specs=[pl.BlockSpec((GDN_CHUNK, gw), lambda b, c: (b * nc + c, 0)),
                   pl.BlockSpec((None, gh, GDN_HEAD_DIM, GDN_HEAD_DIM), lambda b, c: (b, 0, 0, 0))],
        out_shape=[jax.ShapeDtypeStruct((batch * seq, gw), out_dtype),
                   jax.ShapeDtypeStruct((batch, gh, GDN_HEAD_DIM, GDN_HEAD_DIM), F32)],
        scratch_shapes=[pltpu.VMEM((gh, GDN_HEAD_DIM, GDN_HEAD_DIM), F32)],
        compiler_params=_cparams("parallel", "arbitrary"),
        name="gdn_prompt",
    )(qkv_n, ba, z_main, a_log.reshape(1, gh), dt_bias.reshape(1, gh), norm_g.reshape(1, GDN_HEAD_DIM))


def _gdns_kernel(sc_ref, zrow_ref, zb_ref, ba_ref, cw_ref, alog_ref, dtb_ref, ng_ref, s_ref,
                 y_ref, cnew_ref, snew_ref, *, gh, width):
    rows = [sc_ref[i:i + 1, :] for i in range(width - 1)] + [zrow_ref[...]]
    acc = rows[0] * cw_ref[0:1, :]
    for i in range(1, width):
        acc = acc + rows[i] * cw_ref[i:i + 1, :]
    for i in range(width - 1):
        cnew_ref[i:i + 1, :] = rows[i + 1]
    heads = _conv_heads(_silu(acc), gh)
    n = GDN_HEAD_DIM
    eye = lax.broadcasted_iota(jnp.int32, (n, n), 0) == lax.broadcasted_iota(jnp.int32, (n, n), 1)

    def column(r):
        return jnp.sum(jnp.where(eye, jnp.broadcast_to(r, (n, n)), 0.0), axis=1, keepdims=True)

    ba = ba_ref[...]
    for h in range(gh):
        q, k, v = heads[h], heads[gh + h], heads[2 * gh + h]
        beta = _sigmoid(ba[:, h:h + 1])
        g = -jnp.exp(alog_ref[0:1, h:h + 1]) * _softplus(ba[:, gh + h:gh + h + 1] + dtb_ref[0:1, h:h + 1])
        eg = jnp.exp(g)
        kcol = column(k)
        s = s_ref[h]
        ks = jnp.sum(kcol * s, axis=0, keepdims=True)
        v_new = beta * v - (beta * eg) * ks
        s_new = s * eg + kcol * v_new
        snew_ref[h] = s_new
        o = jnp.sum(column(q) * s_new, axis=0, keepdims=True)
        o = o * lax.rsqrt(jnp.mean(o * o, axis=-1, keepdims=True) + RMS_EPS) * ng_ref[...]
        y_ref[:, h * LANES:(h + 1) * LANES] = o * _silu(zb_ref[:, h * LANES:(h + 1) * LANES])


def _gdn_sample(state_conv, z3, ba3, conv_w, a_log, dt_bias, norm_g, state_ssm, qkvb_col_block, zb_col_block, gh):
    bsz, wm1, cc = state_conv.shape
    width = wm1 + 1
    gw = gh * GDN_HEAD_DIM
    small = lambda shape: pl.BlockSpec(shape, lambda b: (0,) * len(shape))
    return pl.pallas_call(
        functools.partial(_gdns_kernel, gh=gh, width=width),
        grid=(bsz,),
        in_specs=[pl.BlockSpec((None, wm1, cc), lambda b: (b, 0, 0)),
                  pl.BlockSpec((None, 1, cc), lambda b: (b, 0, qkvb_col_block)),
                  pl.BlockSpec((None, 1, gw), lambda b: (b, 0, zb_col_block)),
                  pl.BlockSpec((None, 1, 2 * gh), lambda b: (b, 0, 0)),
                  small((width, cc)), small((1, gh)), small((1, gh)), small((1, GDN_HEAD_DIM)),
                  pl.BlockSpec((None, gh, GDN_HEAD_DIM, GDN_HEAD_DIM), lambda b: (b, 0, 0, 0))],
        out_specs=[pl.BlockSpec((None, 1, gw), lambda b: (b, 0, 0)),
                   pl.BlockSpec((None, wm1, cc), lambda b: (b, 0, 0)),
                   pl.BlockSpec((None, gh, GDN_HEAD_DIM, GDN_HEAD_DIM), lambda b: (b, 0, 0, 0))],
        out_shape=[jax.ShapeDtypeStruct((bsz, 1, gw), F32),
                   jax.ShapeDtypeStruct((bsz, wm1, cc), F32),
                   jax.ShapeDtypeStruct((bsz, gh, GDN_HEAD_DIM, GDN_HEAD_DIM), F32)],
        compiler_params=_cparams("parallel"),
        name="gdn_sample",
    )(state_conv, z3, z3, ba3, conv_w, a_log.reshape(1, gh), dt_bias.reshape(1, gh),
      norm_g.reshape(1, GDN_HEAD_DIM), state_ssm)


def _mixa_kernel(x_ref, ya_ref, yb_ref, wga_ref, wgb_ref, wpa_ref, wpb_ref, o_ref, *, hp):
    x = x_ref[...]
    ga = _dot_nt(x, wga_ref[...], hp)
    gb = _dot_nt(x, wgb_ref[...], hp)
    pa = _dot(ya_ref[...], wpa_ref[...], hp)
    pb = _dot(yb_ref[...], wpb_ref[...], hp)
    o_ref[...] = (_sigmoid(ga) * pa + _sigmoid(gb) * pb).astype(o_ref.dtype)


def _mix_merge(x, ya, yb, w_gates, w_pa, w_pb, tm, tn, hp):
    m, dm = x.shape
    nj = dm // tn
    return pl.pallas_call(
        functools.partial(_mixa_kernel, hp=hp),
        grid=(m // tm, nj),
        in_specs=[pl.BlockSpec((tm, dm), lambda i, j: (i, 0)),
                  pl.BlockSpec((tm, ya.shape[1]), lambda i, j: (i, 0)),
                  pl.BlockSpec((tm, yb.shape[1]), lambda i, j: (i, 0)),
                  pl.BlockSpec((tn, dm), lambda i, j: (j, 0)),
                  pl.BlockSpec((tn, dm), lambda i, j: (nj + j, 0)),
                  pl.BlockSpec((w_pa.shape[0], tn), lambda i, j: (0, j)),
                  pl.BlockSpec((w_pb.shape[0], tn), lambda i, j: (0, j))],
        out_specs=pl.BlockSpec((tm, tn), lambda i, j: (i, j)),
        out_shape=jax.ShapeDtypeStruct((m, dm), F32),
        compiler_params=_cparams("parallel", "arbitrary"),
        name="mix_merge",
    )(x, ya, yb, w_gates, w_gates, w_pa, w_pb)


def _route(logits):
    lane = lax.broadcasted_iota(jnp.int32, logits.shape, 1).astype(F32)
    is_group = (lane >= N_EXPERTS) & (lane < N_EXPERTS + N_GROUPS)
    gl = jnp.where(is_group, logits, NEG_BIG)
    gmax = jnp.max(gl, axis=1, keepdims=True)
    gsel = jnp.min(jnp.where(gl == gmax, lane, 1e9), axis=1, keepdims=True) - N_EXPERTS
    p_group = 1.0 / jnp.sum(jnp.where(is_group, jnp.exp(gl - gmax), 0.0), axis=1, keepdims=True)
    in_group = (lane >= gsel * EXPERTS_PER_GROUP) & (lane < (gsel + 1.0) * EXPERTS_PER_GROUP)
    el = jnp.where(in_group, logits, NEG_BIG)
    m1 = jnp.max(el, axis=1, keepdims=True)
    i1 = jnp.min(jnp.where(el == m1, lane, 1e9), axis=1, keepdims=True)
    el2 = jnp.where(lane == i1, NEG_BIG, el)
    m2 = jnp.max(el2, axis=1, keepdims=True)
    i2 = jnp.min(jnp.where(el2 == m2, lane, 1e9), axis=1, keepdims=True)
    e2 = jnp.exp(m2 - m1)
    g1 = p_group / (1.0 + e2)
    g2 = p_group * e2 / (1.0 + e2)
    return jnp.where(lane == 0.0, i1, jnp.where(lane == 1.0, i2, jnp.where(lane == 2.0, g1,
                                                                            jnp.where(lane == 3.0, g2, 0.0))))


HI16 = 0xFFFF0000


def _pack_bf16_pair(lo, hi):
    lo_bits = lax.bitcast_convert_type(lo.astype(BF16).astype(F32), jnp.uint32)
    hi_bits = lax.bitcast_convert_type(hi.astype(BF16).astype(F32), jnp.uint32)
    return hi_bits | (lo_bits >> 16)


def _unpack_bf16_pair(u):
    lo = lax.bitcast_convert_type(u << 16, F32)
    hi = lax.bitcast_convert_type(u & jnp.uint32(HI16), F32)
    return lo, hi


def _mixb_kernel(x_ref, m_ref, wout_ref, g_ref, b_ref, rw_ref, rb_ref, cnt0_ref,
                 h_ref, hpk_ref, route_ref, cnt_ref, cnt_scr, *, alpha, hp):
    @pl.when(pl.program_id(0) == 0)
    def _():
        cnt_scr[...] = cnt0_ref[...]

    t = alpha * x_ref[...] + _dot(m_ref[...], wout_ref[...], hp)
    h = _layer_norm(t, g_ref[...], b_ref[...])
    h_ref[...] = h
    half = h.shape[1] // 2
    hpk_ref[...] = _pack_bf16_pair(h[:, :half], h[:, half:])
    if hp:
        logits = _dot(h, rw_ref[...], True) + rb_ref[...]
    else:
        logits = _dot3(h, rw_ref[...]) + rb_ref[...]
    route = _route(logits)
    tm = h.shape[0]
    lane = lax.broadcasted_iota(jnp.int32, (tm, LANES), 1).astype(F32)
    first = jnp.where(lane == route[:, 0:1], 1.0, 0.0)
    second = jnp.where(lane == route[:, 1:2], 1.0, 0.0)
    earlier = jnp.where(lax.broadcasted_iota(jnp.int32, (tm, tm), 0) > lax.broadcasted_iota(jnp.int32, (tm, tm), 1),
                        1.0, 0.0).astype(BF16)
    n_first = jnp.sum(first, axis=0, keepdims=True)
    rank1 = jnp.dot(earlier, first.astype(BF16), preferred_element_type=F32) + cnt_scr[...]
    rank2 = jnp.dot(earlier, second.astype(BF16), preferred_element_type=F32) + cnt_scr[...] + n_first
    pos1 = jnp.sum(first * rank1, axis=1, keepdims=True)
    pos2 = jnp.sum(second * rank2, axis=1, keepdims=True)
    cnt_scr[...] = cnt_scr[...] + n_first + jnp.sum(second, axis=0, keepdims=True)
    cnt_ref[...] = cnt_scr[...]
    route_ref[...] = jnp.where(lane == 4.0, pos1, jnp.where(lane == 5.0, pos2, route))


def _mix_out(x, merged, w_out, ln_g, ln_b, router_w, router_b, counts0, alpha, tm, hp):
    m, dm = x.shape
    const = lambda shape: pl.BlockSpec(shape, lambda i: (0,) * len(shape))
    return pl.pallas_call(
        functools.partial(_mixb_kernel, alpha=alpha, hp=hp),
        grid=(m // tm,),
        in_specs=[pl.BlockSpec((tm, dm), lambda i: (i, 0)),
                  pl.BlockSpec((tm, dm), lambda i: (i, 0)),
                  const((dm, dm)), const((1, dm)), const((1, dm)), const((dm, LANES)), const((1, LANES)),
                  const((1, LANES))],
        out_specs=[pl.BlockSpec((tm, dm), lambda i: (i, 0)),
                   pl.BlockSpec((tm, dm // 2), lambda i: (i, 0)),
                   pl.BlockSpec((tm, LANES), lambda i: (i, 0)),
                   const((1, LANES))],
        out_shape=[jax.ShapeDtypeStruct((m, dm), F32), jax.ShapeDtypeStruct((m, dm // 2), jnp.uint32),
                   jax.ShapeDtypeStruct((m, LANES), F32), jax.ShapeDtypeStruct((1, LANES), F32)],
        scratch_shapes=[pltpu.VMEM((1, LANES), F32)],
        compiler_params=_cparams("arbitrary"),
        name="mix_out_ln_route",
    )(x, merged, w_out, ln_g.reshape(1, dm), ln_b.reshape(1, dm), router_w, router_b, counts0)


MOE_TILE = 128
ROW_UNROLL = 8


def _moe_plan(route, counts, n_tok):
    experts = jnp.arange(N_EXPERTS, dtype=jnp.int32)
    padded = (counts[0, :N_EXPERTS].astype(jnp.int32) + MOE_TILE - 1) // MOE_TILE * MOE_TILE
    ends = jnp.cumsum(padded)
    starts = ends - padded
    eid = route[:, :2].astype(jnp.int32)
    pos = route[:, 4:6].astype(jnp.int32)
    seg = jnp.sum(jnp.where(eid[..., None] == experts, starts, 0), axis=-1)
    dest = (seg + pos).reshape(-1)
    n_rows = (2 * n_tok + MOE_TILE - 1) // MOE_TILE * MOE_TILE + N_EXPERTS * MOE_TILE
    tile_start = jnp.arange(n_rows // MOE_TILE, dtype=jnp.int32) * MOE_TILE
    tile_expert = jnp.minimum(jnp.sum((ends[None, :] <= tile_start[:, None]).astype(jnp.int32), axis=1),
                              N_EXPERTS - 1)
    n_valid = (ends[-1] // MOE_TILE).astype(jnp.int32).reshape(1)
    return dest, tile_expert, n_valid, n_rows


def _gather_kernel(dest_ref, x_ref, o_ref, src_scr, *, rows, n_assign, n_rows):
    step = pl.program_id(0)

    @pl.when(step == 0)
    def _():
        def clear(r, carry):
            src_scr[r] = 0
            return carry

        def fill(a, carry):
            src_scr[dest_ref[a]] = a // 2
            return carry

        lax.fori_loop(0, n_rows, clear, 0, unroll=ROW_UNROLL)
        lax.fori_loop(0, n_assign, fill, 0, unroll=ROW_UNROLL)

    base = step * rows

    def body(r, carry):
        o_ref[pl.ds(r, 1), :] = x_ref[pl.ds(src_scr[base + r], 1), :]
        return carry

    lax.fori_loop(0, rows, body, 0, unroll=ROW_UNROLL)


def _gather_rows(dest, n_rows, x_packed):
    rows = _pick(n_rows, (256, 128))
    grid_spec = pltpu.PrefetchScalarGridSpec(
        num_scalar_prefetch=1,
        grid=(n_rows // rows,),
        in_specs=[pl.BlockSpec(memory_space=pltpu.VMEM)],
        out_specs=pl.BlockSpec((rows, x_packed.shape[1]), lambda i, dest: (i, 0)),
        scratch_shapes=[pltpu.SMEM((n_rows,), jnp.int32)],
    )
    return pl.pallas_call(
        functools.partial(_gather_kernel, rows=rows, n_assign=dest.shape[0], n_rows=n_rows),
        grid_spec=grid_spec,
        out_shape=jax.ShapeDtypeStruct((n_rows, x_packed.shape[1]), jnp.uint32),
        compiler_params=_cparams("arbitrary"),
        name="moe_gather",
    )(dest, x_packed)


def _experts_kernel(te_ref, nv_ref, x_ref, wg_ref, wu_ref, wd_ref, ylo_ref, yhi_ref, wg_scr, wu_scr, wd_scr):
    i = pl.program_id(0)

    @pl.when((i == 0) | (te_ref[i] != te_ref[jnp.maximum(i - 1, 0)]))
    def _():
        wg_scr[...] = wg_ref[...].astype(BF16)
        wu_scr[...] = wu_ref[...].astype(BF16)
        wd_scr[...] = wd_ref[...].astype(BF16)

    @pl.when(i < nv_ref[0])
    def _():
        lo, hi = _unpack_bf16_pair(x_ref[...])
        x = jnp.concatenate([lo, hi], axis=1).astype(BF16)
        a = jnp.dot(x, wg_scr[...], preferred_element_type=F32)
        u = jnp.dot(x, wu_scr[...], preferred_element_type=F32)
        y = jnp.dot((_silu(a) * u).astype(BF16), wd_scr[...], preferred_element_type=F32)
        q = y.shape[1] // 4
        ylo_ref[...] = _pack_bf16_pair(y[:, :q], y[:, q:2 * q])
        yhi_ref[...] = _pack_bf16_pair(y[:, 2 * q:3 * q], y[:, 3 * q:])

    @pl.when(i >= nv_ref[0])
    def _():
        ylo_ref[...] = jnp.zeros_like(ylo_ref)
        yhi_ref[...] = jnp.zeros_like(yhi_ref)


def _experts(x_sorted, tile_expert, n_valid, w_gate, w_up, w_down):
    n_rows, half = x_sorted.shape
    dm = 2 * half
    _, _, ff = w_gate.shape
    grid_spec = pltpu.PrefetchScalarGridSpec(
        num_scalar_prefetch=2,
        grid=(n_rows // MOE_TILE,),
        in_specs=[pl.BlockSpec((MOE_TILE, half), lambda i, te, nv: (i, 0)),
                  pl.BlockSpec((None, dm, ff), lambda i, te, nv: (te[i], 0, 0)),
                  pl.BlockSpec((None, dm, ff), lambda i, te, nv: (te[i], 0, 0)),
                  pl.BlockSpec((None, ff, dm), lambda i, te, nv: (te[i], 0, 0))],
        out_specs=[pl.BlockSpec((MOE_TILE, dm // 4), lambda i, te, nv: (i, 0)),
                   pl.BlockSpec((MOE_TILE, dm // 4), lambda i, te, nv: (i, 0))],
        scratch_shapes=[pltpu.VMEM((dm, ff), BF16), pltpu.VMEM((dm, ff), BF16), pltpu.VMEM((ff, dm), BF16)],
    )
    return pl.pallas_call(
        _experts_kernel,
        grid_spec=grid_spec,
        out_shape=[jax.ShapeDtypeStruct((n_rows, dm // 4), jnp.uint32)] * 2,
        compiler_params=_cparams("arbitrary"),
        name="moe_experts",
    )(tile_expert, n_valid, x_sorted, w_gate, w_up, w_down)


def _combine_kernel(dest_ref, y_ref, route_ref, o_ref, u1_scr, u2_scr, *, rows):
    base = pl.program_id(0) * rows

    def body(t, carry):
        u1_scr[pl.ds(t, 1), :] = y_ref[pl.ds(dest_ref[2 * (base + t)], 1), :]
        u2_scr[pl.ds(t, 1), :] = y_ref[pl.ds(dest_ref[2 * (base + t) + 1], 1), :]
        return carry

    lax.fori_loop(0, rows, body, 0, unroll=ROW_UNROLL)
    route = route_ref[...]
    y1 = jnp.concatenate(_unpack_bf16_pair(u1_scr[...]), axis=1)
    y2 = jnp.concatenate(_unpack_bf16_pair(u2_scr[...]), axis=1)
    o_ref[...] = route[:, 2:3] * y1 + route[:, 3:4] * y2


def _combine(dest, y_half, route):
    n_tok = route.shape[0]
    rows = _pick(n_tok, (256, 128, 64, 32))
    width = y_half.shape[1]
    grid_spec = pltpu.PrefetchScalarGridSpec(
        num_scalar_prefetch=1,
        grid=(n_tok // rows,),
        in_specs=[pl.BlockSpec(memory_space=pltpu.VMEM),
                  pl.BlockSpec((rows, LANES), lambda i, dest: (i, 0))],
        out_specs=pl.BlockSpec((rows, 2 * width), lambda i, dest: (i, 0)),
        scratch_shapes=[pltpu.VMEM((rows, width), jnp.uint32), pltpu.VMEM((rows, width), jnp.uint32)],
    )
    return pl.pallas_call(
        functools.partial(_combine_kernel, rows=rows),
        grid_spec=grid_spec,
        out_shape=jax.ShapeDtypeStruct((n_tok, 2 * width), F32),
        compiler_params=_cparams("arbitrary"),
        name="moe_combine",
    )(dest, y_half, route)


def _ln2_kernel(h_ref, mlo_ref, mhi_ref, g_ref, b_ref, o_ref, *, alpha):
    moe = jnp.concatenate([mlo_ref[...], mhi_ref[...]], axis=1)
    o_ref[...] = _layer_norm(alpha * h_ref[...] + moe, g_ref[...], b_ref[...])


def _ln2(h, moe_lo, moe_hi, row_block0, ln_g, ln_b, alpha, tm):
    m, dm = h.shape
    return pl.pallas_call(
        functools.partial(_ln2_kernel, alpha=alpha),
        grid=(m // tm,),
        in_specs=[pl.BlockSpec((tm, dm), lambda i: (i, 0)),
                  pl.BlockSpec((tm, dm // 2), lambda i: (i + row_block0, 0)),
                  pl.BlockSpec((tm, dm // 2), lambda i: (i + row_block0, 0)),
                  pl.BlockSpec((1, dm), lambda i: (0, 0)),
                  pl.BlockSpec((1, dm), lambda i: (0, 0))],
        out_specs=pl.BlockSpec((tm, dm), lambda i: (i, 0)),
        out_shape=jax.ShapeDtypeStruct((m, dm), F32),
        compiler_params=_cparams("parallel"),
        name="moe_ln2",
    )(h, moe_lo, moe_hi, ln_g.reshape(1, dm), ln_b.reshape(1, dm))


def _moe(h_p, hpk_p, route_p, h_s, hpk_s, route_s, counts, w_gate, w_up, w_down, ln_g, ln_b, alpha):
    m_p, m_s = h_p.shape[0], h_s.shape[0]
    n_tok = m_p + m_s
    assert m_p % m_s == 0 and m_s % 8 == 0
    n_pad = (n_tok + 255) // 256 * 256
    route = jnp.concatenate([route_p, route_s], axis=0)
    dest, tile_expert, n_valid, n_rows = _moe_plan(route, counts, n_tok)
    x_sorted = _gather_rows(dest, n_rows, jnp.concatenate([hpk_p, hpk_s], axis=0))
    y_lo, y_hi = _experts(x_sorted, tile_expert, n_valid, w_gate, w_up, w_down)
    route_pad = jnp.concatenate([route, jnp.zeros((n_pad - n_tok, LANES), F32)], axis=0)
    dest_pad = jnp.concatenate([dest, jnp.zeros((2 * (n_pad - n_tok),), jnp.int32)])
    moe_lo = _combine(dest_pad, y_lo, route_pad)
    moe_hi = _combine(dest_pad, y_hi, route_pad)
    out_p = _ln2(h_p, moe_lo, moe_hi, 0, ln_g, ln_b, alpha, _pick(m_p, (256, 128, 64, 32, 16, 8)))
    out_s = _ln2(h_s, moe_lo, moe_hi, m_p // m_s, ln_g, ln_b, alpha, m_s)
    return out_p, out_s


def _layer(hp_x, hs_x, cache_k, cache_v, page_table, state_conv, state_ssm, layer, w, alpha):
    batch, seq, dm = hp_x.shape
    dec_batch, dec_seq, _ = hs_x.shape
    assert dec_seq == 1
    sb_heads = w["sb_bias"].shape[0]
    gh = w["a_log"].shape[0]
    sbw = sb_heads * SB_HEAD_DIM
    gw = gh * GDN_HEAD_DIM
    cc = 3 * gw
    n_main = 3 * sbw + cc + gw
    assert sbw % LANES == 0 and 3 * sbw % cc == 0 and n_main % gw == 0 and sbw == gw
    qkvb_block = 3 * sbw // cc
    zb_block = (3 * sbw + cc) // gw
    w_t = jnp.transpose(w["w_in"])
    w_ba = w_t[n_main:n_main + 2 * gh]
    w_gates = w_t[n_main + 2 * gh:]
    router_w = jnp.concatenate(
        [jnp.transpose(w["router_expert"], (1, 0, 2)).reshape(dm, N_EXPERTS), w["router_group"],
         jnp.zeros((dm, LANES - N_EXPERTS - N_GROUPS), F32)], axis=1)
    router_b = jnp.concatenate(
        [w["router_expert_b"].reshape(N_EXPERTS), w["router_group_b"],
         jnp.zeros((LANES - N_EXPERTS - N_GROUPS,), F32)]).reshape(1, LANES)

    m = batch * seq
    xp = hp_x.reshape(m, dm)
    xp16 = xp.astype(BF16)
    tm = _pick(m, (1024, 512, 256, 128, 64, 32, 16, 8))
    z_main = _matmul(xp16, w_t, n_main, tm, _pick(n_main, (512, 256, 128)), False)
    ba = _matmul(xp16, w_ba, 2 * gh, tm, 2 * gh, False)
    y_a, k_t, v_t = _sb_prompt(z_main, w["sb_bias"], batch, seq, sbw, BF16)
    qkv_n = _gdn_prep(z_main, w["conv_w"], batch, seq, qkvb_block, gh)
    y_b, ssm_p = _gdn_prompt(qkv_n, ba, z_main, zb_block, w["a_log"], w["dt_bias"], w["gdn_norm_g"],
                             batch, seq, gh, BF16)
    merged = _mix_merge(xp16, y_a, y_b, w_gates.astype(BF16), w["w_proj_a"].astype(BF16), w["w_proj_b"].astype(BF16),
                        tm, _pick(dm, (512, 256, 128)), False)
    h1, hpk, route, counts = _mix_out(xp, merged, w["w_out"].astype(BF16), w["ln1_g"], w["ln1_b"], router_w, router_b,
                                      jnp.zeros((1, LANES), F32), alpha, _pick(m, (256, 128, 64, 32, 16, 8)), False)
    k_p = jnp.transpose(k_t.reshape(batch, sb_heads, SB_HEAD_DIM, seq), (0, 3, 1, 2))
    v_p = jnp.transpose(v_t.reshape(batch, sb_heads, SB_HEAD_DIM, seq), (0, 3, 1, 2))
    conv_p = z_main.reshape(batch, seq, n_main)[:, seq - (w["conv_w"].shape[0] - 1):, 3 * sbw:3 * sbw + cc]

    xs = hs_x.reshape(dec_batch, dm)
    zs = _matmul(xs, w_t, n_main, dec_batch, _pick(n_main, (1024, 512, 256, 128)), True)
    ba_s = _matmul(xs, w_ba, 2 * gh, dec_batch, 2 * gh, True)
    q_s = zs[:, :sbw].reshape(dec_batch, sb_heads, SB_HEAD_DIM)
    k_s = zs[:, sbw:2 * sbw].reshape(dec_batch, sb_heads, SB_HEAD_DIM)
    v_s = zs[:, 2 * sbw:3 * sbw].reshape(dec_batch, sb_heads, SB_HEAD_DIM)
    ya_s = _sb_sample(q_s, k_s, v_s, w["sb_bias"], cache_k, cache_v, layer, page_table).reshape(dec_batch, sbw)
    yb_s, conv_s, ssm_s = _gdn_sample(state_conv, zs.reshape(dec_batch, 1, n_main), ba_s.reshape(dec_batch, 1, 2 * gh),
                                      w["conv_w"], w["a_log"], w["dt_bias"], w["gdn_norm_g"], state_ssm,
                                      qkvb_block, zb_block, gh)
    merged_s = _mix_merge(xs, ya_s, yb_s.reshape(dec_batch, gw), w_gates, w["w_proj_a"], w["w_proj_b"],
                          dec_batch, _pick(dm, (512, 256, 128)), True)
    h1_s, hpk_s, route_s, counts = _mix_out(xs, merged_s, w["w_out"], w["ln1_g"], w["ln1_b"], router_w, router_b,
                                            counts, alpha, dec_batch, True)

    out_p, out_s = _moe(h1, hpk, route, h1_s, hpk_s, route_s, counts, w["w_gate"], w["w_up"], w["w_down"],
                        w["ln2_g"], w["ln2_b"], alpha)

    return (out_p.reshape(batch, seq, dm), out_s.reshape(dec_batch, 1, dm), k_p, v_p,
            k_s.reshape(dec_batch, 1, sb_heads, SB_HEAD_DIM), v_s.reshape(dec_batch, 1, sb_heads, SB_HEAD_DIM),
            conv_p, conv_s, ssm_p, ssm_s)


def kernel(x_prompt, x_sample, cache_k, cache_v, page_table, state_conv, state_ssm, w_in, sb_bias, conv_w, a_log,
           dt_bias, gdn_norm_g, w_proj_a, w_proj_b, w_out, ln1_g, ln1_b, router_group, router_group_b,
           router_expert, router_expert_b, w_gate, w_up, w_down, ln2_g, ln2_b):
    depth = w_in.shape[0]
    alpha = (2.0 * depth) ** 0.25
    stacked = dict(w_in=w_in, sb_bias=sb_bias, conv_w=conv_w, a_log=a_log, dt_bias=dt_bias, gdn_norm_g=gdn_norm_g,
                   w_proj_a=w_proj_a, w_proj_b=w_proj_b, w_out=w_out, ln1_g=ln1_g, ln1_b=ln1_b,
                   router_group=router_group, router_group_b=router_group_b, router_expert=router_expert,
                   router_expert_b=router_expert_b, w_gate=w_gate, w_up=w_up, w_down=w_down, ln2_g=ln2_g, ln2_b=ln2_b)
    hp_x, hs_x = x_prompt, x_sample
    per_layer = []
    for layer in range(depth):
        w = {name: t[layer] for name, t in stacked.items()}
        outs = _layer(hp_x, hs_x, cache_k, cache_v, page_table, state_conv[layer], state_ssm[layer], layer, w, alpha)
        hp_x, hs_x = outs[0], outs[1]
        per_layer.append(outs[2:])
    return (hp_x, hs_x) + tuple(jnp.stack([o[i] for o in per_layer]) for i in range(8))
```

```python
import functools

import jax
import jax.numpy as jnp
from jax import lax
from jax.experimental import pallas as pl
from jax.experimental.pallas import tpu as pltpu

F32 = jnp.float32
BF16 = jnp.bfloat16

LANES = 128
SB_HEAD_DIM = 64
GDN_HEAD_DIM = 128
GDN_CHUNK = 128
LN_EPS = 1e-5
RMS_EPS = 1e-6
N_GROUPS = 4
EXPERTS_PER_GROUP = 8
N_EXPERTS = N_GROUPS * EXPERTS_PER_GROUP
NEG_BIG = -1e30
LOG2E = 1.4426950408889634
VMEM_LIMIT_BYTES = 58 * 1024 * 1024


def _cparams(*sem):
    return pltpu.CompilerParams(dimension_semantics=sem, vmem_limit_bytes=VMEM_LIMIT_BYTES)


def _pick(n, prefs):
    for p in prefs:
        if n % p == 0:
            return p
    return n


def _dot(a, b, hp=False):
    if hp:
        return jnp.dot(a.astype(F32), b.astype(F32), preferred_element_type=F32, precision=lax.Precision.HIGHEST)
    return jnp.dot(a.astype(BF16), b.astype(BF16), preferred_element_type=F32)


def _dot_nt(a, b, hp=False):
    dims = (((1,), (1,)), ((), ()))
    if hp:
        return lax.dot_general(a.astype(F32), b.astype(F32), dims, preferred_element_type=F32,
                               precision=lax.Precision.HIGHEST)
    return lax.dot_general(a.astype(BF16), b.astype(BF16), dims, preferred_element_type=F32)


def _dot_tn(a, b):
    return lax.dot_general(a.astype(BF16), b.astype(BF16), (((0,), (0,)), ((), ())), preferred_element_type=F32)


def _split2(a):
    hi = a.astype(BF16)
    lo = (a - hi.astype(F32)).astype(BF16)
    return hi, lo


def _split3(a):
    hi = a.astype(BF16)
    r = a - hi.astype(F32)
    mid = r.astype(BF16)
    lo = (r - mid.astype(F32)).astype(BF16)
    return hi, mid, lo


def _dot3(a, b):
    ah, al = _split2(a)
    bh, bl = _split2(b)
    d = functools.partial(jnp.dot, preferred_element_type=F32)
    return d(ah, bh) + d(ah, bl) + d(al, bh)


def _dot_exact_lhs(a_bf16, b):
    bh, bm, bl = _split3(b)
    d = functools.partial(jnp.dot, preferred_element_type=F32)
    return d(a_bf16, bh) + d(a_bf16, bm) + d(a_bf16, bl)


def _dot_exact_rhs(a, b_bf16):
    ah, am, al = _split3(a)
    d = functools.partial(jnp.dot, preferred_element_type=F32)
    return d(ah, b_bf16) + d(am, b_bf16) + d(al, b_bf16)


def _softplus(z):
    return jnp.maximum(z, 0.0) + jnp.log(1.0 + jnp.exp(-jnp.abs(z)))


def _sigmoid(z):
    return 1.0 / (1.0 + jnp.exp(-z))


def _silu(z):
    return z * _sigmoid(z)


def _layer_norm(t, g, b):
    mu = jnp.mean(t, axis=-1, keepdims=True)
    c = t - mu
    var = jnp.mean(c * c, axis=-1, keepdims=True)
    return c * lax.rsqrt(var + LN_EPS) * g + b


def _mm_kernel(x_ref, wt_ref, o_ref, *, hp):
    o_ref[...] = _dot_nt(x_ref[...], wt_ref[...], hp)


def _matmul(x, w_t, n_cols, tm, tn, hp):
    m, k = x.shape
    return pl.pallas_call(
        functools.partial(_mm_kernel, hp=hp),
        grid=(m // tm, n_cols // tn),
        in_specs=[pl.BlockSpec((tm, k), lambda i, j: (i, 0)),
                  pl.BlockSpec((tn, k), lambda i, j: (j, 0))],
        out_specs=pl.BlockSpec((tm, tn), lambda i, j: (i, j)),
        out_shape=jax.ShapeDtypeStruct((m, n_cols), F32),
        compiler_params=_cparams("parallel", "arbitrary"),
        name="proj_matmul",
    )(x, w_t)


def _sbp_kernel(bias_ref, q_ref, k_ref, v_ref, o_ref, kt_ref, vt_ref, *, tq, kb):
    hpair = pl.program_id(1)
    qi = pl.program_id(2)
    q = q_ref[...] * (SB_HEAD_DIM ** -0.5 * LOG2E)
    lane = lax.broadcasted_iota(jnp.int32, (1, LANES), 1)
    row = lax.broadcasted_iota(jnp.int32, (tq, kb), 0)
    col = lax.broadcasted_iota(jnp.int32, (tq, kb), 1)
    r2 = lax.broadcasted_iota(jnp.int32, (kb, kb), 0)
    c2 = lax.broadcasted_iota(jnp.int32, (kb, kb), 1)
    after = jnp.where(r2 > c2, 1.0, 0.0).astype(BF16)
    n_diag = tq // kb
    n_heads = LANES // SB_HEAD_DIM
    head_lanes = [(lane >= h * SB_HEAD_DIM) & (lane < (h + 1) * SB_HEAD_DIM) for h in range(n_heads)]
    q_stack = jnp.concatenate([jnp.where(m, q, 0.0) for m in head_lanes], axis=0).astype(BF16)
    bias = jnp.concatenate([jnp.full((tq, 1), bias_ref[hpair * n_heads + h] * LOG2E, F32) for h in range(n_heads)],
                           axis=0)
    row = jnp.concatenate([row] * n_heads, axis=0)
    col = jnp.concatenate([col] * n_heads, axis=0)

    def block(kstart, carry, masked):
        log_surv, acc = carry
        kblk = k_ref[pl.ds(kstart, kb), :]
        vblk = v_ref[pl.ds(kstart, kb), :]
        z = _dot_nt(q_stack, kblk) + bias
        sp = jnp.maximum(z, 0.0) + jnp.log2(1.0 + jnp.exp2(-jnp.abs(z)))
        if masked:
            visible = (kstart + col) < (qi * tq + row)
            sp = jnp.where(visible, sp, 0.0)
        later = jnp.dot(sp.astype(BF16), after, preferred_element_type=F32) + log_surv
        w = jnp.exp2(z - sp - later)
        if masked:
            w = jnp.where(visible, w, 0.0)
        w = w.astype(BF16)
        for h in range(n_heads):
            v_h = jnp.where(head_lanes[h], vblk, 0.0).astype(BF16)
            acc = acc + jnp.dot(w[h * tq:(h + 1) * tq], v_h, preferred_element_type=F32)
        log_surv = log_surv + jnp.sum(sp, axis=1, keepdims=True)
        return log_surv, acc

    carry = (jnp.zeros((n_heads * tq, 1), F32), jnp.zeros((tq, LANES), F32))
    for d in range(n_diag):
        kstart = pl.multiple_of(qi * tq + (n_diag - 1 - d) * kb, kb)
        carry = block(kstart, carry, True)
    n_before = qi * n_diag

    def body(i, carry):
        for d in range(n_diag):
            kstart = pl.multiple_of((n_before - 1 - i * n_diag - d) * kb, kb)
            carry = block(kstart, carry, False)
        return carry

    carry = lax.fori_loop(0, qi, body, carry)
    o_ref[...] = carry[1].astype(o_ref.dtype)

    @pl.when(qi == 0)
    def _():
        for start in range(0, k_ref.shape[0], LANES):
            kt_ref[:, start:start + LANES] = k_ref[start:start + LANES, :].T
            vt_ref[:, start:start + LANES] = v_ref[start:start + LANES, :].T


def _sb_prompt(z_main, sb_bias, batch, seq, sbw, out_dtype):
    tq = _pick(seq, (512, 256, 128))
    kb = _pick(tq, (256, 128))
    nq = seq // tq
    n_pairs = sbw // LANES
    grid_spec = pltpu.PrefetchScalarGridSpec(
        num_scalar_prefetch=1,
        grid=(batch, n_pairs, nq),
        in_specs=[pl.BlockSpec((tq, LANES), lambda b, h, i, bias: (b * nq + i, h)),
                  pl.BlockSpec((seq, LANES), lambda b, h, i, bias: (b, n_pairs + h)),
                  pl.BlockSpec((seq, LANES), lambda b, h, i, bias: (b, 2 * n_pairs + h))],
        out_specs=[pl.BlockSpec((tq, LANES), lambda b, h, i, bias: (b * nq + i, h)),
                   pl.BlockSpec((None, LANES, seq), lambda b, h, i, bias: (b, h, 0)),
                   pl.BlockSpec((None, LANES, seq), lambda b, h, i, bias: (b, h, 0))],
    )
    return pl.pallas_call(
        functools.partial(_sbp_kernel, tq=tq, kb=kb),
        grid_spec=grid_spec,
        out_shape=[jax.ShapeDtypeStruct((batch * seq, sbw), out_dtype),
                   jax.ShapeDtypeStruct((batch, sbw, seq), F32),
                   jax.ShapeDtypeStruct((batch, sbw, seq), F32)],
        compiler_params=_cparams("parallel", "parallel", "arbitrary"),
        name="sb_prompt",
    )(sb_bias, z_main, z_main, z_main)


def _sbs_kernel(pt_ref, q_ref, qb_ref, knew_ref, vnew_ref, bias_ref, *refs, n_slots, heads, page):
    k_refs = refs[:n_slots]
    v_refs = refs[n_slots:2 * n_slots]
    o_ref = refs[2 * n_slots]
    ls_ref, new_ref, acc_ref = refs[2 * n_slots + 1:]
    step = pl.program_id(1)
    d = SB_HEAD_DIM
    sub = 8
    bias = bias_ref[...]

    @pl.when(step == 0)
    def _():
        z_new = jnp.sum(q_ref[...] * knew_ref[...], axis=1, keepdims=True) * (d ** -0.5) + bias
        visible = jnp.zeros(z_new.shape, jnp.bool_)
        ls_ref[...] = jnp.broadcast_to(jnp.where(visible, _softplus(z_new), 0.0), ls_ref.shape)
        new_ref[...] = jnp.where(visible, jnp.exp(z_new - _softplus(z_new)), 0.0) * vnew_ref[...]
        acc_ref[...] = jnp.zeros_like(acc_ref)

    r2 = lax.broadcasted_iota(jnp.int32, (page, 2 * page), 0)
    c2 = lax.broadcasted_iota(jnp.int32, (page, 2 * page), 1)
    after_ones = jnp.where((r2 > c2) | (c2 >= page), 1.0, 0.0).astype(BF16)
    z_rows = [[None] * heads for _ in range(n_slots)]
    for h in range(heads):
        qh = qb_ref[h]
        for s in range(n_slots):
            prod = k_refs[s][h] * qh
            z_rows[s][h] = jnp.sum(prod, axis=0, keepdims=True)
    z = jnp.concatenate([r for rows in z_rows for r in rows], axis=0) + jnp.tile(bias, (n_slots, 1))
    sp = _softplus(z)
    cum_tot = _dot_exact_rhs(sp, after_ones)
    base = z - sp - cum_tot[:, :page]
    log_surv = ls_ref[...]
    w = []
    for s in range(n_slots):
        w.append(jnp.exp(base[s * heads:(s + 1) * heads] - log_surv))
        log_surv = log_surv + cum_tot[s * heads:(s + 1) * heads, page:]
    ls_ref[...] = log_surv
    for h in range(heads):
        a = acc_ref[h].reshape(d // sub, sub, page)
        for s in range(n_slots):
            w_row = jnp.broadcast_to(w[s][h:h + 1, :], (sub, page))
            a = a + v_refs[s][h].reshape(d // sub, sub, page) * w_row[None]
        acc_ref[h] = a.reshape(d, page)

    @pl.when(step == pl.num_programs(1) - 1)
    def _():
        o_ref[...] = new_ref[...] + jnp.sum(acc_ref[...], axis=-1)


def _sb_sample(q, k_new, v_new, sb_bias, cache_k, cache_v, layer, page_table):
    bsz, heads, d = q.shape
    page = cache_k.shape[2]
    assert d == SB_HEAD_DIM and page == LANES
    n_pages = page_table.shape[1]
    n_slots = _pick(n_pages, (8, 4, 2, 1))
    n_steps = n_pages // n_slots
    k_t = jnp.transpose(cache_k, (0, 1, 3, 4, 2))
    v_t = jnp.transpose(cache_v, (0, 1, 3, 4, 2))
    q_lanes = jnp.broadcast_to((q * (d ** -0.5))[..., None], (bsz, heads, d, page))

    def page_map(slot):
        def index_map(b, s, pt):
            return (layer, pt[b, n_pages - 1 - (s * n_slots + slot)], 0, 0, 0)
        return index_map

    row_spec = pl.BlockSpec((None, heads, d), lambda b, s, pt: (b, 0, 0))
    page_specs = [pl.BlockSpec((None, None, heads, d, page), page_map(slot)) for slot in range(n_slots)]
    grid_spec = pltpu.PrefetchScalarGridSpec(
        num_scalar_prefetch=1,
        grid=(bsz, n_steps),
        in_specs=[row_spec, pl.BlockSpec((None, heads, d, page), lambda b, s, pt: (b, 0, 0, 0)), row_spec, row_spec,
                  pl.BlockSpec((heads, 1), lambda b, s, pt: (0, 0))] + page_specs + page_specs,
        out_specs=row_spec,
        scratch_shapes=[pltpu.VMEM((heads, page), F32), pltpu.VMEM((heads, d), F32), pltpu.VMEM((heads, d, page), F32)],
    )
    return pl.pallas_call(
        functools.partial(_sbs_kernel, n_slots=n_slots, heads=heads, page=page),
        grid_spec=grid_spec,
        out_shape=jax.ShapeDtypeStruct((bsz, heads, d), F32),
        compiler_params=_cparams("parallel", "arbitrary"),
        name="sb_sample",
    )(page_table, q, q_lanes, k_new, v_new, sb_bias.reshape(heads, 1), *([k_t] * n_slots), *([v_t] * n_slots))


def _conv_heads(c, gh):
    outs = []
    for hh in range(3 * gh):
        x = c[:, hh * LANES:(hh + 1) * LANES]
        if hh < 2 * gh:
            x = x * lax.rsqrt(jnp.sum(x * x, axis=-1, keepdims=True) + RMS_EPS)
            if hh < gh:
                x = x * (GDN_HEAD_DIM ** -0.5)
        outs.append(x)
    return outs


def _gprep_kernel(prev_ref, cur_ref, cw_ref, o_ref, ext_ref, *, tb, gh, width):
    t = pl.program_id(1)
    ext_ref[8:, :] = cur_ref[...]
    ext_ref[0:8, :] = jnp.where(t == 0, 0.0, prev_ref[...])
    off = 8 - (width - 1)
    acc = ext_ref[off:off + tb, :] * cw_ref[0:1, :]
    for i in range(1, width):
        acc = acc + ext_ref[off + i:off + i + tb, :] * cw_ref[i:i + 1, :]
    c = _silu(acc)
    for hh, x in enumerate(_conv_heads(c, gh)):
        o_ref[:, hh * LANES:(hh + 1) * LANES] = x


def _gdn_prep(z_main, conv_w, batch, seq, col_block, gh):
    width, cc = conv_w.shape
    tb = _pick(seq, (256, 128, 64, 32, 16, 8))
    nt = seq // tb
    return pl.pallas_call(
        functools.partial(_gprep_kernel, tb=tb, gh=gh, width=width),
        grid=(batch, nt),
        in_specs=[pl.BlockSpec((8, cc), lambda b, t: (jnp.maximum(b * (seq // 8) + t * (tb // 8) - 1, 0), col_block)),
                  pl.BlockSpec((tb, cc), lambda b, t: (b * nt + t, col_block)),
                  pl.BlockSpec((width, cc), lambda b, t: (0, 0))],
        out_specs=pl.BlockSpec((tb, cc), lambda b, t: (b * nt + t, 0)),
        out_shape=jax.ShapeDtypeStruct((batch * seq, cc), F32),
        scratch_shapes=[pltpu.VMEM((tb + 8, cc), F32)],
        compiler_params=_cparams("parallel", "arbitrary"),
        name="gdn_prep",
    )(z_main, z_main, conv_w)


def _unit_lower_inverses(mats):
    n = mats[0].shape[0]
    eye = jnp.where(lax.broadcasted_iota(jnp.int32, (n, n), 0) == lax.broadcasted_iota(jnp.int32, (n, n), 1), 1.0, 0.0)
    ps = [eye - a for a in mats]
    pws = [_dot3(a, a) for a in mats]
    size = 2
    while size < n:
        ps = [p + _dot3(p, pw) for p, pw in zip(ps, pws)]
        size *= 2
        if size < n:
            pws = [_dot3(pw, pw) for pw in pws]
    return ps


def _gdn_kernel(qkv_ref, ba_ref, zb_ref, alog_ref, dtb_ref, ng_ref, y_ref, sfin_ref, s_scr, *, gh):
    c = pl.program_id(1)
    n = GDN_CHUNK
    gw = gh * GDN_HEAD_DIM

    @pl.when(c == 0)
    def _():
        s_scr[...] = jnp.zeros_like(s_scr)

    row = lax.broadcasted_iota(jnp.int32, (n, n), 0)
    col = lax.broadcasted_iota(jnp.int32, (n, n), 1)
    tri = row >= col
    strict = row > col
    lmat = jnp.where(tri, 1.0, 0.0).astype(BF16)
    ba = ba_ref[...]
    heads = range(gh)
    q = [qkv_ref[:, h * LANES:(h + 1) * LANES] for h in heads]
    k = [qkv_ref[:, gw + h * LANES:gw + (h + 1) * LANES] for h in heads]
    v = [qkv_ref[:, 2 * gw + h * LANES:2 * gw + (h + 1) * LANES] for h in heads]
    beta_b = [jnp.broadcast_to(_sigmoid(ba[:, h:h + 1]), (n, GDN_HEAD_DIM)) for h in heads]
    g_b = [jnp.broadcast_to(-jnp.exp(alog_ref[0:1, h:h + 1])
                            * _softplus(ba[:, gh + h:gh + h + 1] + dtb_ref[0:1, h:h + 1]), (n, n)) for h in heads]
    gm = [_dot_exact_lhs(lmat, jnp.concatenate([jnp.where(strict, g, 0.0), g], axis=1)) for g in g_b]
    decay = [jnp.where(tri, jnp.exp(m[:, :n]), 0.0) for m in gm]
    gc = [m[:, n:] for m in gm]
    e_gc = [jnp.exp(x) for x in gc]
    e_rest = [jnp.exp(x[n - 1:n, :] - x) for x in gc]
    g_last = [jnp.exp(x[n - 1:n, :]) for x in gc]
    kbeta = [a * b for a, b in zip(k, beta_b)]
    a_low = [jnp.where(strict, _dot_nt(kb_, k_) * d, 0.0) for kb_, k_, d in zip(kbeta, k, decay)]
    qk = [jnp.where(tri, _dot_nt(q_, k_) * d, 0.0) for q_, k_, d in zip(q, k, decay)]
    t_inv = _unit_lower_inverses(a_low)
    sol = [_dot3(t, jnp.concatenate([v_ * b, kb_ * e], axis=1))
           for t, v_, b, kb_, e in zip(t_inv, v, beta_b, kbeta, e_gc)]
    s = [s_scr[h] for h in heads]
    v_new = [x[:, :GDN_HEAD_DIM] - _dot(x[:, GDN_HEAD_DIM:], s_) for x, s_ in zip(sol, s)]
    o = [_dot(q_ * e, s_) + _dot(qk_, vn) for q_, e, s_, qk_, vn in zip(q, e_gc, s, qk, v_new)]
    for h in heads:
        s_scr[h] = s[h] * g_last[h] + _dot_tn(k[h] * e_rest[h], v_new[h])
    for h in heads:
        on = o[h] * lax.rsqrt(jnp.mean(o[h] * o[h], axis=-1, keepdims=True) + RMS_EPS) * ng_ref[...]
        y_ref[:, h * LANES:(h + 1) * LANES] = (on * _silu(zb_ref[:, h * LANES:(h + 1) * LANES])).astype(y_ref.dtype)

    @pl.when(c == pl.num_programs(1) - 1)
    def _():
        sfin_ref[...] = s_scr[...]


def _gdn_prompt(qkv_n, ba, z_main, zb_col_block, a_log, dt_bias, norm_g, batch, seq, gh, out_dtype):
    assert seq % GDN_CHUNK == 0 and GDN_CHUNK == GDN_HEAD_DIM == LANES
    gw = gh * GDN_HEAD_DIM
    nc = seq // GDN_CHUNK
    return pl.pallas_call(
        functools.partial(_gdn_kernel, gh=gh),
        grid=(batch, nc),
        in_specs=[pl.BlockSpec((GDN_CHUNK, 3 * gw), lambda b, c: (b * nc + c, 0)),
                  pl.BlockSpec((GDN_CHUNK, 2 * gh), lambda b, c: (b * nc + c, 0)),
                  pl.BlockSpec((GDN_CHUNK, gw), lambda b, c: (b * nc + c, zb_col_block)),
                  pl.BlockSpec((1, gh), lambda b, c: (0, 0)),
                  pl.BlockSpec((1, gh), lambda b, c: (0, 0)),
                  pl.BlockSpec((1, GDN_HEAD_DIM), lambda b, c: (0, 0))],
        out_specs=[pl.BlockSpec((GDN_CHUNK, gw), lambda b, c: (b * nc + c, 0)),
                   pl.BlockSpec((None, gh, GDN_HEAD_DIM, GDN_HEAD_DIM), lambda b, c: (b, 0, 0, 0))],
        out_shape=[jax.ShapeDtypeStruct((batch * seq, gw), out_dtype),
                   jax.ShapeDtypeStruct((batch, gh, GDN_HEAD_DIM, GDN_HEAD_DIM), F32)],
        scratch_shapes=[pltpu.VMEM((gh, GDN_HEAD_DIM, GDN_HEAD_DIM), F32)],
        compiler_params=_cparams("parallel", "arbitrary"),
        name="gdn_prompt",
    )(qkv_n, ba, z_main, a_log.reshape(1, gh), dt_bias.reshape(1, gh), norm_g.reshape(1, GDN_HEAD_DIM))


def _gdns_kernel(sc_ref, zrow_ref, zb_ref, ba_ref, cw_ref, alog_ref, dtb_ref, ng_ref, s_ref,
                 y_ref, cnew_ref, snew_ref, *, gh, width):
    rows = [sc_ref[i:i + 1, :] for i in range(width - 1)] + [zrow_ref[...]]
    acc = rows[0] * cw_ref[0:1, :]
    for i in range(1, width):
        acc = acc + rows[i] * cw_ref[i:i + 1, :]
    for i in range(width - 1):
        cnew_ref[i:i + 1, :] = rows[i + 1]
    heads = _conv_heads(_silu(acc), gh)
    n = GDN_HEAD_DIM
    eye = lax.broadcasted_iota(jnp.int32, (n, n), 0) == lax.broadcasted_iota(jnp.int32, (n, n), 1)

    def column(r):
        return jnp.sum(jnp.where(eye, jnp.broadcast_to(r, (n, n)), 0.0), axis=1, keepdims=True)

    ba = ba_ref[...]
    for h in range(gh):
        q, k, v = heads[h], heads[gh + h], heads[2 * gh + h]
        beta = _sigmoid(ba[:, h:h + 1])
        g = -jnp.exp(alog_ref[0:1, h:h + 1]) * _softplus(ba[:, gh + h:gh + h + 1] + dtb_ref[0:1, h:h + 1])
        eg = jnp.exp(g)
        kcol = column(k)
        s = s_ref[h]
        ks = jnp.sum(kcol * s, axis=0, keepdims=True)
        v_new = beta * v - (beta * eg) * ks
        s_new = s * eg + kcol * v_new
        snew_ref[h] = s_new
        o = jnp.sum(column(q) * s_new, axis=0, keepdims=True)
        o = o * lax.rsqrt(jnp.mean(o * o, axis=-1, keepdims=True) + RMS_EPS) * ng_ref[...]
        y_ref[:, h * LANES:(h + 1) * LANES] = o * _silu(zb_ref[:, h * LANES:(h + 1) * LANES])


def _gdn_sample(state_conv, z3, ba3, conv_w, a_log, dt_bias, norm_g, state_ssm, qkvb_col_block, zb_col_block, gh):
    bsz, wm1, cc = state_conv.shape
    width = wm1 + 1
    gw = gh * GDN_HEAD_DIM
    small = lambda shape: pl.BlockSpec(shape, lambda b: (0,) * len(shape))
    return pl.pallas_call(
        functools.partial(_gdns_kernel, gh=gh, width=width),
        grid=(bsz,),
        in_specs=[pl.BlockSpec((None, wm1, cc), lambda b: (b, 0, 0)),
                  pl.BlockSpec((None, 1, cc), lambda b: (b, 0, qkvb_col_block)),
                  pl.BlockSpec((None, 1, gw), lambda b: (b, 0, zb_col_block)),
                  pl.BlockSpec((None, 1, 2 * gh), lambda b: (b, 0, 0)),
                  small((width, cc)), small((1, gh)), small((1, gh)), small((1, GDN_HEAD_DIM)),
                  pl.BlockSpec((None, gh, GDN_HEAD_DIM, GDN_HEAD_DIM), lambda b: (b, 0, 0, 0))],
        out_specs=[pl.BlockSpec((None, 1, gw), lambda b: (b, 0, 0)),
                   pl.BlockSpec((None, wm1, cc), lambda b: (b, 0, 0)),
                   pl.BlockSpec((None, gh, GDN_HEAD_DIM, GDN_HEAD_DIM), lambda b: (b, 0, 0, 0))],
        out_shape=[jax.ShapeDtypeStruct((bsz, 1, gw), F32),
                   jax.ShapeDtypeStruct((bsz, wm1, cc), F32),
                   jax.ShapeDtypeStruct((bsz, gh, GDN_HEAD_DIM, GDN_HEAD_DIM), F32)],
        compiler_params=_cparams("parallel"),
        name="gdn_sample",
    )(state_conv, z3, z3, ba3, conv_w, a_log.reshape(1, gh), dt_bias.reshape(1, gh),
      norm_g.reshape(1, GDN_HEAD_DIM), state_ssm)


def _mixa_kernel(x_ref, ya_ref, yb_ref, wga_ref, wgb_ref, wpa_ref, wpb_ref, o_ref, *, hp):
    x = x_ref[...]
    ga = _dot_nt(x, wga_ref[...], hp)
    gb = _dot_nt(x, wgb_ref[...], hp)
    pa = _dot(ya_ref[...], wpa_ref[...], hp)
    pb = _dot(yb_ref[...], wpb_ref[...], hp)
    o_ref[...] = (_sigmoid(ga) * pa + _sigmoid(gb) * pb).astype(o_ref.dtype)


def _mix_merge(x, ya, yb, w_gates, w_pa, w_pb, tm, tn, hp):
    m, dm = x.shape
    nj = dm // tn
    return pl.pallas_call(
        functools.partial(_mixa_kernel, hp=hp),
        grid=(m // tm, nj),
        in_specs=[pl.BlockSpec((tm, dm), lambda i, j: (i, 0)),
                  pl.BlockSpec((tm, ya.shape[1]), lambda i, j: (i, 0)),
                  pl.BlockSpec((tm, yb.shape[1]), lambda i, j: (i, 0)),
                  pl.BlockSpec((tn, dm), lambda i, j: (j, 0)),
                  pl.BlockSpec((tn, dm), lambda i, j: (nj + j, 0)),
                  pl.BlockSpec((w_pa.shape[0], tn), lambda i, j: (0, j)),
                  pl.BlockSpec((w_pb.shape[0], tn), lambda i, j: (0, j))],
        out_specs=pl.BlockSpec((tm, tn), lambda i, j: (i, j)),
        out_shape=jax.ShapeDtypeStruct((m, dm), F32),
        compiler_params=_cparams("parallel", "arbitrary"),
        name="mix_merge",
    )(x, ya, yb, w_gates, w_gates, w_pa, w_pb)


def _route(logits):
    lane = lax.broadcasted_iota(jnp.int32, logits.shape, 1).astype(F32)
    is_group = (lane >= N_EXPERTS) & (lane < N_EXPERTS + N_GROUPS)
    gl = jnp.where(is_group, logits, NEG_BIG)
    gmax = jnp.max(gl, axis=1, keepdims=True)
    gsel = jnp.min(jnp.where(gl == gmax, lane, 1e9), axis=1, keepdims=True) - N_EXPERTS
    p_group = 1.0 / jnp.sum(jnp.where(is_group, jnp.exp(gl - gmax), 0.0), axis=1, keepdims=True)
    in_group = (lane >= gsel * EXPERTS_PER_GROUP) & (lane < (gsel + 1.0) * EXPERTS_PER_GROUP)
    el = jnp.where(in_group, logits, NEG_BIG)
    m1 = jnp.max(el, axis=1, keepdims=True)
    i1 = jnp.min(jnp.where(el == m1, lane, 1e9), axis=1, keepdims=True)
    el2 = jnp.where(lane == i1, NEG_BIG, el)
    m2 = jnp.max(el2, axis=1, keepdims=True)
    i2 = jnp.min(jnp.where(el2 == m2, lane, 1e9), axis=1, keepdims=True)
    e2 = jnp.exp(m2 - m1)
    g1 = p_group / (1.0 + e2)
    g2 = p_group * e2 / (1.0 + e2)
    return jnp.where(lane == 0.0, i1, jnp.where(lane == 1.0, i2, jnp.where(lane == 2.0, g1,
                                                                            jnp.where(lane == 3.0, g2, 0.0))))


def _mixb_kernel(x_ref, m_ref, wout_ref, g_ref, b_ref, rw_ref, rb_ref, cnt0_ref,
                 hlo_ref, hhi_ref, route_ref, cnt_ref, cnt_scr, *, alpha, hp):
    @pl.when(pl.program_id(0) == 0)
    def _():
        cnt_scr[...] = cnt0_ref[...]

    t = alpha * x_ref[...] + _dot(m_ref[...], wout_ref[...], hp)
    h = _layer_norm(t, g_ref[...], b_ref[...])
    half = h.shape[1] // 2
    hlo_ref[...] = h[:, :half]
    hhi_ref[...] = h[:, half:]
    if hp:
        logits = _dot(h, rw_ref[...], True) + rb_ref[...]
    else:
        logits = _dot3(h, rw_ref[...]) + rb_ref[...]
    route = _route(logits)
    tm = h.shape[0]
    lane = lax.broadcasted_iota(jnp.int32, (tm, LANES), 1).astype(F32)
    first = jnp.where(lane == route[:, 0:1], 1.0, 0.0)
    second = jnp.where(lane == route[:, 1:2], 1.0, 0.0)
    earlier = jnp.where(lax.broadcasted_iota(jnp.int32, (tm, tm), 0) > lax.broadcasted_iota(jnp.int32, (tm, tm), 1),
                        1.0, 0.0).astype(BF16)
    n_first = jnp.sum(first, axis=0, keepdims=True)
    rank1 = jnp.dot(earlier, first.astype(BF16), preferred_element_type=F32) + cnt_scr[...]
    rank2 = jnp.dot(earlier, second.astype(BF16), preferred_element_type=F32) + cnt_scr[...] + n_first
    pos1 = jnp.sum(first * rank1, axis=1, keepdims=True)
    pos2 = jnp.sum(second * rank2, axis=1, keepdims=True)
    cnt_scr[...] = cnt_scr[...] + n_first + jnp.sum(second, axis=0, keepdims=True)
    cnt_ref[...] = cnt_scr[...]
    route_ref[...] = jnp.where(lane == 4.0, pos1, jnp.where(lane == 5.0, pos2, route))


def _mix_out(x, merged, w_out, ln_g, ln_b, router_w, router_b, counts0, alpha, tm, hp):
    m, dm = x.shape
    const = lambda shape: pl.BlockSpec(shape, lambda i: (0,) * len(shape))
    return pl.pallas_call(
        functools.partial(_mixb_kernel, alpha=alpha, hp=hp),
        grid=(m // tm,),
        in_specs=[pl.BlockSpec((tm, dm), lambda i: (i, 0)),
                  pl.BlockSpec((tm, dm), lambda i: (i, 0)),
                  const((dm, dm)), const((1, dm)), const((1, dm)), const((dm, LANES)), const((1, LANES)),
                  const((1, LANES))],
        out_specs=[pl.BlockSpec((tm, dm // 2), lambda i: (i, 0)),
                   pl.BlockSpec((tm, dm // 2), lambda i: (i, 0)),
                   pl.BlockSpec((tm, LANES), lambda i: (i, 0)),
                   const((1, LANES))],
        out_shape=[jax.ShapeDtypeStruct((m, dm // 2), F32), jax.ShapeDtypeStruct((m, dm // 2), F32),
                   jax.ShapeDtypeStruct((m, LANES), F32), jax.ShapeDtypeStruct((1, LANES), F32)],
        scratch_shapes=[pltpu.VMEM((1, LANES), F32)],
        compiler_params=_cparams("arbitrary"),
        name="mix_out_ln_route",
    )(x, merged, w_out, ln_g.reshape(1, dm), ln_b.reshape(1, dm), router_w, router_b, counts0)


MOE_TILE = 128
ROW_UNROLL = 8
MOE_COL_SPLIT = 4


def _moe_plan(route, counts, n_tok):
    experts = jnp.arange(N_EXPERTS, dtype=jnp.int32)
    padded = (counts[0, :N_EXPERTS].astype(jnp.int32) + MOE_TILE - 1) // MOE_TILE * MOE_TILE
    ends = jnp.cumsum(padded)
    starts = ends - padded
    eid = route[:, :2].astype(jnp.int32)
    pos = route[:, 4:6].astype(jnp.int32)
    seg = jnp.sum(jnp.where(eid[..., None] == experts, starts, 0), axis=-1)
    dest = (seg + pos).reshape(-1)
    n_rows = (2 * n_tok + MOE_TILE - 1) // MOE_TILE * MOE_TILE + N_EXPERTS * MOE_TILE
    tile_start = jnp.arange(n_rows // MOE_TILE, dtype=jnp.int32) * MOE_TILE
    tile_expert = jnp.minimum(jnp.sum((ends[None, :] <= tile_start[:, None]).astype(jnp.int32), axis=1),
                              N_EXPERTS - 1)
    n_valid = (ends[-1] // MOE_TILE).astype(jnp.int32).reshape(1)
    return dest, tile_expert, n_valid, n_rows


def _invert_kernel(dest_ref, src_ref, *, n_assign, n_rows):
    def clear(r, carry):
        src_ref[r] = 0
        return carry

    def fill(a, carry):
        src_ref[dest_ref[a]] = a // 2
        return carry

    lax.fori_loop(0, n_rows, clear, 0, unroll=ROW_UNROLL)
    lax.fori_loop(0, n_assign, fill, 0, unroll=ROW_UNROLL)


def _invert(dest, n_rows):
    grid_spec = pltpu.PrefetchScalarGridSpec(
        num_scalar_prefetch=1, grid=(1,), in_specs=[],
        out_specs=pl.BlockSpec(memory_space=pltpu.SMEM))
    return pl.pallas_call(
        functools.partial(_invert_kernel, n_assign=dest.shape[0], n_rows=n_rows),
        grid_spec=grid_spec,
        out_shape=jax.ShapeDtypeStruct((n_rows,), jnp.int32),
        compiler_params=_cparams("arbitrary"),
        name="moe_invert",
    )(dest)


def _gather_kernel(src_ref, x_ref, o_ref, rows_scr, *, rows):
    base = pl.program_id(0) * rows

    def body(r, carry):
        rows_scr[pl.ds(r, 1), :] = x_ref[pl.ds(src_ref[base + r], 1), :]
        return carry

    lax.fori_loop(0, rows, body, 0, unroll=ROW_UNROLL)
    o_ref[...] = rows_scr[...].astype(o_ref.dtype)


def _gather_rows(src, x):
    n_rows = src.shape[0]
    rows = _pick(n_rows, (256, 128))
    grid_spec = pltpu.PrefetchScalarGridSpec(
        num_scalar_prefetch=1,
        grid=(n_rows // rows,),
        in_specs=[pl.BlockSpec(memory_space=pltpu.VMEM)],
        out_specs=pl.BlockSpec((rows, x.shape[1]), lambda i, src: (i, 0)),
        scratch_shapes=[pltpu.VMEM((rows, x.shape[1]), F32)],
    )
    return pl.pallas_call(
        functools.partial(_gather_kernel, rows=rows),
        grid_spec=grid_spec,
        out_shape=jax.ShapeDtypeStruct((n_rows, x.shape[1]), BF16),
        compiler_params=_cparams("arbitrary"),
        name="moe_gather",
    )(src, x)


def _experts_kernel(te_ref, nv_ref, xlo_ref, xhi_ref, wg_ref, wu_ref, wd_ref, *refs):
    y_refs = refs[:MOE_COL_SPLIT]
    wg_scr, wu_scr, wd_scr = refs[MOE_COL_SPLIT:]
    i = pl.program_id(0)

    @pl.when((i == 0) | (te_ref[i] != te_ref[jnp.maximum(i - 1, 0)]))
    def _():
        wg_scr[...] = wg_ref[...].astype(BF16)
        wu_scr[...] = wu_ref[...].astype(BF16)
        wd_scr[...] = wd_ref[...].astype(BF16)

    @pl.when(i < nv_ref[0])
    def _():
        x = jnp.concatenate([xlo_ref[...], xhi_ref[...]], axis=1)
        a = jnp.dot(x, wg_scr[...], preferred_element_type=F32)
        u = jnp.dot(x, wu_scr[...], preferred_element_type=F32)
        y = jnp.dot((_silu(a) * u).astype(BF16), wd_scr[...], preferred_element_type=F32)
        q = y.shape[1] // MOE_COL_SPLIT
        for c, y_ref in enumerate(y_refs):
            y_ref[...] = y[:, c * q:(c + 1) * q]

    @pl.when(i >= nv_ref[0])
    def _():
        for y_ref in y_refs:
            y_ref[...] = jnp.zeros_like(y_ref)


def _experts(x_lo, x_hi, tile_expert, n_valid, w_gate, w_up, w_down):
    n_rows, half = x_lo.shape
    dm = 2 * half
    _, _, ff = w_gate.shape
    q = dm // MOE_COL_SPLIT
    grid_spec = pltpu.PrefetchScalarGridSpec(
        num_scalar_prefetch=2,
        grid=(n_rows // MOE_TILE,),
        in_specs=[pl.BlockSpec((MOE_TILE, half), lambda i, te, nv: (i, 0)),
                  pl.BlockSpec((MOE_TILE, half), lambda i, te, nv: (i, 0)),
                  pl.BlockSpec((None, dm, ff), lambda i, te, nv: (te[i], 0, 0)),
                  pl.BlockSpec((None, dm, ff), lambda i, te, nv: (te[i], 0, 0)),
                  pl.BlockSpec((None, ff, dm), lambda i, te, nv: (te[i], 0, 0))],
        out_specs=[pl.BlockSpec((MOE_TILE, q), lambda i, te, nv: (i, 0))] * MOE_COL_SPLIT,
        scratch_shapes=[pltpu.VMEM((dm, ff), BF16), pltpu.VMEM((dm, ff), BF16), pltpu.VMEM((ff, dm), BF16)],
    )
    return pl.pallas_call(
        _experts_kernel,
        grid_spec=grid_spec,
        out_shape=[jax.ShapeDtypeStruct((n_rows, q), F32)] * MOE_COL_SPLIT,
        compiler_params=_cparams("arbitrary"),
        name="moe_experts",
    )(tile_expert, n_valid, x_lo, x_hi, w_gate, w_up, w_down)


def _combine_kernel(dest_ref, y_ref, route_ref, o_ref, y1_scr, y2_scr, *, rows):
    base = pl.program_id(0) * rows

    def body(t, carry):
        y1_scr[pl.ds(t, 1), :] = y_ref[pl.ds(dest_ref[2 * (base + t)], 1), :]
        y2_scr[pl.ds(t, 1), :] = y_ref[pl.ds(dest_ref[2 * (base + t) + 1], 1), :]
        return carry

    lax.fori_loop(0, rows, body, 0, unroll=ROW_UNROLL)
    route = route_ref[...]
    o_ref[...] = route[:, 2:3] * y1_scr[...] + route[:, 3:4] * y2_scr[...]


def _combine(dest, y_slab, route):
    n_tok = route.shape[0]
    rows = _pick(n_tok, (256, 128, 64, 32))
    width = y_slab.shape[1]
    grid_spec = pltpu.PrefetchScalarGridSpec(
        num_scalar_prefetch=1,
        grid=(n_tok // rows,),
        in_specs=[pl.BlockSpec(memory_space=pltpu.VMEM),
                  pl.BlockSpec((rows, LANES), lambda i, dest: (i, 0))],
        out_specs=pl.BlockSpec((rows, width), lambda i, dest: (i, 0)),
        scratch_shapes=[pltpu.VMEM((rows, width), F32), pltpu.VMEM((rows, width), F32)],
    )
    return pl.pallas_call(
        functools.partial(_combine_kernel, rows=rows),
        grid_spec=grid_spec,
        out_shape=jax.ShapeDtypeStruct((n_tok, width), F32),
        compiler_params=_cparams("arbitrary"),
        name="moe_combine",
    )(dest, y_slab, route)


def _ln2_kernel(hlo_ref, hhi_ref, *refs, alpha):
    moe_refs = refs[:MOE_COL_SPLIT]
    g_ref, b_ref, o_ref = refs[MOE_COL_SPLIT:]
    h = jnp.concatenate([hlo_ref[...], hhi_ref[...]], axis=1)
    moe = jnp.concatenate([r[...] for r in moe_refs], axis=1)
    o_ref[...] = _layer_norm(alpha * h + moe, g_ref[...], b_ref[...])


def _ln2(h_lo, h_hi, moe_slabs, row_block0, ln_g, ln_b, alpha, tm):
    m, half = h_lo.shape
    dm = 2 * half
    q = dm // MOE_COL_SPLIT
    return pl.pallas_call(
        functools.partial(_ln2_kernel, alpha=alpha),
        grid=(m // tm,),
        in_specs=[pl.BlockSpec((tm, half), lambda i: (i, 0)), pl.BlockSpec((tm, half), lambda i: (i, 0))]
        + [pl.BlockSpec((tm, q), lambda i: (i + row_block0, 0))] * MOE_COL_SPLIT
        + [pl.BlockSpec((1, dm), lambda i: (0, 0)), pl.BlockSpec((1, dm), lambda i: (0, 0))],
        out_specs=pl.BlockSpec((tm, dm), lambda i: (i, 0)),
        out_shape=jax.ShapeDtypeStruct((m, dm), F32),
        compiler_params=_cparams("parallel"),
        name="moe_ln2",
    )(h_lo, h_hi, *moe_slabs, ln_g.reshape(1, dm), ln_b.reshape(1, dm))


def _moe(h_p, route_p, h_s, route_s, counts, w_gate, w_up, w_down, ln_g, ln_b, alpha):
    m_p, m_s = h_p[0].shape[0], h_s[0].shape[0]
    n_tok = m_p + m_s
    assert m_p % m_s == 0 and m_s % 8 == 0
    n_pad = (n_tok + 255) // 256 * 256
    route = jnp.concatenate([route_p, route_s], axis=0)
    dest, tile_expert, n_valid, n_rows = _moe_plan(route, counts, n_tok)
    src = _invert(dest, n_rows)
    x_lo = _gather_rows(src, jnp.concatenate([h_p[0], h_s[0]], axis=0))
    x_hi = _gather_rows(src, jnp.concatenate([h_p[1], h_s[1]], axis=0))
    y_slabs = _experts(x_lo, x_hi, tile_expert, n_valid, w_gate, w_up, w_down)
    route_pad = jnp.concatenate([route, jnp.zeros((n_pad - n_tok, LANES), F32)], axis=0)
    dest_pad = jnp.concatenate([dest, jnp.zeros((2 * (n_pad - n_tok),), jnp.int32)])
    moe_slabs = [_combine(dest_pad, y, route_pad) for y in y_slabs]
    out_p = _ln2(h_p[0], h_p[1], moe_slabs, 0, ln_g, ln_b, alpha, _pick(m_p, (256, 128, 64, 32, 16, 8)))
    out_s = _ln2(h_s[0], h_s[1], moe_slabs, m_p // m_s, ln_g, ln_b, alpha, m_s)
    return out_p, out_s


def _layer(hp_x, hs_x, cache_k, cache_v, page_table, state_conv, state_ssm, layer, w, alpha):
    batch, seq, dm = hp_x.shape
    dec_batch, dec_seq, _ = hs_x.shape
    assert dec_seq == 1
    sb_heads = w["sb_bias"].shape[0]
    gh = w["a_log"].shape[0]
    sbw = sb_heads * SB_HEAD_DIM
    gw = gh * GDN_HEAD_DIM
    cc = 3 * gw
    n_main = 3 * sbw + cc + gw
    assert sbw % LANES == 0 and 3 * sbw % cc == 0 and n_main % gw == 0 and sbw == gw
    qkvb_block = 3 * sbw // cc
    zb_block = (3 * sbw + cc) // gw
    w_t = jnp.transpose(w["w_in"])
    w_ba = w_t[n_main:n_main + 2 * gh]
    w_gates = w_t[n_main + 2 * gh:]
    router_w = jnp.concatenate(
        [jnp.transpose(w["router_expert"], (1, 0, 2)).reshape(dm, N_EXPERTS), w["router_group"],
         jnp.zeros((dm, LANES - N_EXPERTS - N_GROUPS), F32)], axis=1)
    router_b = jnp.concatenate(
        [w["router_expert_b"].reshape(N_EXPERTS), w["router_group_b"],
         jnp.zeros((LANES - N_EXPERTS - N_GROUPS,), F32)]).reshape(1, LANES)

    m = batch * seq
    xp = hp_x.reshape(m, dm)
    xp16 = xp.astype(BF16)
    tm = _pick(m, (1024, 512, 256, 128, 64, 32, 16, 8))
    z_main = _matmul(xp16, w_t, n_main, tm, _pick(n_main, (512, 256, 128)), False)
    ba = _matmul(xp16, w_ba, 2 * gh, tm, 2 * gh, False)
    y_a, k_t, v_t = _sb_prompt(z_main, w["sb_bias"], batch, seq, sbw, BF16)
    qkv_n = _gdn_prep(z_main, w["conv_w"], batch, seq, qkvb_block, gh)
    y_b, ssm_p = _gdn_prompt(qkv_n, ba, z_main, zb_block, w["a_log"], w["dt_bias"], w["gdn_norm_g"],
                             batch, seq, gh, BF16)
    merged = _mix_merge(xp16, y_a, y_b, w_gates.astype(BF16), w["w_proj_a"].astype(BF16), w["w_proj_b"].astype(BF16),
                        tm, _pick(dm, (512, 256, 128)), False)
    h1_lo, h1_hi, route, counts = _mix_out(xp, merged, w["w_out"].astype(BF16), w["ln1_g"], w["ln1_b"], router_w, router_b,
                                      jnp.zeros((1, LANES), F32), alpha, _pick(m, (256, 128, 64, 32, 16, 8)), False)
    k_p = jnp.transpose(k_t.reshape(batch, sb_heads, SB_HEAD_DIM, seq), (0, 3, 1, 2))
    v_p = jnp.transpose(v_t.reshape(batch, sb_heads, SB_HEAD_DIM, seq), (0, 3, 1, 2))
    conv_p = z_main.reshape(batch, seq, n_main)[:, seq - (w["conv_w"].shape[0] - 1):, 3 * sbw:3 * sbw + cc]

    xs = hs_x.reshape(dec_batch, dm)
    zs = _matmul(xs, w_t, n_main, dec_batch, _pick(n_main, (1024, 512, 256, 128)), True)
    ba_s = _matmul(xs, w_ba, 2 * gh, dec_batch, 2 * gh, True)
    q_s = zs[:, :sbw].reshape(dec_batch, sb_heads, SB_HEAD_DIM)
    k_s = zs[:, sbw:2 * sbw].reshape(dec_batch, sb_heads, SB_HEAD_DIM)
    v_s = zs[:, 2 * sbw:3 * sbw].reshape(dec_batch, sb_heads, SB_HEAD_DIM)
    ya_s = _sb_sample(q_s, k_s, v_s, w["sb_bias"], cache_k, cache_v, layer, page_table).reshape(dec_batch, sbw)
    yb_s, conv_s, ssm_s = _gdn_sample(state_conv, zs.reshape(dec_batch, 1, n_main), ba_s.reshape(dec_batch, 1, 2 * gh),
                                      w["conv_w"], w["a_log"], w["dt_bias"], w["gdn_norm_g"], state_ssm,
                                      qkvb_block, zb_block, gh)
    merged_s = _mix_merge(xs, ya_s, yb_s.reshape(dec_batch, gw), w_gates, w["w_proj_a"], w["w_proj_b"],
                          dec_batch, _pick(dm, (512, 256, 128)), True)
    h1s_lo, h1s_hi, route_s, counts = _mix_out(xs, merged_s, w["w_out"], w["ln1_g"], w["ln1_b"], router_w, router_b,
                                               counts, alpha, dec_batch, True)

    out_p, out_s = _moe((h1_lo, h1_hi), route, (h1s_lo, h1s_hi), route_s, counts, w["w_gate"], w["w_up"],
                        w["w_down"], w["ln2_g"], w["ln2_b"], alpha)

    return (out_p.reshape(batch, seq, dm), out_s.reshape(dec_batch, 1, dm), k_p, v_p,
            k_s.reshape(dec_batch, 1, sb_heads, SB_HEAD_DIM), v_s.reshape(dec_batch, 1, sb_heads, SB_HEAD_DIM),
            conv_p, conv_s, ssm_p, ssm_s)


def kernel(x_prompt, x_sample, cache_k, cache_v, page_table, state_conv, state_ssm, w_in, sb_bias, conv_w, a_log,
           dt_bias, gdn_norm_g, w_proj_a, w_proj_b, w_out, ln1_g, ln1_b, router_group, router_group_b,
           router_expert, router_expert_b, w_gate, w_up, w_down, ln2_g, ln2_b):
    depth = w_in.shape[0]
    alpha = (2.0 * depth) ** 0.25
    stacked = dict(w_in=w_in, sb_bias=sb_bias, conv_w=conv_w, a_log=a_log, dt_bias=dt_bias, gdn_norm_g=gdn_norm_g,
                   w_proj_a=w_proj_a, w_proj_b=w_proj_b, w_out=w_out, ln1_g=ln1_g, ln1_b=ln1_b,
                   router_group=router_group, router_group_b=router_group_b, router_expert=router_expert,
                   router_expert_b=router_expert_b, w_gate=w_gate, w_up=w_up, w_down=w_down, ln2_g=ln2_g, ln2_b=ln2_b)
    hp_x, hs_x = x_prompt, x_sample
    per_layer = []
    for layer in range(depth):
        w = {name: t[layer] for name, t in stacked.items()}
        outs = _layer(hp_x, hs_x, cache_k, cache_v, page_table, state_conv[layer], state_ssm[layer], layer, w, alpha)
        hp_x, hs_x = outs[0], outs[1]
        per_layer.append(outs[2:])
    return (hp_x, hs_x) + tuple(jnp.stack([o[i] for o in per_layer]) for i in range(8))
```

```python
import functools

import jax
import jax.numpy as jnp
from jax import lax
from jax.experimental import pallas as pl
from jax.experimental.pallas import tpu as pltpu

F32 = jnp.float32
BF16 = jnp.bfloat16

LANES = 128
SB_HEAD_DIM = 64
GDN_HEAD_DIM = 128
GDN_CHUNK = 128
LN_EPS = 1e-5
RMS_EPS = 1e-6
N_GROUPS = 4
EXPERTS_PER_GROUP = 8
N_EXPERTS = N_GROUPS * EXPERTS_PER_GROUP
NEG_BIG = -1e30
LOG2E = 1.4426950408889634
VMEM_LIMIT_BYTES = 58 * 1024 * 1024


def _cparams(*sem):
    return pltpu.CompilerParams(dimension_semantics=sem, vmem_limit_bytes=VMEM_LIMIT_BYTES)


def _pick(n, prefs):
    for p in prefs:
        if n % p == 0:
            return p
    return n


def _dot(a, b, hp=False):
    if hp:
        return jnp.dot(a.astype(F32), b.astype(F32), preferred_element_type=F32, precision=lax.Precision.HIGHEST)
    return jnp.dot(a.astype(BF16), b.astype(BF16), preferred_element_type=F32)


def _dot_nt(a, b, hp=False):
    dims = (((1,), (1,)), ((), ()))
    if hp:
        return lax.dot_general(a.astype(F32), b.astype(F32), dims, preferred_element_type=F32,
                               precision=lax.Precision.HIGHEST)
    return lax.dot_general(a.astype(BF16), b.astype(BF16), dims, preferred_element_type=F32)


def _dot_tn(a, b):
    return lax.dot_general(a.astype(BF16), b.astype(BF16), (((0,), (0,)), ((), ())), preferred_element_type=F32)


def _split2(a):
    hi = a.astype(BF16)
    lo = (a - hi.astype(F32)).astype(BF16)
    return hi, lo


def _split3(a):
    hi = a.astype(BF16)
    r = a - hi.astype(F32)
    mid = r.astype(BF16)
    lo = (r - mid.astype(F32)).astype(BF16)
    return hi, mid, lo


def _dot3(a, b):
    ah, al = _split2(a)
    bh, bl = _split2(b)
    d = functools.partial(jnp.dot, preferred_element_type=F32)
    return d(ah, bh) + d(ah, bl) + d(al, bh)


def _dot_exact_lhs(a_bf16, b):
    bh, bm, bl = _split3(b)
    d = functools.partial(jnp.dot, preferred_element_type=F32)
    return d(a_bf16, bh) + d(a_bf16, bm) + d(a_bf16, bl)


def _dot_exact_rhs(a, b_bf16):
    ah, am, al = _split3(a)
    d = functools.partial(jnp.dot, preferred_element_type=F32)
    return d(ah, b_bf16) + d(am, b_bf16) + d(al, b_bf16)


def _softplus(z):
    return jnp.maximum(z, 0.0) + jnp.log(1.0 + jnp.exp(-jnp.abs(z)))


def _sigmoid(z):
    return 1.0 / (1.0 + jnp.exp(-z))


def _silu(z):
    return z * _sigmoid(z)


def _layer_norm(t, g, b):
    mu = jnp.mean(t, axis=-1, keepdims=True)
    c = t - mu
    var = jnp.mean(c * c, axis=-1, keepdims=True)
    return c * lax.rsqrt(var + LN_EPS) * g + b


def _mm_kernel(x_ref, wt_ref, o_ref, *, hp):
    o_ref[...] = _dot_nt(x_ref[...], wt_ref[...], hp)


def _matmul(x, w_t, n_cols, tm, tn, hp):
    m, k = x.shape
    return pl.pallas_call(
        functools.partial(_mm_kernel, hp=hp),
        grid=(m // tm, n_cols // tn),
        in_specs=[pl.BlockSpec((tm, k), lambda i, j: (i, 0)),
                  pl.BlockSpec((tn, k), lambda i, j: (j, 0))],
        out_specs=pl.BlockSpec((tm, tn), lambda i, j: (i, j)),
        out_shape=jax.ShapeDtypeStruct((m, n_cols), F32),
        compiler_params=_cparams("parallel", "arbitrary"),
        name="proj_matmul",
    )(x, w_t)


def _sbp_kernel(bias_ref, q_ref, k_ref, v_ref, o_ref, kt_ref, vt_ref, *, tq, kb):
    hpair = pl.program_id(1)
    qi = pl.program_id(2)
    q = q_ref[...] * (SB_HEAD_DIM ** -0.5 * LOG2E)
    lane = lax.broadcasted_iota(jnp.int32, (1, LANES), 1)
    row = lax.broadcasted_iota(jnp.int32, (tq, kb), 0)
    col = lax.broadcasted_iota(jnp.int32, (tq, kb), 1)
    r2 = lax.broadcasted_iota(jnp.int32, (kb, kb), 0)
    c2 = lax.broadcasted_iota(jnp.int32, (kb, kb), 1)
    after = jnp.where(r2 > c2, 1.0, 0.0).astype(BF16)
    n_diag = tq // kb
    n_heads = LANES // SB_HEAD_DIM
    head_lanes = [(lane >= h * SB_HEAD_DIM) & (lane < (h + 1) * SB_HEAD_DIM) for h in range(n_heads)]
    q_stack = jnp.concatenate([jnp.where(m, q, 0.0) for m in head_lanes], axis=0).astype(BF16)
    bias = jnp.concatenate([jnp.full((tq, 1), bias_ref[hpair * n_heads + h] * LOG2E, F32) for h in range(n_heads)],
                           axis=0)
    row = jnp.concatenate([row] * n_heads, axis=0)
    col = jnp.concatenate([col] * n_heads, axis=0)

    def block(kstart, carry, masked):
        log_surv, acc = carry
        kblk = k_ref[pl.ds(kstart, kb), :]
        vblk = v_ref[pl.ds(kstart, kb), :]
        z = _dot_nt(q_stack, kblk) + bias
        sp = jnp.maximum(z, 0.0) + jnp.log2(1.0 + jnp.exp2(-jnp.abs(z)))
        if masked:
            visible = (kstart + col) < (qi * tq + row)
            sp = jnp.where(visible, sp, 0.0)
        later = jnp.dot(sp.astype(BF16), after, preferred_element_type=F32) + log_surv
        w = jnp.exp2(z - sp - later)
        if masked:
            w = jnp.where(visible, w, 0.0)
        w = w.astype(BF16)
        for h in range(n_heads):
            v_h = jnp.where(head_lanes[h], vblk, 0.0).astype(BF16)
            acc = acc + jnp.dot(w[h * tq:(h + 1) * tq], v_h, preferred_element_type=F32)
        log_surv = log_surv + jnp.sum(sp, axis=1, keepdims=True)
        return log_surv, acc

    carry = (jnp.zeros((n_heads * tq, 1), F32), jnp.zeros((tq, LANES), F32))
    for d in range(n_diag):
        kstart = pl.multiple_of(qi * tq + (n_diag - 1 - d) * kb, kb)
        carry = block(kstart, carry, True)
    n_before = qi * n_diag

    def body(i, carry):
        for d in range(n_diag):
            kstart = pl.multiple_of((n_before - 1 - i * n_diag - d) * kb, kb)
            carry = block(kstart, carry, False)
        return carry

    carry = lax.fori_loop(0, qi, body, carry)
    o_ref[...] = carry[1].astype(o_ref.dtype)

    @pl.when(qi == 0)
    def _():
        for start in range(0, k_ref.shape[0], LANES):
            kt_ref[:, start:start + LANES] = k_ref[start:start + LANES, :].T
            vt_ref[:, start:start + LANES] = v_ref[start:start + LANES, :].T


def _sb_prompt(z_main, sb_bias, batch, seq, sbw, out_dtype):
    tq = _pick(seq, (512, 256, 128))
    kb = _pick(tq, (256, 128))
    nq = seq // tq
    n_pairs = sbw // LANES
    grid_spec = pltpu.PrefetchScalarGridSpec(
        num_scalar_prefetch=1,
        grid=(batch, n_pairs, nq),
        in_specs=[pl.BlockSpec((tq, LANES), lambda b, h, i, bias: (b * nq + i, h)),
                  pl.BlockSpec((seq, LANES), lambda b, h, i, bias: (b, n_pairs + h)),
                  pl.BlockSpec((seq, LANES), lambda b, h, i, bias: (b, 2 * n_pairs + h))],
        out_specs=[pl.BlockSpec((tq, LANES), lambda b, h, i, bias: (b * nq + i, h)),
                   pl.BlockSpec((None, LANES, seq), lambda b, h, i, bias: (b, h, 0)),
                   pl.BlockSpec((None, LANES, seq), lambda b, h, i, bias: (b, h, 0))],
    )
    return pl.pallas_call(
        functools.partial(_sbp_kernel, tq=tq, kb=kb),
        grid_spec=grid_spec,
        out_shape=[jax.ShapeDtypeStruct((batch * seq, sbw), out_dtype),
                   jax.ShapeDtypeStruct((batch, sbw, seq), F32),
                   jax.ShapeDtypeStruct((batch, sbw, seq), F32)],
        compiler_params=_cparams("parallel", "parallel", "arbitrary"),
        name="sb_prompt",
    )(sb_bias, z_main, z_main, z_main)


def _sbs_kernel(pt_ref, q_ref, qb_ref, knew_ref, vnew_ref, bias_ref, *refs, n_slots, heads, page):
    k_refs = refs[:n_slots]
    v_refs = refs[n_slots:2 * n_slots]
    o_ref = refs[2 * n_slots]
    ls_ref, new_ref, acc_ref = refs[2 * n_slots + 1:]
    step = pl.program_id(1)
    d = SB_HEAD_DIM
    sub = 8
    bias = bias_ref[...]

    @pl.when(step == 0)
    def _():
        z_new = jnp.sum(q_ref[...] * knew_ref[...], axis=1, keepdims=True) * (d ** -0.5) + bias
        visible = jnp.zeros(z_new.shape, jnp.bool_)
        ls_ref[...] = jnp.broadcast_to(jnp.where(visible, _softplus(z_new), 0.0), ls_ref.shape)
        new_ref[...] = jnp.where(visible, jnp.exp(z_new - _softplus(z_new)), 0.0) * vnew_ref[...]
        acc_ref[...] = jnp.zeros_like(acc_ref)

    r2 = lax.broadcasted_iota(jnp.int32, (page, 2 * page), 0)
    c2 = lax.broadcasted_iota(jnp.int32, (page, 2 * page), 1)
    after_ones = jnp.where((r2 > c2) | (c2 >= page), 1.0, 0.0).astype(BF16)
    z_rows = [[None] * heads for _ in range(n_slots)]
    for h in range(heads):
        qh = qb_ref[h]
        for s in range(n_slots):
            prod = k_refs[s][h] * qh
            z_rows[s][h] = jnp.sum(prod, axis=0, keepdims=True)
    z = jnp.concatenate([r for rows in z_rows for r in rows], axis=0) + jnp.tile(bias, (n_slots, 1))
    sp = _softplus(z)
    cum_tot = _dot_exact_rhs(sp, after_ones)
    base = z - sp - cum_tot[:, :page]
    log_surv = ls_ref[...]
    w = []
    for s in range(n_slots):
        w.append(jnp.exp(base[s * heads:(s + 1) * heads] - log_surv))
        log_surv = log_surv + cum_tot[s * heads:(s + 1) * heads, page:]
    ls_ref[...] = log_surv
    for h in range(heads):
        a = acc_ref[h].reshape(d // sub, sub, page)
        for s in range(n_slots):
            w_row = jnp.broadcast_to(w[s][h:h + 1, :], (sub, page))
            a = a + v_refs[s][h].reshape(d // sub, sub, page) * w_row[None]
        acc_ref[h] = a.reshape(d, page)

    @pl.when(step == pl.num_programs(1) - 1)
    def _():
        o_ref[...] = new_ref[...] + jnp.sum(acc_ref[...], axis=-1)


def _sb_sample(q, k_new, v_new, sb_bias, cache_k, cache_v, layer, page_table):
    bsz, heads, d = q.shape
    page = cache_k.shape[2]
    assert d == SB_HEAD_DIM and page == LANES
    n_pages = page_table.shape[1]
    n_slots = _pick(n_pages, (8, 4, 2, 1))
    n_steps = n_pages // n_slots
    k_t = jnp.transpose(cache_k, (0, 1, 3, 4, 2))
    v_t = jnp.transpose(cache_v, (0, 1, 3, 4, 2))
    q_lanes = jnp.broadcast_to((q * (d ** -0.5))[..., None], (bsz, heads, d, page))

    def page_map(slot):
        def index_map(b, s, pt):
            return (layer, pt[b, n_pages - 1 - (s * n_slots + slot)], 0, 0, 0)
        return index_map

    row_spec = pl.BlockSpec((None, heads, d), lambda b, s, pt: (b, 0, 0))
    page_specs = [pl.BlockSpec((None, None, heads, d, page), page_map(slot)) for slot in range(n_slots)]
    grid_spec = pltpu.PrefetchScalarGridSpec(
        num_scalar_prefetch=1,
        grid=(bsz, n_steps),
        in_specs=[row_spec, pl.BlockSpec((None, heads, d, page), lambda b, s, pt: (b, 0, 0, 0)), row_spec, row_spec,
                  pl.BlockSpec((heads, 1), lambda b, s, pt: (0, 0))] + page_specs + page_specs,
        out_specs=row_spec,
        scratch_shapes=[pltpu.VMEM((heads, page), F32), pltpu.VMEM((heads, d), F32), pltpu.VMEM((heads, d, page), F32)],
    )
    return pl.pallas_call(
        functools.partial(_sbs_kernel, n_slots=n_slots, heads=heads, page=page),
        grid_spec=grid_spec,
        out_shape=jax.ShapeDtypeStruct((bsz, heads, d), F32),
        compiler_params=_cparams("parallel", "arbitrary"),
        name="sb_sample",
    )(page_table, q, q_lanes, k_new, v_new, sb_bias.reshape(heads, 1), *([k_t] * n_slots), *([v_t] * n_slots))


def _conv_heads(c, gh):
    outs = []
    for hh in range(3 * gh):
        x = c[:, hh * LANES:(hh + 1) * LANES]
        if hh < 2 * gh:
            x = x * lax.rsqrt(jnp.sum(x * x, axis=-1, keepdims=True) + RMS_EPS)
            if hh < gh:
                x = x * (GDN_HEAD_DIM ** -0.5)
        outs.append(x)
    return outs


def _gprep_kernel(prev_ref, cur_ref, cw_ref, o_ref, ext_ref, *, tb, gh, width):
    t = pl.program_id(1)
    ext_ref[8:, :] = cur_ref[...]
    ext_ref[0:8, :] = jnp.where(t == 0, 0.0, prev_ref[...])
    off = 8 - (width - 1)
    acc = ext_ref[off:off + tb, :] * cw_ref[0:1, :]
    for i in range(1, width):
        acc = acc + ext_ref[off + i:off + i + tb, :] * cw_ref[i:i + 1, :]
    c = _silu(acc)
    for hh, x in enumerate(_conv_heads(c, gh)):
        o_ref[:, hh * LANES:(hh + 1) * LANES] = x


def _gdn_prep(z_main, conv_w, batch, seq, col_block, gh):
    width, cc = conv_w.shape
    tb = _pick(seq, (256, 128, 64, 32, 16, 8))
    nt = seq // tb
    return pl.pallas_call(
        functools.partial(_gprep_kernel, tb=tb, gh=gh, width=width),
        grid=(batch, nt),
        in_specs=[pl.BlockSpec((8, cc), lambda b, t: (jnp.maximum(b * (seq // 8) + t * (tb // 8) - 1, 0), col_block)),
                  pl.BlockSpec((tb, cc), lambda b, t: (b * nt + t, col_block)),
                  pl.BlockSpec((width, cc), lambda b, t: (0, 0))],
        out_specs=pl.BlockSpec((tb, cc), lambda b, t: (b * nt + t, 0)),
        out_shape=jax.ShapeDtypeStruct((batch * seq, cc), F32),
        scratch_shapes=[pltpu.VMEM((tb + 8, cc), F32)],
        compiler_params=_cparams("parallel", "arbitrary"),
        name="gdn_prep",
    )(z_main, z_main, conv_w)


def _unit_lower_inverses(mats):
    n = mats[0].shape[0]
    eye = jnp.where(lax.broadcasted_iota(jnp.int32, (n, n), 0) == lax.broadcasted_iota(jnp.int32, (n, n), 1), 1.0, 0.0)
    ps = [eye - a for a in mats]
    pws = [_dot3(a, a) for a in mats]
    size = 2
    while size < n:
        ps = [p + _dot3(p, pw) for p, pw in zip(ps, pws)]
        size *= 2
        if size < n:
            pws = [_dot3(pw, pw) for pw in pws]
    return ps


def _gdn_kernel(qkv_ref, ba_ref, zb_ref, alog_ref, dtb_ref, ng_ref, y_ref, sfin_ref, s_scr, *, gh):
    c = pl.program_id(1)
    n = GDN_CHUNK
    gw = gh * GDN_HEAD_DIM

    @pl.when(c == 0)
    def _():
        s_scr[...] = jnp.zeros_like(s_scr)

    row = lax.broadcasted_iota(jnp.int32, (n, n), 0)
    col = lax.broadcasted_iota(jnp.int32, (n, n), 1)
    tri = row >= col
    strict = row > col
    lmat = jnp.where(tri, 1.0, 0.0).astype(BF16)
    ba = ba_ref[...]
    heads = range(gh)
    q = [qkv_ref[:, h * LANES:(h + 1) * LANES] for h in heads]
    k = [qkv_ref[:, gw + h * LANES:gw + (h + 1) * LANES] for h in heads]
    v = [qkv_ref[:, 2 * gw + h * LANES:2 * gw + (h + 1) * LANES] for h in heads]
    beta_b = [jnp.broadcast_to(_sigmoid(ba[:, h:h + 1]), (n, GDN_HEAD_DIM)) for h in heads]
    g_b = [jnp.broadcast_to(-jnp.exp(alog_ref[0:1, h:h + 1])
                            * _softplus(ba[:, gh + h:gh + h + 1] + dtb_ref[0:1, h:h + 1]), (n, n)) for h in heads]
    gm = [_dot_exact_lhs(lmat, jnp.concatenate([jnp.where(strict, g, 0.0), g], axis=1)) for g in g_b]
    decay = [jnp.where(tri, jnp.exp(m[:, :n]), 0.0) for m in gm]
    gc = [m[:, n:] for m in gm]
    e_gc = [jnp.exp(x) for x in gc]
    e_rest = [jnp.exp(x[n - 1:n, :] - x) for x in gc]
    g_last = [jnp.exp(x[n - 1:n, :]) for x in gc]
    kbeta = [a * b for a, b in zip(k, beta_b)]
    a_low = [jnp.where(strict, _dot_nt(kb_, k_) * d, 0.0) for kb_, k_, d in zip(kbeta, k, decay)]
    qk = [jnp.where(tri, _dot_nt(q_, k_) * d, 0.0) for q_, k_, d in zip(q, k, decay)]
    t_inv = _unit_lower_inverses(a_low)
    sol = [_dot3(t, jnp.concatenate([v_ * b, kb_ * e], axis=1))
           for t, v_, b, kb_, e in zip(t_inv, v, beta_b, kbeta, e_gc)]
    s = [s_scr[h] for h in heads]
    v_new = [x[:, :GDN_HEAD_DIM] - _dot(x[:, GDN_HEAD_DIM:], s_) for x, s_ in zip(sol, s)]
    o = [_dot(q_ * e, s_) + _dot(qk_, vn) for q_, e, s_, qk_, vn in zip(q, e_gc, s, qk, v_new)]
    for h in heads:
        s_scr[h] = s[h] * g_last[h] + _dot_tn(k[h] * e_rest[h], v_new[h])
    for h in heads:
        on = o[h] * lax.rsqrt(jnp.mean(o[h] * o[h], axis=-1, keepdims=True) + RMS_EPS) * ng_ref[...]
        y_ref[:, h * LANES:(h + 1) * LANES] = (on * _silu(zb_ref[:, h * LANES:(h + 1) * LANES])).astype(y_ref.dtype)

    @pl.when(c == pl.num_programs(1) - 1)
    def _():
        sfin_ref[...] = s_scr[...]


def _gdn_prompt(qkv_n, ba, z_main, zb_col_block, a_log, dt_bias, norm_g, batch, seq, gh, out_dtype):
    assert seq % GDN_CHUNK == 0 and GDN_CHUNK == GDN_HEAD_DIM == LANES
    gw = gh * GDN_HEAD_DIM
    nc = seq // GDN_CHUNK
    return pl.pallas_call(
        functools.partial(_gdn_kernel, gh=gh),
        grid=(batch, nc),
        in_specs=[pl.BlockSpec((GDN_CHUNK, 3 * gw), lambda b, c: (b * nc + c, 0)),
                  pl.BlockSpec((GDN_CHUNK, 2 * gh), lambda b, c: (b * nc + c, 0)),
                  pl.BlockSpec((GDN_CHUNK, gw), lambda b, c: (b * nc + c, zb_col_block)),
                  pl.BlockSpec((1, gh), lambda b, c: (0, 0)),
                  pl.BlockSpec((1, gh), lambda b, c: (0, 0)),
                  pl.BlockSpec((1, GDN_HEAD_DIM), lambda b, c: (0, 0))],
        out_specs=[pl.BlockSpec((GDN_CHUNK, gw), lambda b, c: (b * nc + c, 0)),
                   pl.BlockSpec((None, gh, GDN_HEAD_DIM, GDN_HEAD_DIM), lambda b, c: (b, 0, 0, 0))],
        out_shape=[jax.ShapeDtypeStruct((batch * seq, gw), out_dtype),
                   jax.ShapeDtypeStruct((batch, gh, GDN_HEAD_DIM, GDN_HEAD_DIM), F32)],
        scratch_shapes=[pltpu.VMEM((gh, GDN_HEAD_DIM, GDN_HEAD_DIM), F32)],
        compiler_params=_cparams("parallel", "arbitrary"),
        name="gdn_prompt",
    )(qkv_n, ba, z_main, a_log.reshape(1, gh), dt_bias.reshape(1, gh), norm_g.reshape(1, GDN_HEAD_DIM))


def _gdns_kernel(sc_ref, zrow_ref, zb_ref, ba_ref, cw_ref, alog_ref, dtb_ref, ng_ref, s_ref,
                 y_ref, cnew_ref, snew_ref, *, gh, width):
    rows = [sc_ref[i:i + 1, :] for i in range(width - 1)] + [zrow_ref[...]]
    acc = rows[0] * cw_ref[0:1, :]
    for i in range(1, width):
        acc = acc + rows[i] * cw_ref[i:i + 1, :]
    for i in range(width - 1):
        cnew_ref[i:i + 1, :] = rows[i + 1]
    heads = _conv_heads(_silu(acc), gh)
    n = GDN_HEAD_DIM
    eye = lax.broadcasted_iota(jnp.int32, (n, n), 0) == lax.broadcasted_iota(jnp.int32, (n, n), 1)

    def column(r):
        return jnp.sum(jnp.where(eye, jnp.broadcast_to(r, (n, n)), 0.0), axis=1, keepdims=True)

    ba = ba_ref[...]
    for h in range(gh):
        q, k, v = heads[h], heads[gh + h], heads[2 * gh + h]
        beta = _sigmoid(ba[:, h:h + 1])
        g = -jnp.exp(alog_ref[0:1, h:h + 1]) * _softplus(ba[:, gh + h:gh + h + 1] + dtb_ref[0:1, h:h + 1])
        eg = jnp.exp(g)
        kcol = column(k)
        s = s_ref[h]
        ks = jnp.sum(kcol * s, axis=0, keepdims=True)
        v_new = beta * v - (beta * eg) * ks
        s_new = s * eg + kcol * v_new
        snew_ref[h] = s_new
        o = jnp.sum(column(q) * s_new, axis=0, keepdims=True)
        o = o * lax.rsqrt(jnp.mean(o * o, axis=-1, keepdims=True) + RMS_EPS) * ng_ref[...]
        y_ref[:, h * LANES:(h + 1) * LANES] = o * _silu(zb_ref[:, h * LANES:(h + 1) * LANES])


def _gdn_sample(state_conv, z3, ba3, conv_w, a_log, dt_bias, norm_g, state_ssm, qkvb_col_block, zb_col_block, gh):
    bsz, wm1, cc = state_conv.shape
    width = wm1 + 1
    gw = gh * GDN_HEAD_DIM
    small = lambda shape: pl.BlockSpec(shape, lambda b: (0,) * len(shape))
    return pl.pallas_call(
        functools.partial(_gdns_kernel, gh=gh, width=width),
        grid=(bsz,),
        in_specs=[pl.BlockSpec((None, wm1, cc), lambda b: (b, 0, 0)),
                  pl.BlockSpec((None, 1, cc), lambda b: (b, 0, qkvb_col_block)),
                  pl.BlockSpec((None, 1, gw), lambda b: (b, 0, zb_col_block)),
                  pl.BlockSpec((None, 1, 2 * gh), lambda b: (b, 0, 0)),
                  small((width, cc)), small((1, gh)), small((1, gh)), small((1, GDN_HEAD_DIM)),
                  pl.BlockSpec((None, gh, GDN_HEAD_DIM, GDN_HEAD_DIM), lambda b: (b, 0, 0, 0))],
        out_specs=[pl.BlockSpec((None, 1, gw), lambda b: (b, 0, 0)),
                   pl.BlockSpec((None, wm1, cc), lambda b: (b, 0, 0)),
                   pl.BlockSpec((None, gh, GDN_HEAD_DIM, GDN_HEAD_DIM), lambda b: (b, 0, 0, 0))],
        out_shape=[jax.ShapeDtypeStruct((bsz, 1, gw), F32),
                   jax.ShapeDtypeStruct((bsz, wm1, cc), F32),
                   jax.ShapeDtypeStruct((bsz, gh, GDN_HEAD_DIM, GDN_HEAD_DIM), F32)],
        compiler_params=_cparams("parallel"),
        name="gdn_sample",
    )(state_conv, z3, z3, ba3, conv_w, a_log.reshape(1, gh), dt_bias.reshape(1, gh),
      norm_g.reshape(1, GDN_HEAD_DIM), state_ssm)


def _mixa_kernel(x_ref, ya_ref, yb_ref, wga_ref, wgb_ref, wpa_ref, wpb_ref, o_ref, *, hp):
    x = x_ref[...]
    ga = _dot_nt(x, wga_ref[...], hp)
    gb = _dot_nt(x, wgb_ref[...], hp)
    pa = _dot(ya_ref[...], wpa_ref[...], hp)
    pb = _dot(yb_ref[...], wpb_ref[...], hp)
    o_ref[...] = (_sigmoid(ga) * pa + _sigmoid(gb) * pb).astype(o_ref.dtype)


def _mix_merge(x, ya, yb, w_gates, w_pa, w_pb, tm, tn, hp):
    m, dm = x.shape
    nj = dm // tn
    return pl.pallas_call(
        functools.partial(_mixa_kernel, hp=hp),
        grid=(m // tm, nj),
        in_specs=[pl.BlockSpec((tm, dm), lambda i, j: (i, 0)),
                  pl.BlockSpec((tm, ya.shape[1]), lambda i, j: (i, 0)),
                  pl.BlockSpec((tm, yb.shape[1]), lambda i, j: (i, 0)),
                  pl.BlockSpec((tn, dm), lambda i, j: (j, 0)),
                  pl.BlockSpec((tn, dm), lambda i, j: (nj + j, 0)),
                  pl.BlockSpec((w_pa.shape[0], tn), lambda i, j: (0, j)),
                  pl.BlockSpec((w_pb.shape[0], tn), lambda i, j: (0, j))],
        out_specs=pl.BlockSpec((tm, tn), lambda i, j: (i, j)),
        out_shape=jax.ShapeDtypeStruct((m, dm), F32),
        compiler_params=_cparams("parallel", "arbitrary"),
        name="mix_merge",
    )(x, ya, yb, w_gates, w_gates, w_pa, w_pb)


def _route(logits):
    lane = lax.broadcasted_iota(jnp.int32, logits.shape, 1).astype(F32)
    is_group = (lane >= N_EXPERTS) & (lane < N_EXPERTS + N_GROUPS)
    gl = jnp.where(is_group, logits, NEG_BIG)
    gmax = jnp.max(gl, axis=1, keepdims=True)
    gsel = jnp.min(jnp.where(gl == gmax, lane, 1e9), axis=1, keepdims=True) - N_EXPERTS
    p_group = 1.0 / jnp.sum(jnp.where(is_group, jnp.exp(gl - gmax), 0.0), axis=1, keepdims=True)
    in_group = (lane >= gsel * EXPERTS_PER_GROUP) & (lane < (gsel + 1.0) * EXPERTS_PER_GROUP)
    el = jnp.where(in_group, logits, NEG_BIG)
    m1 = jnp.max(el, axis=1, keepdims=True)
    i1 = jnp.min(jnp.where(el == m1, lane, 1e9), axis=1, keepdims=True)
    el2 = jnp.where(lane == i1, NEG_BIG, el)
    m2 = jnp.max(el2, axis=1, keepdims=True)
    i2 = jnp.min(jnp.where(el2 == m2, lane, 1e9), axis=1, keepdims=True)
    e2 = jnp.exp(m2 - m1)
    g1 = p_group / (1.0 + e2)
    g2 = p_group * e2 / (1.0 + e2)
    return jnp.where(lane == 0.0, i1, jnp.where(lane == 1.0, i2, jnp.where(lane == 2.0, g1,
                                                                            jnp.where(lane == 3.0, g2, 0.0))))


def _mixb_kernel(x_ref, m_ref, wout_ref, g_ref, b_ref, rw_ref, rb_ref, cnt0_ref,
                 hlo_ref, hhi_ref, route_ref, cnt_ref, cnt_scr, *, alpha, hp):
    @pl.when(pl.program_id(0) == 0)
    def _():
        cnt_scr[...] = cnt0_ref[...]

    t = alpha * x_ref[...] + _dot(m_ref[...], wout_ref[...], hp)
    h = _layer_norm(t, g_ref[...], b_ref[...])
    half = h.shape[1] // 2
    hlo_ref[...] = h[:, :half]
    hhi_ref[...] = h[:, half:]
    if hp:
        logits = _dot(h, rw_ref[...], True) + rb_ref[...]
    else:
        logits = _dot3(h, rw_ref[...]) + rb_ref[...]
    route = _route(logits)
    tm = h.shape[0]
    lane = lax.broadcasted_iota(jnp.int32, (tm, LANES), 1).astype(F32)
    first = jnp.where(lane == route[:, 0:1], 1.0, 0.0)
    second = jnp.where(lane == route[:, 1:2], 1.0, 0.0)
    earlier = jnp.where(lax.broadcasted_iota(jnp.int32, (tm, tm), 0) > lax.broadcasted_iota(jnp.int32, (tm, tm), 1),
                        1.0, 0.0).astype(BF16)
    n_first = jnp.sum(first, axis=0, keepdims=True)
    rank1 = jnp.dot(earlier, first.astype(BF16), preferred_element_type=F32) + cnt_scr[...]
    rank2 = jnp.dot(earlier, second.astype(BF16), preferred_element_type=F32) + cnt_scr[...] + n_first
    pos1 = jnp.sum(first * rank1, axis=1, keepdims=True)
    pos2 = jnp.sum(second * rank2, axis=1, keepdims=True)
    cnt_scr[...] = cnt_scr[...] + n_first + jnp.sum(second, axis=0, keepdims=True)
    cnt_ref[...] = cnt_scr[...]
    route_ref[...] = jnp.where(lane == 4.0, pos1, jnp.where(lane == 5.0, pos2, route))


def _mix_out(x, merged, w_out, ln_g, ln_b, router_w, router_b, counts0, alpha, tm, hp):
    m, dm = x.shape
    const = lambda shape: pl.BlockSpec(shape, lambda i: (0,) * len(shape))
    return pl.pallas_call(
        functools.partial(_mixb_kernel, alpha=alpha, hp=hp),
        grid=(m // tm,),
        in_specs=[pl.BlockSpec((tm, dm), lambda i: (i, 0)),
                  pl.BlockSpec((tm, dm), lambda i: (i, 0)),
                  const((dm, dm)), const((1, dm)), const((1, dm)), const((dm, LANES)), const((1, LANES)),
                  const((1, LANES))],
        out_specs=[pl.BlockSpec((tm, dm // 2), lambda i: (i, 0)),
                   pl.BlockSpec((tm, dm // 2), lambda i: (i, 0)),
                   pl.BlockSpec((tm, LANES), lambda i: (i, 0)),
                   const((1, LANES))],
        out_shape=[jax.ShapeDtypeStruct((m, dm // 2), F32), jax.ShapeDtypeStruct((m, dm // 2), F32),
                   jax.ShapeDtypeStruct((m, LANES), F32), jax.ShapeDtypeStruct((1, LANES), F32)],
        scratch_shapes=[pltpu.VMEM((1, LANES), F32)],
        compiler_params=_cparams("arbitrary"),
        name="mix_out_ln_route",
    )(x, merged, w_out, ln_g.reshape(1, dm), ln_b.reshape(1, dm), router_w, router_b, counts0)


MOE_TILE = 128
ROW_UNROLL = 8
MOE_COL_SPLIT = 4


def _moe_plan(route, counts, n_tok):
    experts = jnp.arange(N_EXPERTS, dtype=jnp.int32)
    padded = (counts[0, :N_EXPERTS].astype(jnp.int32) + MOE_TILE - 1) // MOE_TILE * MOE_TILE
    ends = jnp.cumsum(padded)
    starts = ends - padded
    eid = route[:, :2].astype(jnp.int32)
    pos = route[:, 4:6].astype(jnp.int32)
    seg = jnp.sum(jnp.where(eid[..., None] == experts, starts, 0), axis=-1)
    dest = (seg + pos).reshape(-1)
    n_rows = (2 * n_tok + MOE_TILE - 1) // MOE_TILE * MOE_TILE + N_EXPERTS * MOE_TILE
    tile_start = jnp.arange(n_rows // MOE_TILE, dtype=jnp.int32) * MOE_TILE
    tile_expert = jnp.minimum(jnp.sum((ends[None, :] <= tile_start[:, None]).astype(jnp.int32), axis=1),
                              N_EXPERTS - 1)
    n_valid = (ends[-1] // MOE_TILE).astype(jnp.int32).reshape(1)
    pad_lo = jnp.concatenate([starts + counts[0, :N_EXPERTS].astype(jnp.int32), ends[-1:]])
    pad_hi = jnp.concatenate([ends, jnp.full((1,), n_rows, jnp.int32)])
    return dest, tile_expert, n_valid, n_rows, pad_lo, pad_hi


def _invert_kernel(dest_ref, pad_lo_ref, pad_hi_ref, src_ref, *, n_tok):
    def clear_range(e, carry):
        def clear(r, inner):
            src_ref[r] = 0
            return inner

        lax.fori_loop(pad_lo_ref[e], pad_hi_ref[e], clear, 0)
        return carry

    def fill(t, carry):
        src_ref[dest_ref[2 * t]] = t
        src_ref[dest_ref[2 * t + 1]] = t
        return carry

    lax.fori_loop(0, pad_lo_ref.shape[0], clear_range, 0)
    lax.fori_loop(0, n_tok, fill, 0, unroll=ROW_UNROLL)


def _invert(dest, pad_lo, pad_hi, n_rows):
    grid_spec = pltpu.PrefetchScalarGridSpec(
        num_scalar_prefetch=3, grid=(1,), in_specs=[],
        out_specs=pl.BlockSpec(memory_space=pltpu.SMEM))
    return pl.pallas_call(
        functools.partial(_invert_kernel, n_tok=dest.shape[0] // 2),
        grid_spec=grid_spec,
        out_shape=jax.ShapeDtypeStruct((n_rows,), jnp.int32),
        compiler_params=_cparams("arbitrary"),
        name="moe_invert",
    )(dest, pad_lo, pad_hi)


def _gather_kernel(src_ref, x_ref, o_ref, rows_scr, *, rows):
    base = pl.program_id(0) * rows

    def body(r, carry):
        rows_scr[pl.ds(r, 1), :] = x_ref[pl.ds(src_ref[base + r], 1), :]
        return carry

    lax.fori_loop(0, rows, body, 0, unroll=ROW_UNROLL)
    o_ref[...] = rows_scr[...].astype(o_ref.dtype)


def _gather_rows(src, x):
    n_rows = src.shape[0]
    rows = _pick(n_rows, (256, 128))
    grid_spec = pltpu.PrefetchScalarGridSpec(
        num_scalar_prefetch=1,
        grid=(n_rows // rows,),
        in_specs=[pl.BlockSpec(memory_space=pltpu.VMEM)],
        out_specs=pl.BlockSpec((rows, x.shape[1]), lambda i, src: (i, 0)),
        scratch_shapes=[pltpu.VMEM((rows, x.shape[1]), F32)],
    )
    return pl.pallas_call(
        functools.partial(_gather_kernel, rows=rows),
        grid_spec=grid_spec,
        out_shape=jax.ShapeDtypeStruct((n_rows, x.shape[1]), BF16),
        compiler_params=_cparams("arbitrary"),
        name="moe_gather",
    )(src, x)


def _experts_kernel(te_ref, nv_ref, xlo_ref, xhi_ref, wg_ref, wu_ref, wd_ref, *refs):
    y_refs = refs[:MOE_COL_SPLIT]
    wg_scr, wu_scr, wd_scr = refs[MOE_COL_SPLIT:]
    i = pl.program_id(0)

    @pl.when((i == 0) | (te_ref[i] != te_ref[jnp.maximum(i - 1, 0)]))
    def _():
        wg_scr[...] = wg_ref[...].astype(BF16)
        wu_scr[...] = wu_ref[...].astype(BF16)
        wd_scr[...] = wd_ref[...].astype(BF16)

    @pl.when(i < nv_ref[0])
    def _():
        x = jnp.concatenate([xlo_ref[...], xhi_ref[...]], axis=1)
        a = jnp.dot(x, wg_scr[...], preferred_element_type=F32)
        u = jnp.dot(x, wu_scr[...], preferred_element_type=F32)
        y = jnp.dot((_silu(a) * u).astype(BF16), wd_scr[...], preferred_element_type=F32)
        q = y.shape[1] // MOE_COL_SPLIT
        for c, y_ref in enumerate(y_refs):
            y_ref[...] = y[:, c * q:(c + 1) * q]

    @pl.when(i >= nv_ref[0])
    def _():
        for y_ref in y_refs:
            y_ref[...] = jnp.zeros_like(y_ref)


def _experts(x_lo, x_hi, tile_expert, n_valid, w_gate, w_up, w_down):
    n_rows, half = x_lo.shape
    dm = 2 * half
    _, _, ff = w_gate.shape
    q = dm // MOE_COL_SPLIT
    grid_spec = pltpu.PrefetchScalarGridSpec(
        num_scalar_prefetch=2,
        grid=(n_rows // MOE_TILE,),
        in_specs=[pl.BlockSpec((MOE_TILE, half), lambda i, te, nv: (i, 0)),
                  pl.BlockSpec((MOE_TILE, half), lambda i, te, nv: (i, 0)),
                  pl.BlockSpec((None, dm, ff), lambda i, te, nv: (te[i], 0, 0)),
                  pl.BlockSpec((None, dm, ff), lambda i, te, nv: (te[i], 0, 0)),
                  pl.BlockSpec((None, ff, dm), lambda i, te, nv: (te[i], 0, 0))],
        out_specs=[pl.BlockSpec((MOE_TILE, q), lambda i, te, nv: (i, 0))] * MOE_COL_SPLIT,
        scratch_shapes=[pltpu.VMEM((dm, ff), BF16), pltpu.VMEM((dm, ff), BF16), pltpu.VMEM((ff, dm), BF16)],
    )
    return pl.pallas_call(
        _experts_kernel,
        grid_spec=grid_spec,
        out_shape=[jax.ShapeDtypeStruct((n_rows, q), F32)] * MOE_COL_SPLIT,
        compiler_params=_cparams("arbitrary"),
        name="moe_experts",
    )(tile_expert, n_valid, x_lo, x_hi, w_gate, w_up, w_down)


def _combine_kernel(dest_ref, gate_ref, y_ref, o_ref, *, rows):
    base = pl.program_id(0) * rows

    def body(t, carry):
        a = 2 * (base + t)
        first = y_ref[pl.ds(dest_ref[a], 1), :]
        second = y_ref[pl.ds(dest_ref[a + 1], 1), :]
        o_ref[pl.ds(t, 1), :] = gate_ref[a] * first + gate_ref[a + 1] * second
        return carry

    lax.fori_loop(0, rows, body, 0, unroll=ROW_UNROLL)


def _combine(dest, gates, y_slab):
    n_tok = dest.shape[0] // 2
    rows = _pick(n_tok, (256, 128, 64, 32))
    width = y_slab.shape[1]
    grid_spec = pltpu.PrefetchScalarGridSpec(
        num_scalar_prefetch=2,
        grid=(n_tok // rows,),
        in_specs=[pl.BlockSpec(memory_space=pltpu.VMEM)],
        out_specs=pl.BlockSpec((rows, width), lambda i, dest, gates: (i, 0)),
    )
    return pl.pallas_call(
        functools.partial(_combine_kernel, rows=rows),
        grid_spec=grid_spec,
        out_shape=jax.ShapeDtypeStruct((n_tok, width), F32),
        compiler_params=_cparams("arbitrary"),
        name="moe_combine",
    )(dest, gates, y_slab)


def _ln2_kernel(hlo_ref, hhi_ref, *refs, alpha):
    moe_refs = refs[:MOE_COL_SPLIT]
    g_ref, b_ref, o_ref = refs[MOE_COL_SPLIT:]
    h = jnp.concatenate([hlo_ref[...], hhi_ref[...]], axis=1)
    moe = jnp.concatenate([r[...] for r in moe_refs], axis=1)
    o_ref[...] = _layer_norm(alpha * h + moe, g_ref[...], b_ref[...])


def _ln2(h_lo, h_hi, moe_slabs, row_block0, ln_g, ln_b, alpha, tm):
    m, half = h_lo.shape
    dm = 2 * half
    q = dm // MOE_COL_SPLIT
    return pl.pallas_call(
        functools.partial(_ln2_kernel, alpha=alpha),
        grid=(m // tm,),
        in_specs=[pl.BlockSpec((tm, half), lambda i: (i, 0)), pl.BlockSpec((tm, half), lambda i: (i, 0))]
        + [pl.BlockSpec((tm, q), lambda i: (i + row_block0, 0))] * MOE_COL_SPLIT
        + [pl.BlockSpec((1, dm), lambda i: (0, 0)), pl.BlockSpec((1, dm), lambda i: (0, 0))],
        out_specs=pl.BlockSpec((tm, dm), lambda i: (i, 0)),
        out_shape=jax.ShapeDtypeStruct((m, dm), F32),
        compiler_params=_cparams("parallel"),
        name="moe_ln2",
    )(h_lo, h_hi, *moe_slabs, ln_g.reshape(1, dm), ln_b.reshape(1, dm))


def _moe(h_p, route_p, h_s, route_s, counts, w_gate, w_up, w_down, ln_g, ln_b, alpha):
    m_p, m_s = h_p[0].shape[0], h_s[0].shape[0]
    n_tok = m_p + m_s
    assert m_p % m_s == 0 and m_s % 8 == 0
    n_pad = (n_tok + 255) // 256 * 256
    route = jnp.concatenate([route_p, route_s], axis=0)
    dest, tile_expert, n_valid, n_rows, pad_lo, pad_hi = _moe_plan(route, counts, n_tok)
    src = _invert(dest, pad_lo, pad_hi, n_rows)
    x_lo = _gather_rows(src, jnp.concatenate([h_p[0], h_s[0]], axis=0))
    x_hi = _gather_rows(src, jnp.concatenate([h_p[1], h_s[1]], axis=0))
    y_slabs = _experts(x_lo, x_hi, tile_expert, n_valid, w_gate, w_up, w_down)
    gates_pad = jnp.concatenate([route[:, 2:4].reshape(-1), jnp.zeros((2 * (n_pad - n_tok),), F32)])
    dest_pad = jnp.concatenate([dest, jnp.zeros((2 * (n_pad - n_tok),), jnp.int32)])
    moe_slabs = [_combine(dest_pad, gates_pad, y) for y in y_slabs]
    out_p = _ln2(h_p[0], h_p[1], moe_slabs, 0, ln_g, ln_b, alpha, _pick(m_p, (256, 128, 64, 32, 16, 8)))
    out_s = _ln2(h_s[0], h_s[1], moe_slabs, m_p // m_s, ln_g, ln_b, alpha, m_s)
    return out_p, out_s


def _layer(hp_x, hs_x, cache_k, cache_v, page_table, state_conv, state_ssm, layer, w, alpha):
    batch, seq, dm = hp_x.shape
    dec_batch, dec_seq, _ = hs_x.shape
    assert dec_seq == 1
    sb_heads = w["sb_bias"].shape[0]
    gh = w["a_log"].shape[0]
    sbw = sb_heads * SB_HEAD_DIM
    gw = gh * GDN_HEAD_DIM
    cc = 3 * gw
    n_main = 3 * sbw + cc + gw
    assert sbw % LANES == 0 and 3 * sbw % cc == 0 and n_main % gw == 0 and sbw == gw
    qkvb_block = 3 * sbw // cc
    zb_block = (3 * sbw + cc) // gw
    w_t = jnp.transpose(w["w_in"])
    w_ba = w_t[n_main:n_main + 2 * gh]
    w_gates = w_t[n_main + 2 * gh:]
    router_w = jnp.concatenate(
        [jnp.transpose(w["router_expert"], (1, 0, 2)).reshape(dm, N_EXPERTS), w["router_group"],
         jnp.zeros((dm, LANES - N_EXPERTS - N_GROUPS), F32)], axis=1)
    router_b = jnp.concatenate(
        [w["router_expert_b"].reshape(N_EXPERTS), w["router_group_b"],
         jnp.zeros((LANES - N_EXPERTS - N_GROUPS,), F32)]).reshape(1, LANES)

    m = batch * seq
    xp = hp_x.reshape(m, dm)
    xp16 = xp.astype(BF16)
    tm = _pick(m, (1024, 512, 256, 128, 64, 32, 16, 8))
    z_main = _matmul(xp16, w_t, n_main, tm, _pick(n_main, (512, 256, 128)), False)
    ba = _matmul(xp16, w_ba, 2 * gh, tm, 2 * gh, False)
    y_a, k_t, v_t = _sb_prompt(z_main, w["sb_bias"], batch, seq, sbw, BF16)
    qkv_n = _gdn_prep(z_main, w["conv_w"], batch, seq, qkvb_block, gh)
    y_b, ssm_p = _gdn_prompt(qkv_n, ba, z_main, zb_block, w["a_log"], w["dt_bias"], w["gdn_norm_g"],
                             batch, seq, gh, BF16)
    merged = _mix_merge(xp16, y_a, y_b, w_gates.astype(BF16), w["w_proj_a"].astype(BF16), w["w_proj_b"].astype(BF16),
                        tm, _pick(dm, (512, 256, 128)), False)
    h1_lo, h1_hi, route, counts = _mix_out(xp, merged, w["w_out"].astype(BF16), w["ln1_g"], w["ln1_b"], router_w, router_b,
                                      jnp.zeros((1, LANES), F32), alpha, _pick(m, (256, 128, 64, 32, 16, 8)), False)
    k_p = jnp.transpose(k_t.reshape(batch, sb_heads, SB_HEAD_DIM, seq), (0, 3, 1, 2))
    v_p = jnp.transpose(v_t.reshape(batch, sb_heads, SB_HEAD_DIM, seq), (0, 3, 1, 2))
    conv_p = z_main.reshape(batch, seq, n_main)[:, seq - (w["conv_w"].shape[0] - 1):, 3 * sbw:3 * sbw + cc]

    xs = hs_x.reshape(dec_batch, dm)
    zs = _matmul(xs, w_t, n_main, dec_batch, _pick(n_main, (1024, 512, 256, 128)), True)
    ba_s = _matmul(xs, w_ba, 2 * gh, dec_batch, 2 * gh, True)
    q_s = zs[:, :sbw].reshape(dec_batch, sb_heads, SB_HEAD_DIM)
    k_s = zs[:, sbw:2 * sbw].reshape(dec_batch, sb_heads, SB_HEAD_DIM)
    v_s = zs[:, 2 * sbw:3 * sbw].reshape(dec_batch, sb_heads, SB_HEAD_DIM)
    ya_s = _sb_sample(q_s, k_s, v_s, w["sb_bias"], cache_k, cache_v, layer, page_table).reshape(dec_batch, sbw)
    yb_s, conv_s, ssm_s = _gdn_sample(state_conv, zs.reshape(dec_batch, 1, n_main), ba_s.reshape(dec_batch, 1, 2 * gh),
                                      w["conv_w"], w["a_log"], w["dt_bias"], w["gdn_norm_g"], state_ssm,
                                      qkvb_block, zb_block, gh)
    merged_s = _mix_merge(xs, ya_s, yb_s.reshape(dec_batch, gw), w_gates, w["w_proj_a"], w["w_proj_b"],
                          dec_batch, _pick(dm, (512, 256, 128)), True)
    h1s_lo, h1s_hi, route_s, counts = _mix_out(xs, merged_s, w["w_out"], w["ln1_g"], w["ln1_b"], router_w, router_b,
                                               counts, alpha, dec_batch, True)

    out_p, out_s = _moe((h1_lo, h1_hi), route, (h1s_lo, h1s_hi), route_s, counts, w["w_gate"], w["w_up"],
                        w["w_down"], w["ln2_g"], w["ln2_b"], alpha)

    return (out_p.reshape(batch, seq, dm), out_s.reshape(dec_batch, 1, dm), k_p, v_p,
            k_s.reshape(dec_batch, 1, sb_heads, SB_HEAD_DIM), v_s.reshape(dec_batch, 1, sb_heads, SB_HEAD_DIM),
            conv_p, conv_s, ssm_p, ssm_s)


def kernel(x_prompt, x_sample, cache_k, cache_v, page_table, state_conv, state_ssm, w_in, sb_bias, conv_w, a_log,
           dt_bias, gdn_norm_g, w_proj_a, w_proj_b, w_out, ln1_g, ln1_b, router_group, router_group_b,
           router_expert, router_expert_b, w_gate, w_up, w_down, ln2_g, ln2_b):
    depth = w_in.shape[0]
    alpha = (2.0 * depth) ** 0.25
    stacked = dict(w_in=w_in, sb_bias=sb_bias, conv_w=conv_w, a_log=a_log, dt_bias=dt_bias, gdn_norm_g=gdn_norm_g,
                   w_proj_a=w_proj_a, w_proj_b=w_proj_b, w_out=w_out, ln1_g=ln1_g, ln1_b=ln1_b,
                   router_group=router_group, router_group_b=router_group_b, router_expert=router_expert,
                   router_expert_b=router_expert_b, w_gate=w_gate, w_up=w_up, w_down=w_down, ln2_g=ln2_g, ln2_b=ln2_b)
    hp_x, hs_x = x_prompt, x_sample
    per_layer = []
    for layer in range(depth):
        w = {name: t[layer] for name, t in stacked.items()}
        outs = _layer(hp_x, hs_x, cache_k, cache_v, page_table, state_conv[layer], state_ssm[layer], layer, w, alpha)
        hp_x, hs_x = outs[0], outs[1]
        per_layer.append(outs[2:])
    return (hp_x, hs_x) + tuple(jnp.stack([o[i] for o in per_layer]) for i in range(8))
```

```python
import functools

import jax
import jax.numpy as jnp
from jax import lax
from jax.experimental import pallas as pl
from jax.experimental.pallas import tpu as pltpu

F32 = jnp.float32
BF16 = jnp.bfloat16

LANES = 128
SB_HEAD_DIM = 64
GDN_HEAD_DIM = 128
GDN_CHUNK = 128
LN_EPS = 1e-5
RMS_EPS = 1e-6
N_GROUPS = 4
EXPERTS_PER_GROUP = 8
N_EXPERTS = N_GROUPS * EXPERTS_PER_GROUP
NEG_BIG = -1e30
LOG2E = 1.4426950408889634
VMEM_LIMIT_BYTES = 58 * 1024 * 1024


def _cparams(*sem):
    return pltpu.CompilerParams(dimension_semantics=sem, vmem_limit_bytes=VMEM_LIMIT_BYTES)


def _pick(n, prefs):
    for p in prefs:
        if n % p == 0:
            return p
    return n


def _dot(a, b, hp=False):
    if hp:
        return jnp.dot(a.astype(F32), b.astype(F32), preferred_element_type=F32, precision=lax.Precision.HIGHEST)
    return jnp.dot(a.astype(BF16), b.astype(BF16), preferred_element_type=F32)


def _dot_nt(a, b, hp=False):
    dims = (((1,), (1,)), ((), ()))
    if hp:
        return lax.dot_general(a.astype(F32), b.astype(F32), dims, preferred_element_type=F32,
                               precision=lax.Precision.HIGHEST)
    return lax.dot_general(a.astype(BF16), b.astype(BF16), dims, preferred_element_type=F32)


def _dot_tn(a, b):
    return lax.dot_general(a.astype(BF16), b.astype(BF16), (((0,), (0,)), ((), ())), preferred_element_type=F32)


def _split2(a):
    hi = a.astype(BF16)
    lo = (a - hi.astype(F32)).astype(BF16)
    return hi, lo


def _split3(a):
    hi = a.astype(BF16)
    r = a - hi.astype(F32)
    mid = r.astype(BF16)
    lo = (r - mid.astype(F32)).astype(BF16)
    return hi, mid, lo


def _dot3(a, b):
    ah, al = _split2(a)
    bh, bl = _split2(b)
    d = functools.partial(jnp.dot, preferred_element_type=F32)
    return d(ah, bh) + d(ah, bl) + d(al, bh)


def _dot_exact_lhs(a_bf16, b):
    bh, bm, bl = _split3(b)
    d = functools.partial(jnp.dot, preferred_element_type=F32)
    return d(a_bf16, bh) + d(a_bf16, bm) + d(a_bf16, bl)


def _dot_exact_rhs(a, b_bf16):
    ah, am, al = _split3(a)
    d = functools.partial(jnp.dot, preferred_element_type=F32)
    return d(ah, b_bf16) + d(am, b_bf16) + d(al, b_bf16)


def _softplus(z):
    return jnp.maximum(z, 0.0) + jnp.log(1.0 + jnp.exp(-jnp.abs(z)))


def _sigmoid(z):
    return 1.0 / (1.0 + jnp.exp(-z))


def _silu(z):
    return z * _sigmoid(z)


def _layer_norm(t, g, b):
    mu = jnp.mean(t, axis=-1, keepdims=True)
    c = t - mu
    var = jnp.mean(c * c, axis=-1, keepdims=True)
    return c * lax.rsqrt(var + LN_EPS) * g + b


def _mm_kernel(x_ref, wt_ref, o_ref, *, hp):
    o_ref[...] = _dot_nt(x_ref[...], wt_ref[...], hp)


def _matmul(x, w_t, n_cols, tm, tn, hp):
    m, k = x.shape
    return pl.pallas_call(
        functools.partial(_mm_kernel, hp=hp),
        grid=(m // tm, n_cols // tn),
        in_specs=[pl.BlockSpec((tm, k), lambda i, j: (i, 0)),
                  pl.BlockSpec((tn, k), lambda i, j: (j, 0))],
        out_specs=pl.BlockSpec((tm, tn), lambda i, j: (i, j)),
        out_shape=jax.ShapeDtypeStruct((m, n_cols), F32),
        compiler_params=_cparams("parallel", "arbitrary"),
        name="proj_matmul",
    )(x, w_t)


def _sbp_kernel(bias_ref, q_ref, k_ref, v_ref, o_ref, kt_ref, vt_ref, *, tq, kb):
    hpair = pl.program_id(1)
    qi = pl.program_id(2)
    q = q_ref[...] * (SB_HEAD_DIM ** -0.5 * LOG2E)
    lane = lax.broadcasted_iota(jnp.int32, (1, LANES), 1)
    row = lax.broadcasted_iota(jnp.int32, (tq, kb), 0)
    col = lax.broadcasted_iota(jnp.int32, (tq, kb), 1)
    r2 = lax.broadcasted_iota(jnp.int32, (kb, kb), 0)
    c2 = lax.broadcasted_iota(jnp.int32, (kb, kb), 1)
    after = jnp.where(r2 > c2, 1.0, 0.0).astype(BF16)
    n_diag = tq // kb
    n_heads = LANES // SB_HEAD_DIM
    head_lanes = [(lane >= h * SB_HEAD_DIM) & (lane < (h + 1) * SB_HEAD_DIM) for h in range(n_heads)]
    q_stack = jnp.concatenate([jnp.where(m, q, 0.0) for m in head_lanes], axis=0).astype(BF16)
    bias = jnp.concatenate([jnp.full((tq, 1), bias_ref[hpair * n_heads + h] * LOG2E, F32) for h in range(n_heads)],
                           axis=0)
    row = jnp.concatenate([row] * n_heads, axis=0)
    col = jnp.concatenate([col] * n_heads, axis=0)

    def block(kstart, carry, masked):
        log_surv, acc = carry
        kblk = k_ref[pl.ds(kstart, kb), :]
        vblk = v_ref[pl.ds(kstart, kb), :]
        z = _dot_nt(q_stack, kblk) + bias
        sp = jnp.maximum(z, 0.0) + jnp.log2(1.0 + jnp.exp2(-jnp.abs(z)))
        if masked:
            visible = (kstart + col) < (qi * tq + row)
            sp = jnp.where(visible, sp, 0.0)
        later = jnp.dot(sp.astype(BF16), after, preferred_element_type=F32) + log_surv
        w = jnp.exp2(z - sp - later)
        if masked:
            w = jnp.where(visible, w, 0.0)
        w = w.astype(BF16)
        for h in range(n_heads):
            v_h = jnp.where(head_lanes[h], vblk, 0.0).astype(BF16)
            acc = acc + jnp.dot(w[h * tq:(h + 1) * tq], v_h, preferred_element_type=F32)
        log_surv = log_surv + jnp.sum(sp, axis=1, keepdims=True)
        return log_surv, acc

    carry = (jnp.zeros((n_heads * tq, 1), F32), jnp.zeros((tq, LANES), F32))
    for d in range(n_diag):
        kstart = pl.multiple_of(qi * tq + (n_diag - 1 - d) * kb, kb)
        carry = block(kstart, carry, True)
    n_before = qi * n_diag

    def body(i, carry):
        for d in range(n_diag):
            kstart = pl.multiple_of((n_before - 1 - i * n_diag - d) * kb, kb)
            carry = block(kstart, carry, False)
        return carry

    carry = lax.fori_loop(0, qi, body, carry)
    o_ref[...] = carry[1].astype(o_ref.dtype)

    @pl.when(qi == 0)
    def _():
        for start in range(0, k_ref.shape[0], LANES):
            kt_ref[:, start:start + LANES] = k_ref[start:start + LANES, :].T
            vt_ref[:, start:start + LANES] = v_ref[start:start + LANES, :].T


def _sb_prompt(z_main, sb_bias, batch, seq, sbw, out_dtype):
    tq = _pick(seq, (512, 256, 128))
    kb = _pick(tq, (256, 128))
    nq = seq // tq
    n_pairs = sbw // LANES
    grid_spec = pltpu.PrefetchScalarGridSpec(
        num_scalar_prefetch=1,
        grid=(batch, n_pairs, nq),
        in_specs=[pl.BlockSpec((tq, LANES), lambda b, h, i, bias: (b * nq + i, h)),
                  pl.BlockSpec((seq, LANES), lambda b, h, i, bias: (b, n_pairs + h)),
                  pl.BlockSpec((seq, LANES), lambda b, h, i, bias: (b, 2 * n_pairs + h))],
        out_specs=[pl.BlockSpec((tq, LANES), lambda b, h, i, bias: (b * nq + i, h)),
                   pl.BlockSpec((None, LANES, seq), lambda b, h, i, bias: (b, h, 0)),
                   pl.BlockSpec((None, LANES, seq), lambda b, h, i, bias: (b, h, 0))],
    )
    return pl.pallas_call(
        functools.partial(_sbp_kernel, tq=tq, kb=kb),
        grid_spec=grid_spec,
        out_shape=[jax.ShapeDtypeStruct((batch * seq, sbw), out_dtype),
                   jax.ShapeDtypeStruct((batch, sbw, seq), F32),
                   jax.ShapeDtypeStruct((batch, sbw, seq), F32)],
        compiler_params=_cparams("parallel", "parallel", "arbitrary"),
        name="sb_prompt",
    )(sb_bias, z_main, z_main, z_main)


def _sbs_kernel(pt_ref, q_ref, qb_ref, knew_ref, vnew_ref, bias_ref, *refs, n_slots, heads, page):
    k_refs = refs[:n_slots]
    v_refs = refs[n_slots:2 * n_slots]
    o_ref = refs[2 * n_slots]
    ls_ref, new_ref, acc_ref = refs[2 * n_slots + 1:]
    step = pl.program_id(1)
    d = SB_HEAD_DIM
    sub = 8
    bias = bias_ref[...]

    @pl.when(step == 0)
    def _():
        z_new = jnp.sum(q_ref[...] * knew_ref[...], axis=1, keepdims=True) * (d ** -0.5) + bias
        visible = jnp.zeros(z_new.shape, jnp.bool_)
        ls_ref[...] = jnp.broadcast_to(jnp.where(visible, _softplus(z_new), 0.0), ls_ref.shape)
        new_ref[...] = jnp.where(visible, jnp.exp(z_new - _softplus(z_new)), 0.0) * vnew_ref[...]
        acc_ref[...] = jnp.zeros_like(acc_ref)

    r2 = lax.broadcasted_iota(jnp.int32, (page, 2 * page), 0)
    c2 = lax.broadcasted_iota(jnp.int32, (page, 2 * page), 1)
    after_ones = jnp.where((r2 > c2) | (c2 >= page), 1.0, 0.0).astype(BF16)
    z_rows = [[None] * heads for _ in range(n_slots)]
    for h in range(heads):
        qh = qb_ref[h]
        for s in range(n_slots):
            prod = k_refs[s][h] * qh
            z_rows[s][h] = jnp.sum(prod, axis=0, keepdims=True)
    z = jnp.concatenate([r for rows in z_rows for r in rows], axis=0) + jnp.tile(bias, (n_slots, 1))
    sp = _softplus(z)
    cum_tot = _dot_exact_rhs(sp, after_ones)
    base = z - sp - cum_tot[:, :page]
    log_surv = ls_ref[...]
    w = []
    for s in range(n_slots):
        w.append(jnp.exp(base[s * heads:(s + 1) * heads] - log_surv))
        log_surv = log_surv + cum_tot[s * heads:(s + 1) * heads, page:]
    ls_ref[...] = log_surv
    for h in range(heads):
        a = acc_ref[h].reshape(d // sub, sub, page)
        for s in range(n_slots):
            w_row = jnp.broadcast_to(w[s][h:h + 1, :], (sub, page))
            a = a + v_refs[s][h].reshape(d // sub, sub, page) * w_row[None]
        acc_ref[h] = a.reshape(d, page)

    @pl.when(step == pl.num_programs(1) - 1)
    def _():
        o_ref[...] = new_ref[...] + jnp.sum(acc_ref[...], axis=-1)


def _sb_sample(q, k_new, v_new, sb_bias, cache_k, cache_v, layer, page_table):
    bsz, heads, d = q.shape
    page = cache_k.shape[2]
    assert d == SB_HEAD_DIM and page == LANES
    n_pages = page_table.shape[1]
    n_slots = _pick(n_pages, (8, 4, 2, 1))
    n_steps = n_pages // n_slots
    k_t = jnp.transpose(cache_k, (0, 1, 3, 4, 2))
    v_t = jnp.transpose(cache_v, (0, 1, 3, 4, 2))
    q_lanes = jnp.broadcast_to((q * (d ** -0.5))[..., None], (bsz, heads, d, page))

    def page_map(slot):
        def index_map(b, s, pt):
            return (layer, pt[b, n_pages - 1 - (s * n_slots + slot)], 0, 0, 0)
        return index_map

    row_spec = pl.BlockSpec((None, heads, d), lambda b, s, pt: (b, 0, 0))
    page_specs = [pl.BlockSpec((None, None, heads, d, page), page_map(slot)) for slot in range(n_slots)]
    grid_spec = pltpu.PrefetchScalarGridSpec(
        num_scalar_prefetch=1,
        grid=(bsz, n_steps),
        in_specs=[row_spec, pl.BlockSpec((None, heads, d, page), lambda b, s, pt: (b, 0, 0, 0)), row_spec, row_spec,
                  pl.BlockSpec((heads, 1), lambda b, s, pt: (0, 0))] + page_specs + page_specs,
        out_specs=row_spec,
        scratch_shapes=[pltpu.VMEM((heads, page), F32), pltpu.VMEM((heads, d), F32), pltpu.VMEM((heads, d, page), F32)],
    )
    return pl.pallas_call(
        functools.partial(_sbs_kernel, n_slots=n_slots, heads=heads, page=page),
        grid_spec=grid_spec,
        out_shape=jax.ShapeDtypeStruct((bsz, heads, d), F32),
        compiler_params=_cparams("parallel", "arbitrary"),
        name="sb_sample",
    )(page_table, q, q_lanes, k_new, v_new, sb_bias.reshape(heads, 1), *([k_t] * n_slots), *([v_t] * n_slots))


def _conv_heads(c, gh):
    outs = []
    for hh in range(3 * gh):
        x = c[:, hh * LANES:(hh + 1) * LANES]
        if hh < 2 * gh:
            x = x * lax.rsqrt(jnp.sum(x * x, axis=-1, keepdims=True) + RMS_EPS)
            if hh < gh:
                x = x * (GDN_HEAD_DIM ** -0.5)
        outs.append(x)
    return outs


def _gprep_kernel(prev_ref, cur_ref, cw_ref, o_ref, ext_ref, *, tb, gh, width):
    t = pl.program_id(1)
    ext_ref[8:, :] = cur_ref[...]
    ext_ref[0:8, :] = jnp.where(t == 0, 0.0, prev_ref[...])
    off = 8 - (width - 1)
    acc = ext_ref[off:off + tb, :] * cw_ref[0:1, :]
    for i in range(1, width):
        acc = acc + ext_ref[off + i:off + i + tb, :] * cw_ref[i:i + 1, :]
    c = _silu(acc)
    for hh, x in enumerate(_conv_heads(c, gh)):
        o_ref[:, hh * LANES:(hh + 1) * LANES] = x


def _gdn_prep(z_main, conv_w, batch, seq, col_block, gh):
    width, cc = conv_w.shape
    tb = _pick(seq, (256, 128, 64, 32, 16, 8))
    nt = seq // tb
    return pl.pallas_call(
        functools.partial(_gprep_kernel, tb=tb, gh=gh, width=width),
        grid=(batch, nt),
        in_specs=[pl.BlockSpec((8, cc), lambda b, t: (jnp.maximum(b * (seq // 8) + t * (tb // 8) - 1, 0), col_block)),
                  pl.BlockSpec((tb, cc), lambda b, t: (b * nt + t, col_block)),
                  pl.BlockSpec((width, cc), lambda b, t: (0, 0))],
        out_specs=pl.BlockSpec((tb, cc), lambda b, t: (b * nt + t, 0)),
        out_shape=jax.ShapeDtypeStruct((batch * seq, cc), F32),
        scratch_shapes=[pltpu.VMEM((tb + 8, cc), F32)],
        compiler_params=_cparams("parallel", "arbitrary"),
        name="gdn_prep",
    )(z_main, z_main, conv_w)


def _unit_lower_inverses(mats):
    n = mats[0].shape[0]
    eye = jnp.where(lax.broadcasted_iota(jnp.int32, (n, n), 0) == lax.broadcasted_iota(jnp.int32, (n, n), 1), 1.0, 0.0)
    ps = [eye - a for a in mats]
    pws = [_dot3(a, a) for a in mats]
    size = 2
    while size < n:
        ps = [p + _dot3(p, pw) for p, pw in zip(ps, pws)]
        size *= 2
        if size < n:
            pws = [_dot3(pw, pw) for pw in pws]
    return ps


def _gdn_kernel(qkv_ref, ba_ref, zb_ref, alog_ref, dtb_ref, ng_ref, y_ref, sfin_ref, s_scr, *, gh):
    c = pl.program_id(1)
    n = GDN_CHUNK
    gw = gh * GDN_HEAD_DIM

    @pl.when(c == 0)
    def _():
        s_scr[...] = jnp.zeros_like(s_scr)

    row = lax.broadcasted_iota(jnp.int32, (n, n), 0)
    col = lax.broadcasted_iota(jnp.int32, (n, n), 1)
    tri = row >= col
    strict = row > col
    lmat = jnp.where(tri, 1.0, 0.0).astype(BF16)
    ba = ba_ref[...]
    heads = range(gh)
    q = [qkv_ref[:, h * LANES:(h + 1) * LANES] for h in heads]
    k = [qkv_ref[:, gw + h * LANES:gw + (h + 1) * LANES] for h in heads]
    v = [qkv_ref[:, 2 * gw + h * LANES:2 * gw + (h + 1) * LANES] for h in heads]
    beta_b = [jnp.broadcast_to(_sigmoid(ba[:, h:h + 1]), (n, GDN_HEAD_DIM)) for h in heads]
    g_b = [jnp.broadcast_to(-jnp.exp(alog_ref[0:1, h:h + 1])
                            * _softplus(ba[:, gh + h:gh + h + 1] + dtb_ref[0:1, h:h + 1]), (n, n)) for h in heads]
    gm = [_dot_exact_lhs(lmat, jnp.concatenate([jnp.where(strict, g, 0.0), g], axis=1)) for g in g_b]
    decay = [jnp.where(tri, jnp.exp(m[:, :n]), 0.0) for m in gm]
    gc = [m[:, n:] for m in gm]
    e_gc = [jnp.exp(x) for x in gc]
    e_rest = [jnp.exp(x[n - 1:n, :] - x) for x in gc]
    g_last = [jnp.exp(x[n - 1:n, :]) for x in gc]
    kbeta = [a * b for a, b in zip(k, beta_b)]
    a_low = [jnp.where(strict, _dot_nt(kb_, k_) * d, 0.0) for kb_, k_, d in zip(kbeta, k, decay)]
    qk = [jnp.where(tri, _dot_nt(q_, k_) * d, 0.0) for q_, k_, d in zip(q, k, decay)]
    t_inv = _unit_lower_inverses(a_low)
    sol = [_dot3(t, jnp.concatenate([v_ * b, kb_ * e], axis=1))
           for t, v_, b, kb_, e in zip(t_inv, v, beta_b, kbeta, e_gc)]
    s = [s_scr[h] for h in heads]
    v_new = [x[:, :GDN_HEAD_DIM] - _dot(x[:, GDN_HEAD_DIM:], s_) for x, s_ in zip(sol, s)]
    o = [_dot(q_ * e, s_) + _dot(qk_, vn) for q_, e, s_, qk_, vn in zip(q, e_gc, s, qk, v_new)]
    for h in heads:
        s_scr[h] = s[h] * g_last[h] + _dot_tn(k[h] * e_rest[h], v_new[h])
    for h in heads:
        on = o[h] * lax.rsqrt(jnp.mean(o[h] * o[h], axis=-1, keepdims=True) + RMS_EPS) * ng_ref[...]
        y_ref[:, h * LANES:(h + 1) * LANES] = (on * _silu(zb_ref[:, h * LANES:(h + 1) * LANES])).astype(y_ref.dtype)

    @pl.when(c == pl.num_programs(1) - 1)
    def _():
        sfin_ref[...] = s_scr[...]


def _gdn_prompt(qkv_n, ba, z_main, zb_col_block, a_log, dt_bias, norm_g, batch, seq, gh, out_dtype):
    assert seq % GDN_CHUNK == 0 and GDN_CHUNK == GDN_HEAD_DIM == LANES
    gw = gh * GDN_HEAD_DIM
    nc = seq // GDN_CHUNK
    return pl.pallas_call(
        functools.partial(_gdn_kernel, gh=gh),
        grid=(batch, nc),
        in_specs=[pl.BlockSpec((GDN_CHUNK, 3 * gw), lambda b, c: (b * nc + c, 0)),
                  pl.BlockSpec((GDN_CHUNK, 2 * gh), lambda b, c: (b * nc + c, 0)),
                  pl.BlockSpec((GDN_CHUNK, gw), lambda b, c: (b * nc + c, zb_col_block)),
                  pl.BlockSpec((1, gh), lambda b, c: (0, 0)),
                  pl.BlockSpec((1, gh), lambda b, c: (0, 0)),
                  pl.BlockSpec((1, GDN_HEAD_DIM), lambda b, c: (0, 0))],
        out_specs=[pl.BlockSpec((GDN_CHUNK, gw), lambda b, c: (b * nc + c, 0)),
                   pl.BlockSpec((None, gh, GDN_HEAD_DIM, GDN_HEAD_DIM), lambda b, c: (b, 0, 0, 0))],
        out_shape=[jax.ShapeDtypeStruct((batch * seq, gw), out_dtype),
                   jax.ShapeDtypeStruct((batch, gh, GDN_HEAD_DIM, GDN_HEAD_DIM), F32)],
        scratch_shapes=[pltpu.VMEM((gh, GDN_HEAD_DIM, GDN_HEAD_DIM), F32)],
        compiler_params=_cparams("parallel", "arbitrary"),
        name="gdn_prompt",
    )(qkv_n, ba, z_main, a_log.reshape(1, gh), dt_bias.reshape(1, gh), norm_g.reshape(1, GDN_HEAD_DIM))


def _gdns_kernel(sc_ref, zrow_ref, zb_ref, ba_ref, cw_ref, alog_ref, dtb_ref, ng_ref, s_ref,
                 y_ref, cnew_ref, snew_ref, *, gh, width):
    rows = [sc_ref[i:i + 1, :] for i in range(width - 1)] + [zrow_ref[...]]
    acc = rows[0] * cw_ref[0:1, :]
    for i in range(1, width):
        acc = acc + rows[i] * cw_ref[i:i + 1, :]
    for i in range(width - 1):
        cnew_ref[i:i + 1, :] = rows[i + 1]
    heads = _conv_heads(_silu(acc), gh)
    n = GDN_HEAD_DIM
    eye = lax.broadcasted_iota(jnp.int32, (n, n), 0) == lax.broadcasted_iota(jnp.int32, (n, n), 1)

    def column(r):
        return jnp.sum(jnp.where(eye, jnp.broadcast_to(r, (n, n)), 0.0), axis=1, keepdims=True)

    ba = ba_ref[...]
    for h in range(gh):
        q, k, v = heads[h], heads[gh + h], heads[2 * gh + h]
        beta = _sigmoid(ba[:, h:h + 1])
        g = -jnp.exp(alog_ref[0:1, h:h + 1]) * _softplus(ba[:, gh + h:gh + h + 1] + dtb_ref[0:1, h:h + 1])
        eg = jnp.exp(g)
        kcol = column(k)
        s = s_ref[h]
        ks = jnp.sum(kcol * s, axis=0, keepdims=True)
        v_new = beta * v - (beta * eg) * ks
        s_new = s * eg + kcol * v_new
        snew_ref[h] = s_new
        o = jnp.sum(column(q) * s_new, axis=0, keepdims=True)
        o = o * lax.rsqrt(jnp.mean(o * o, axis=-1, keepdims=True) + RMS_EPS) * ng_ref[...]
        y_ref[:, h * LANES:(h + 1) * LANES] = o * _silu(zb_ref[:, h * LANES:(h + 1) * LANES])


def _gdn_sample(state_conv, z3, ba3, conv_w, a_log, dt_bias, norm_g, state_ssm, qkvb_col_block, zb_col_block, gh):
    bsz, wm1, cc = state_conv.shape
    width = wm1 + 1
    gw = gh * GDN_HEAD_DIM
    small = lambda shape: pl.BlockSpec(shape, lambda b: (0,) * len(shape))
    return pl.pallas_call(
        functools.partial(_gdns_kernel, gh=gh, width=width),
        grid=(bsz,),
        in_specs=[pl.BlockSpec((None, wm1, cc), lambda b: (b, 0, 0)),
                  pl.BlockSpec((None, 1, cc), lambda b: (b, 0, qkvb_col_block)),
                  pl.BlockSpec((None, 1, gw), lambda b: (b, 0, zb_col_block)),
                  pl.BlockSpec((None, 1, 2 * gh), lambda b: (b, 0, 0)),
                  small((width, cc)), small((1, gh)), small((1, gh)), small((1, GDN_HEAD_DIM)),
                  pl.BlockSpec((None, gh, GDN_HEAD_DIM, GDN_HEAD_DIM), lambda b: (b, 0, 0, 0))],
        out_specs=[pl.BlockSpec((None, 1, gw), lambda b: (b, 0, 0)),
                   pl.BlockSpec((None, wm1, cc), lambda b: (b, 0, 0)),
                   pl.BlockSpec((None, gh, GDN_HEAD_DIM, GDN_HEAD_DIM), lambda b: (b, 0, 0, 0))],
        out_shape=[jax.ShapeDtypeStruct((bsz, 1, gw), F32),
                   jax.ShapeDtypeStruct((bsz, wm1, cc), F32),
                   jax.ShapeDtypeStruct((bsz, gh, GDN_HEAD_DIM, GDN_HEAD_DIM), F32)],
        compiler_params=_cparams("parallel"),
        name="gdn_sample",
    )(state_conv, z3, z3, ba3, conv_w, a_log.reshape(1, gh), dt_bias.reshape(1, gh),
      norm_g.reshape(1, GDN_HEAD_DIM), state_ssm)


def _mixa_kernel(x_ref, ya_ref, yb_ref, wga_ref, wgb_ref, wpa_ref, wpb_ref, o_ref, *, hp):
    x = x_ref[...]
    ga = _dot_nt(x, wga_ref[...], hp)
    gb = _dot_nt(x, wgb_ref[...], hp)
    pa = _dot(ya_ref[...], wpa_ref[...], hp)
    pb = _dot(yb_ref[...], wpb_ref[...], hp)
    o_ref[...] = (_sigmoid(ga) * pa + _sigmoid(gb) * pb).astype(o_ref.dtype)


def _mix_merge(x, ya, yb, w_gates, w_pa, w_pb, tm, tn, hp):
    m, dm = x.shape
    nj = dm // tn
    return pl.pallas_call(
        functools.partial(_mixa_kernel, hp=hp),
        grid=(m // tm, nj),
        in_specs=[pl.BlockSpec((tm, dm), lambda i, j: (i, 0)),
                  pl.BlockSpec((tm, ya.shape[1]), lambda i, j: (i, 0)),
                  pl.BlockSpec((tm, yb.shape[1]), lambda i, j: (i, 0)),
                  pl.BlockSpec((tn, dm), lambda i, j: (j, 0)),
                  pl.BlockSpec((tn, dm), lambda i, j: (nj + j, 0)),
                  pl.BlockSpec((w_pa.shape[0], tn), lambda i, j: (0, j)),
                  pl.BlockSpec((w_pb.shape[0], tn), lambda i, j: (0, j))],
        out_specs=pl.BlockSpec((tm, tn), lambda i, j: (i, j)),
        out_shape=jax.ShapeDtypeStruct((m, dm), F32 if hp else BF16),
        compiler_params=_cparams("parallel", "arbitrary"),
        name="mix_merge",
    )(x, ya, yb, w_gates, w_gates, w_pa, w_pb)


def _route(logits):
    lane = lax.broadcasted_iota(jnp.int32, logits.shape, 1).astype(F32)
    is_group = (lane >= N_EXPERTS) & (lane < N_EXPERTS + N_GROUPS)
    gl = jnp.where(is_group, logits, NEG_BIG)
    gmax = jnp.max(gl, axis=1, keepdims=True)
    gsel = jnp.min(jnp.where(gl == gmax, lane, 1e9), axis=1, keepdims=True) - N_EXPERTS
    p_group = 1.0 / jnp.sum(jnp.where(is_group, jnp.exp(gl - gmax), 0.0), axis=1, keepdims=True)
    in_group = (lane >= gsel * EXPERTS_PER_GROUP) & (lane < (gsel + 1.0) * EXPERTS_PER_GROUP)
    el = jnp.where(in_group, logits, NEG_BIG)
    m1 = jnp.max(el, axis=1, keepdims=True)
    i1 = jnp.min(jnp.where(el == m1, lane, 1e9), axis=1, keepdims=True)
    el2 = jnp.where(lane == i1, NEG_BIG, el)
    m2 = jnp.max(el2, axis=1, keepdims=True)
    i2 = jnp.min(jnp.where(el2 == m2, lane, 1e9), axis=1, keepdims=True)
    e2 = jnp.exp(m2 - m1)
    g1 = p_group / (1.0 + e2)
    g2 = p_group * e2 / (1.0 + e2)
    return jnp.where(lane == 0.0, i1, jnp.where(lane == 1.0, i2, jnp.where(lane == 2.0, g1,
                                                                            jnp.where(lane == 3.0, g2, 0.0))))


def _mixb_kernel(x_ref, m_ref, wout_ref, g_ref, b_ref, rw_ref, rb_ref, cnt0_ref,
                 hlo_ref, hhi_ref, route_ref, cnt_ref, cnt_scr, *, alpha, hp):
    @pl.when(pl.program_id(0) == 0)
    def _():
        cnt_scr[...] = cnt0_ref[...]

    t = alpha * x_ref[...] + _dot(m_ref[...], wout_ref[...], hp)
    h = _layer_norm(t, g_ref[...], b_ref[...])
    half = h.shape[1] // 2
    hlo_ref[...] = h[:, :half]
    hhi_ref[...] = h[:, half:]
    if hp:
        logits = _dot(h, rw_ref[...], True) + rb_ref[...]
    else:
        logits = _dot3(h, rw_ref[...]) + rb_ref[...]
    route = _route(logits)
    tm = h.shape[0]
    lane = lax.broadcasted_iota(jnp.int32, (tm, LANES), 1).astype(F32)
    first = jnp.where(lane == route[:, 0:1], 1.0, 0.0)
    second = jnp.where(lane == route[:, 1:2], 1.0, 0.0)
    earlier = jnp.where(lax.broadcasted_iota(jnp.int32, (tm, tm), 0) > lax.broadcasted_iota(jnp.int32, (tm, tm), 1),
                        1.0, 0.0).astype(BF16)
    n_first = jnp.sum(first, axis=0, keepdims=True)
    rank1 = jnp.dot(earlier, first.astype(BF16), preferred_element_type=F32) + cnt_scr[...]
    rank2 = jnp.dot(earlier, second.astype(BF16), preferred_element_type=F32) + cnt_scr[...] + n_first
    pos1 = jnp.sum(first * rank1, axis=1, keepdims=True)
    pos2 = jnp.sum(second * rank2, axis=1, keepdims=True)
    cnt_scr[...] = cnt_scr[...] + n_first + jnp.sum(second, axis=0, keepdims=True)
    cnt_ref[...] = cnt_scr[...]
    route_ref[...] = jnp.where(lane == 4.0, pos1, jnp.where(lane == 5.0, pos2, route))


def _mix_out(x, merged, w_out, ln_g, ln_b, router_w, router_b, counts0, alpha, tm, hp):
    m, dm = x.shape
    const = lambda shape: pl.BlockSpec(shape, lambda i: (0,) * len(shape))
    return pl.pallas_call(
        functools.partial(_mixb_kernel, alpha=alpha, hp=hp),
        grid=(m // tm,),
        in_specs=[pl.BlockSpec((tm, dm), lambda i: (i, 0)),
                  pl.BlockSpec((tm, dm), lambda i: (i, 0)),
                  const((dm, dm)), const((1, dm)), const((1, dm)), const((dm, LANES)), const((1, LANES)),
                  const((1, LANES))],
        out_specs=[pl.BlockSpec((tm, dm // 2), lambda i: (i, 0)),
                   pl.BlockSpec((tm, dm // 2), lambda i: (i, 0)),
                   pl.BlockSpec((tm, LANES), lambda i: (i, 0)),
                   const((1, LANES))],
        out_shape=[jax.ShapeDtypeStruct((m, dm // 2), F32), jax.ShapeDtypeStruct((m, dm // 2), F32),
                   jax.ShapeDtypeStruct((m, LANES), F32), jax.ShapeDtypeStruct((1, LANES), F32)],
        scratch_shapes=[pltpu.VMEM((1, LANES), F32)],
        compiler_params=_cparams("arbitrary"),
        name="mix_out_ln_route",
    )(x, merged, w_out, ln_g.reshape(1, dm), ln_b.reshape(1, dm), router_w, router_b, counts0)


MOE_TILE = 128
ROW_UNROLL = 8
MOE_COL_SPLIT = 4


def _moe_plan(route, counts, n_tok):
    experts = jnp.arange(N_EXPERTS, dtype=jnp.int32)
    padded = (counts[0, :N_EXPERTS].astype(jnp.int32) + MOE_TILE - 1) // MOE_TILE * MOE_TILE
    ends = jnp.cumsum(padded)
    starts = ends - padded
    eid = route[:, :2].astype(jnp.int32)
    pos = route[:, 4:6].astype(jnp.int32)
    seg = jnp.sum(jnp.where(eid[..., None] == experts, starts, 0), axis=-1)
    dest = (seg + pos).reshape(-1)
    n_rows = (2 * n_tok + MOE_TILE - 1) // MOE_TILE * MOE_TILE + N_EXPERTS * MOE_TILE
    tile_start = jnp.arange(n_rows // MOE_TILE, dtype=jnp.int32) * MOE_TILE
    tile_expert = jnp.minimum(jnp.sum((ends[None, :] <= tile_start[:, None]).astype(jnp.int32), axis=1),
                              N_EXPERTS - 1)
    n_valid = (ends[-1] // MOE_TILE).astype(jnp.int32).reshape(1)
    pad_lo = jnp.concatenate([starts + counts[0, :N_EXPERTS].astype(jnp.int32), ends[-1:]])
    pad_hi = jnp.concatenate([ends, jnp.full((1,), n_rows, jnp.int32)])
    return dest, tile_expert, n_valid, n_rows, pad_lo, pad_hi


def _invert_kernel(dest_ref, pad_lo_ref, pad_hi_ref, src_ref, *, n_tok):
    def clear_range(e, carry):
        def clear(r, inner):
            src_ref[r] = 0
            return inner

        lax.fori_loop(pad_lo_ref[e], pad_hi_ref[e], clear, 0)
        return carry

    def fill(t, carry):
        src_ref[dest_ref[2 * t]] = t
        src_ref[dest_ref[2 * t + 1]] = t
        return carry

    lax.fori_loop(0, pad_lo_ref.shape[0], clear_range, 0)
    lax.fori_loop(0, n_tok, fill, 0, unroll=ROW_UNROLL)


def _invert(dest, pad_lo, pad_hi, n_rows):
    grid_spec = pltpu.PrefetchScalarGridSpec(
        num_scalar_prefetch=3, grid=(1,), in_specs=[],
        out_specs=pl.BlockSpec(memory_space=pltpu.SMEM))
    return pl.pallas_call(
        functools.partial(_invert_kernel, n_tok=dest.shape[0] // 2),
        grid_spec=grid_spec,
        out_shape=jax.ShapeDtypeStruct((n_rows,), jnp.int32),
        compiler_params=_cparams("arbitrary"),
        name="moe_invert",
    )(dest, pad_lo, pad_hi)


def _gather_kernel(src_ref, x_ref, o_ref, rows_scr, *, rows):
    base = pl.program_id(0) * rows

    def body(r, carry):
        rows_scr[pl.ds(r, 1), :] = x_ref[pl.ds(src_ref[base + r], 1), :]
        return carry

    lax.fori_loop(0, rows, body, 0, unroll=ROW_UNROLL)
    o_ref[...] = rows_scr[...].astype(o_ref.dtype)


def _gather_rows(src, x):
    n_rows = src.shape[0]
    rows = _pick(n_rows, (256, 128))
    grid_spec = pltpu.PrefetchScalarGridSpec(
        num_scalar_prefetch=1,
        grid=(n_rows // rows,),
        in_specs=[pl.BlockSpec(memory_space=pltpu.VMEM)],
        out_specs=pl.BlockSpec((rows, x.shape[1]), lambda i, src: (i, 0)),
        scratch_shapes=[pltpu.VMEM((rows, x.shape[1]), F32)],
    )
    return pl.pallas_call(
        functools.partial(_gather_kernel, rows=rows),
        grid_spec=grid_spec,
        out_shape=jax.ShapeDtypeStruct((n_rows, x.shape[1]), BF16),
        compiler_params=_cparams("arbitrary"),
        name="moe_gather",
    )(src, x)


def _experts_kernel(te_ref, nv_ref, xlo_ref, xhi_ref, wg_ref, wu_ref, wd_ref, *refs):
    y_refs = refs[:MOE_COL_SPLIT]
    wg_scr, wu_scr, wd_scr = refs[MOE_COL_SPLIT:]
    i = pl.program_id(0)

    @pl.when((i == 0) | (te_ref[i] != te_ref[jnp.maximum(i - 1, 0)]))
    def _():
        wg_scr[...] = wg_ref[...].astype(BF16)
        wu_scr[...] = wu_ref[...].astype(BF16)
        wd_scr[...] = wd_ref[...].astype(BF16)

    @pl.when(i < nv_ref[0])
    def _():
        x = jnp.concatenate([xlo_ref[...], xhi_ref[...]], axis=1)
        a = jnp.dot(x, wg_scr[...], preferred_element_type=F32)
        u = jnp.dot(x, wu_scr[...], preferred_element_type=F32)
        y = jnp.dot((_silu(a) * u).astype(BF16), wd_scr[...], preferred_element_type=F32)
        q = y.shape[1] // MOE_COL_SPLIT
        for c, y_ref in enumerate(y_refs):
            y_ref[...] = y[:, c * q:(c + 1) * q]

    @pl.when(i >= nv_ref[0])
    def _():
        for y_ref in y_refs:
            y_ref[...] = jnp.zeros_like(y_ref)


def _experts(x_lo, x_hi, tile_expert, n_valid, w_gate, w_up, w_down):
    n_rows, half = x_lo.shape
    dm = 2 * half
    _, _, ff = w_gate.shape
    q = dm // MOE_COL_SPLIT
    grid_spec = pltpu.PrefetchScalarGridSpec(
        num_scalar_prefetch=2,
        grid=(n_rows // MOE_TILE,),
        in_specs=[pl.BlockSpec((MOE_TILE, half), lambda i, te, nv: (i, 0)),
                  pl.BlockSpec((MOE_TILE, half), lambda i, te, nv: (i, 0)),
                  pl.BlockSpec((None, dm, ff), lambda i, te, nv: (te[i], 0, 0)),
                  pl.BlockSpec((None, dm, ff), lambda i, te, nv: (te[i], 0, 0)),
                  pl.BlockSpec((None, ff, dm), lambda i, te, nv: (te[i], 0, 0))],
        out_specs=[pl.BlockSpec((MOE_TILE, q), lambda i, te, nv: (i, 0))] * MOE_COL_SPLIT,
        scratch_shapes=[pltpu.VMEM((dm, ff), BF16), pltpu.VMEM((dm, ff), BF16), pltpu.VMEM((ff, dm), BF16)],
    )
    return pl.pallas_call(
        _experts_kernel,
        grid_spec=grid_spec,
        out_shape=[jax.ShapeDtypeStruct((n_rows, q), F32)] * MOE_COL_SPLIT,
        compiler_params=_cparams("arbitrary"),
        name="moe_experts",
    )(tile_expert, n_valid, x_lo, x_hi, w_gate, w_up, w_down)


def _combine_kernel(dest_ref, gate_ref, y_ref, o_ref, *, rows):
    base = pl.program_id(0) * rows

    def body(t, carry):
        a = 2 * (base + t)
        first = y_ref[pl.ds(dest_ref[a], 1), :]
        second = y_ref[pl.ds(dest_ref[a + 1], 1), :]
        o_ref[pl.ds(t, 1), :] = gate_ref[a] * first + gate_ref[a + 1] * second
        return carry

    lax.fori_loop(0, rows, body, 0, unroll=ROW_UNROLL)


def _combine(dest, gates, y_slab):
    n_tok = dest.shape[0] // 2
    rows = _pick(n_tok, (256, 128, 64, 32))
    width = y_slab.shape[1]
    grid_spec = pltpu.PrefetchScalarGridSpec(
        num_scalar_prefetch=2,
        grid=(n_tok // rows,),
        in_specs=[pl.BlockSpec(memory_space=pltpu.VMEM)],
        out_specs=pl.BlockSpec((rows, width), lambda i, dest, gates: (i, 0)),
    )
    return pl.pallas_call(
        functools.partial(_combine_kernel, rows=rows),
        grid_spec=grid_spec,
        out_shape=jax.ShapeDtypeStruct((n_tok, width), F32),
        compiler_params=_cparams("arbitrary"),
        name="moe_combine",
    )(dest, gates, y_slab)


def _ln2_kernel(hlo_ref, hhi_ref, *refs, alpha):
    moe_refs = refs[:MOE_COL_SPLIT]
    g_ref, b_ref, o_ref = refs[MOE_COL_SPLIT:]
    h = jnp.concatenate([hlo_ref[...], hhi_ref[...]], axis=1)
    moe = jnp.concatenate([r[...] for r in moe_refs], axis=1)
    o_ref[...] = _layer_norm(alpha * h + moe, g_ref[...], b_ref[...])


def _ln2(h_lo, h_hi, moe_slabs, row_block0, ln_g, ln_b, alpha, tm):
    m, half = h_lo.shape
    dm = 2 * half
    q = dm // MOE_COL_SPLIT
    return pl.pallas_call(
        functools.partial(_ln2_kernel, alpha=alpha),
        grid=(m // tm,),
        in_specs=[pl.BlockSpec((tm, half), lambda i: (i, 0)), pl.BlockSpec((tm, half), lambda i: (i, 0))]
        + [pl.BlockSpec((tm, q), lambda i: (i + row_block0, 0))] * MOE_COL_SPLIT
        + [pl.BlockSpec((1, dm), lambda i: (0, 0)), pl.BlockSpec((1, dm), lambda i: (0, 0))],
        out_specs=pl.BlockSpec((tm, dm), lambda i: (i, 0)),
        out_shape=jax.ShapeDtypeStruct((m, dm), F32),
        compiler_params=_cparams("parallel"),
        name="moe_ln2",
    )(h_lo, h_hi, *moe_slabs, ln_g.reshape(1, dm), ln_b.reshape(1, dm))


def _moe(h_p, route_p, h_s, route_s, counts, w_gate, w_up, w_down, ln_g, ln_b, alpha):
    m_p, m_s = h_p[0].shape[0], h_s[0].shape[0]
    n_tok = m_p + m_s
    assert m_p % m_s == 0 and m_s % 8 == 0
    n_pad = (n_tok + 255) // 256 * 256
    route = jnp.concatenate([route_p, route_s], axis=0)
    dest, tile_expert, n_valid, n_rows, pad_lo, pad_hi = _moe_plan(route, counts, n_tok)
    src = _invert(dest, pad_lo, pad_hi, n_rows)
    x_lo = _gather_rows(src, jnp.concatenate([h_p[0], h_s[0]], axis=0))
    x_hi = _gather_rows(src, jnp.concatenate([h_p[1], h_s[1]], axis=0))
    y_slabs = _experts(x_lo, x_hi, tile_expert, n_valid, w_gate, w_up, w_down)
    gates_pad = jnp.concatenate([route[:, 2:4].reshape(-1), jnp.zeros((2 * (n_pad - n_tok),), F32)])
    dest_pad = jnp.concatenate([dest, jnp.zeros((2 * (n_pad - n_tok),), jnp.int32)])
    moe_slabs = [_combine(dest_pad, gates_pad, y) for y in y_slabs]
    out_p = _ln2(h_p[0], h_p[1], moe_slabs, 0, ln_g, ln_b, alpha, _pick(m_p, (256, 128, 64, 32, 16, 8)))
    out_s = _ln2(h_s[0], h_s[1], moe_slabs, m_p // m_s, ln_g, ln_b, alpha, m_s)
    return out_p, out_s


def _layer(hp_x, hs_x, cache_k, cache_v, page_table, state_conv, state_ssm, layer, w, alpha):
    batch, seq, dm = hp_x.shape
    dec_batch, dec_seq, _ = hs_x.shape
    assert dec_seq == 1
    sb_heads = w["sb_bias"].shape[0]
    gh = w["a_log"].shape[0]
    sbw = sb_heads * SB_HEAD_DIM
    gw = gh * GDN_HEAD_DIM
    cc = 3 * gw
    n_main = 3 * sbw + cc + gw
    assert sbw % LANES == 0 and 3 * sbw % cc == 0 and n_main % gw == 0 and sbw == gw
    qkvb_block = 3 * sbw // cc
    zb_block = (3 * sbw + cc) // gw
    w_t = jnp.transpose(w["w_in"])
    w_ba = w_t[n_main:n_main + 2 * gh]
    w_gates = w_t[n_main + 2 * gh:]
    router_w = jnp.concatenate(
        [jnp.transpose(w["router_expert"], (1, 0, 2)).reshape(dm, N_EXPERTS), w["router_group"],
         jnp.zeros((dm, LANES - N_EXPERTS - N_GROUPS), F32)], axis=1)
    router_b = jnp.concatenate(
        [w["router_expert_b"].reshape(N_EXPERTS), w["router_group_b"],
         jnp.zeros((LANES - N_EXPERTS - N_GROUPS,), F32)]).reshape(1, LANES)

    m = batch * seq
    xp = hp_x.reshape(m, dm)
    tm = _pick(m, (1024, 512, 256, 128, 64, 32, 16, 8))
    z_main = _matmul(xp, w_t, n_main, tm, _pick(n_main, (512, 256, 128)), False)
    ba = _matmul(xp, w_ba, 2 * gh, tm, 2 * gh, False)
    y_a, k_t, v_t = _sb_prompt(z_main, w["sb_bias"], batch, seq, sbw, BF16)
    qkv_n = _gdn_prep(z_main, w["conv_w"], batch, seq, qkvb_block, gh)
    y_b, ssm_p = _gdn_prompt(qkv_n, ba, z_main, zb_block, w["a_log"], w["dt_bias"], w["gdn_norm_g"],
                             batch, seq, gh, BF16)
    merged = _mix_merge(xp, y_a, y_b, w_gates.astype(BF16), w["w_proj_a"].astype(BF16), w["w_proj_b"].astype(BF16),
                        tm, _pick(dm, (512, 256, 128)), False)
    h1_lo, h1_hi, route, counts = _mix_out(xp, merged, w["w_out"].astype(BF16), w["ln1_g"], w["ln1_b"], router_w, router_b,
                                      jnp.zeros((1, LANES), F32), alpha, _pick(m, (256, 128, 64, 32, 16, 8)), False)
    k_p = jnp.transpose(k_t.reshape(batch, sb_heads, SB_HEAD_DIM, seq), (0, 3, 1, 2))
    v_p = jnp.transpose(v_t.reshape(batch, sb_heads, SB_HEAD_DIM, seq), (0, 3, 1, 2))
    conv_p = z_main.reshape(batch, seq, n_main)[:, seq - (w["conv_w"].shape[0] - 1):, 3 * sbw:3 * sbw + cc]

    xs = hs_x.reshape(dec_batch, dm)
    zs = _matmul(xs, w_t, n_main, dec_batch, _pick(n_main, (1024, 512, 256, 128)), True)
    ba_s = _matmul(xs, w_ba, 2 * gh, dec_batch, 2 * gh, True)
    q_s = zs[:, :sbw].reshape(dec_batch, sb_heads, SB_HEAD_DIM)
    k_s = zs[:, sbw:2 * sbw].reshape(dec_batch, sb_heads, SB_HEAD_DIM)
    v_s = zs[:, 2 * sbw:3 * sbw].reshape(dec_batch, sb_heads, SB_HEAD_DIM)
    ya_s = _sb_sample(q_s, k_s, v_s, w["sb_bias"], cache_k, cache_v, layer, page_table).reshape(dec_batch, sbw)
    yb_s, conv_s, ssm_s = _gdn_sample(state_conv, zs.reshape(dec_batch, 1, n_main), ba_s.reshape(dec_batch, 1, 2 * gh),
                                      w["conv_w"], w["a_log"], w["dt_bias"], w["gdn_norm_g"], state_ssm,
                                      qkvb_block, zb_block, gh)
    merged_s = _mix_merge(xs, ya_s, yb_s.reshape(dec_batch, gw), w_gates, w["w_proj_a"], w["w_proj_b"],
                          dec_batch, _pick(dm, (512, 256, 128)), True)
    h1s_lo, h1s_hi, route_s, counts = _mix_out(xs, merged_s, w["w_out"], w["ln1_g"], w["ln1_b"], router_w, router_b,
                                               counts, alpha, dec_batch, True)

    out_p, out_s = _moe((h1_lo, h1_hi), route, (h1s_lo, h1s_hi), route_s, counts, w["w_gate"], w["w_up"],
                        w["w_down"], w["ln2_g"], w["ln2_b"], alpha)

    return (out_p.reshape(batch, seq, dm), out_s.reshape(dec_batch, 1, dm), k_p, v_p,
            k_s.reshape(dec_batch, 1, sb_heads, SB_HEAD_DIM), v_s.reshape(dec_batch, 1, sb_heads, SB_HEAD_DIM),
            conv_p, conv_s, ssm_p, ssm_s)


def kernel(x_prompt, x_sample, cache_k, cache_v, page_table, state_conv, state_ssm, w_in, sb_bias, conv_w, a_log,
           dt_bias, gdn_norm_g, w_proj_a, w_proj_b, w_out, ln1_g, ln1_b, router_group, router_group_b,
           router_expert, router_expert_b, w_gate, w_up, w_down, ln2_g, ln2_b):
    depth = w_in.shape[0]
    alpha = (2.0 * depth) ** 0.25
    stacked = dict(w_in=w_in, sb_bias=sb_bias, conv_w=conv_w, a_log=a_log, dt_bias=dt_bias, gdn_norm_g=gdn_norm_g,
                   w_proj_a=w_proj_a, w_proj_b=w_proj_b, w_out=w_out, ln1_g=ln1_g, ln1_b=ln1_b,
                   router_group=router_group, router_group_b=router_group_b, router_expert=router_expert,
                   router_expert_b=router_expert_b, w_gate=w_gate, w_up=w_up, w_down=w_down, ln2_g=ln2_g, ln2_b=ln2_b)
    hp_x, hs_x = x_prompt, x_sample
    per_layer = []
    for layer in range(depth):
        w = {name: t[layer] for name, t in stacked.items()}
        outs = _layer(hp_x, hs_x, cache_k, cache_v, page_table, state_conv[layer], state_ssm[layer], layer, w, alpha)
        hp_x, hs_x = outs[0], outs[1]
        per_layer.append(outs[2:])
    return (hp_x, hs_x) + tuple(jnp.stack([o[i] for o in per_layer]) for i in range(8))
```

```python
import functools

import jax
import jax.numpy as jnp
from jax import lax
from jax.experimental import pallas as pl
from jax.experimental.pallas import tpu as pltpu

F32 = jnp.float32
BF16 = jnp.bfloat16

LANES = 128
SB_HEAD_DIM = 64
GDN_HEAD_DIM = 128
GDN_CHUNK = 128
LN_EPS = 1e-5
RMS_EPS = 1e-6
N_GROUPS = 4
EXPERTS_PER_GROUP = 8
N_EXPERTS = N_GROUPS * EXPERTS_PER_GROUP
NEG_BIG = -1e30
LOG2E = 1.4426950408889634
VMEM_LIMIT_BYTES = 58 * 1024 * 1024


def _cparams(*sem):
    return pltpu.CompilerParams(dimension_semantics=sem, vmem_limit_bytes=VMEM_LIMIT_BYTES)


def _pick(n, prefs):
    for p in prefs:
        if n % p == 0:
            return p
    return n


def _dot(a, b, hp=False):
    if hp:
        return jnp.dot(a.astype(F32), b.astype(F32), preferred_element_type=F32, precision=lax.Precision.HIGHEST)
    return jnp.dot(a.astype(BF16), b.astype(BF16), preferred_element_type=F32)


def _dot_nt(a, b, hp=False):
    dims = (((1,), (1,)), ((), ()))
    if hp:
        return lax.dot_general(a.astype(F32), b.astype(F32), dims, preferred_element_type=F32,
                               precision=lax.Precision.HIGHEST)
    return lax.dot_general(a.astype(BF16), b.astype(BF16), dims, preferred_element_type=F32)


def _dot_tn(a, b):
    return lax.dot_general(a.astype(BF16), b.astype(BF16), (((0,), (0,)), ((), ())), preferred_element_type=F32)


def _split2(a):
    hi = a.astype(BF16)
    lo = (a - hi.astype(F32)).astype(BF16)
    return hi, lo


def _split3(a):
    hi = a.astype(BF16)
    r = a - hi.astype(F32)
    mid = r.astype(BF16)
    lo = (r - mid.astype(F32)).astype(BF16)
    return hi, mid, lo


def _dot3(a, b):
    ah, al = _split2(a)
    bh, bl = _split2(b)
    d = functools.partial(jnp.dot, preferred_element_type=F32)
    return d(ah, bh) + d(ah, bl) + d(al, bh)


def _dot_exact_lhs(a_bf16, b):
    bh, bm, bl = _split3(b)
    d = functools.partial(jnp.dot, preferred_element_type=F32)
    return d(a_bf16, bh) + d(a_bf16, bm) + d(a_bf16, bl)


def _dot_exact_rhs(a, b_bf16):
    ah, am, al = _split3(a)
    d = functools.partial(jnp.dot, preferred_element_type=F32)
    return d(ah, b_bf16) + d(am, b_bf16) + d(al, b_bf16)


def _softplus(z):
    return jnp.maximum(z, 0.0) + jnp.log(1.0 + jnp.exp(-jnp.abs(z)))


def _sigmoid(z):
    return 1.0 / (1.0 + jnp.exp(-z))


def _silu(z):
    return z * _sigmoid(z)


def _layer_norm(t, g, b):
    mu = jnp.mean(t, axis=-1, keepdims=True)
    c = t - mu
    var = jnp.mean(c * c, axis=-1, keepdims=True)
    return c * lax.rsqrt(var + LN_EPS) * g + b


def _mm_kernel(x_ref, wt_ref, o_ref, *, hp):
    o_ref[...] = _dot_nt(x_ref[...], wt_ref[...], hp)


def _matmul(x, w_t, n_cols, tm, tn, hp):
    m, k = x.shape
    return pl.pallas_call(
        functools.partial(_mm_kernel, hp=hp),
        grid=(m // tm, n_cols // tn),
        in_specs=[pl.BlockSpec((tm, k), lambda i, j: (i, 0)),
                  pl.BlockSpec((tn, k), lambda i, j: (j, 0))],
        out_specs=pl.BlockSpec((tm, tn), lambda i, j: (i, j)),
        out_shape=jax.ShapeDtypeStruct((m, n_cols), F32),
        compiler_params=_cparams("parallel", "arbitrary"),
        name="proj_matmul",
    )(x, w_t)


def _sbp_kernel(bias_ref, q_ref, k_ref, v_ref, o_ref, kt_ref, vt_ref, *, tq, kb):
    hpair = pl.program_id(1)
    qi = pl.program_id(2)
    q = q_ref[...] * (SB_HEAD_DIM ** -0.5 * LOG2E)
    lane = lax.broadcasted_iota(jnp.int32, (1, LANES), 1)
    row = lax.broadcasted_iota(jnp.int32, (tq, kb), 0)
    col = lax.broadcasted_iota(jnp.int32, (tq, kb), 1)
    r2 = lax.broadcasted_iota(jnp.int32, (kb, kb), 0)
    c2 = lax.broadcasted_iota(jnp.int32, (kb, kb), 1)
    after = jnp.where(r2 > c2, 1.0, 0.0).astype(BF16)
    n_diag = tq // kb
    n_heads = LANES // SB_HEAD_DIM
    head_lanes = [(lane >= h * SB_HEAD_DIM) & (lane < (h + 1) * SB_HEAD_DIM) for h in range(n_heads)]
    q_stack = jnp.concatenate([jnp.where(m, q, 0.0) for m in head_lanes], axis=0).astype(BF16)
    bias = jnp.concatenate([jnp.full((tq, 1), bias_ref[hpair * n_heads + h] * LOG2E, F32) for h in range(n_heads)],
                           axis=0)
    row = jnp.concatenate([row] * n_heads, axis=0)
    col = jnp.concatenate([col] * n_heads, axis=0)

    def block(kstart, carry, masked):
        log_surv, acc = carry
        kblk = k_ref[pl.ds(kstart, kb), :]
        vblk = v_ref[pl.ds(kstart, kb), :]
        z = _dot_nt(q_stack, kblk) + bias
        sp = jnp.maximum(z, 0.0) + jnp.log2(1.0 + jnp.exp2(-jnp.abs(z)))
        if masked:
            visible = (kstart + col) < (qi * tq + row)
            sp = jnp.where(visible, sp, 0.0)
        later = jnp.dot(sp.astype(BF16), after, preferred_element_type=F32) + log_surv
        w = jnp.exp2(z - sp - later)
        if masked:
            w = jnp.where(visible, w, 0.0)
        w = w.astype(BF16)
        for h in range(n_heads):
            v_h = jnp.where(head_lanes[h], vblk, 0.0).astype(BF16)
            acc = acc + jnp.dot(w[h * tq:(h + 1) * tq], v_h, preferred_element_type=F32)
        log_surv = log_surv + jnp.sum(sp, axis=1, keepdims=True)
        return log_surv, acc

    carry = (jnp.zeros((n_heads * tq, 1), F32), jnp.zeros((tq, LANES), F32))
    for d in range(n_diag):
        kstart = pl.multiple_of(qi * tq + (n_diag - 1 - d) * kb, kb)
        carry = block(kstart, carry, True)
    n_before = qi * n_diag

    def body(i, carry):
        for d in range(n_diag):
            kstart = pl.multiple_of((n_before - 1 - i * n_diag - d) * kb, kb)
            carry = block(kstart, carry, False)
        return carry

    carry = lax.fori_loop(0, qi, body, carry)
    o_ref[...] = carry[1].astype(o_ref.dtype)

    @pl.when(qi == 0)
    def _():
        for start in range(0, k_ref.shape[0], LANES):
            kt_ref[:, start:start + LANES] = k_ref[start:start + LANES, :].T
            vt_ref[:, start:start + LANES] = v_ref[start:start + LANES, :].T


def _sb_prompt(z_main, sb_bias, batch, seq, sbw, out_dtype):
    tq = _pick(seq, (512, 256, 128))
    kb = _pick(tq, (256, 128))
    nq = seq // tq
    n_pairs = sbw // LANES
    grid_spec = pltpu.PrefetchScalarGridSpec(
        num_scalar_prefetch=1,
        grid=(batch, n_pairs, nq),
        in_specs=[pl.BlockSpec((tq, LANES), lambda b, h, i, bias: (b * nq + i, h)),
                  pl.BlockSpec((seq, LANES), lambda b, h, i, bias: (b, n_pairs + h)),
                  pl.BlockSpec((seq, LANES), lambda b, h, i, bias: (b, 2 * n_pairs + h))],
        out_specs=[pl.BlockSpec((tq, LANES), lambda b, h, i, bias: (b * nq + i, h)),
                   pl.BlockSpec((None, LANES, seq), lambda b, h, i, bias: (b, h, 0)),
                   pl.BlockSpec((None, LANES, seq), lambda b, h, i, bias: (b, h, 0))],
    )
    return pl.pallas_call(
        functools.partial(_sbp_kernel, tq=tq, kb=kb),
        grid_spec=grid_spec,
        out_shape=[jax.ShapeDtypeStruct((batch * seq, sbw), out_dtype),
                   jax.ShapeDtypeStruct((batch, sbw, seq), F32),
                   jax.ShapeDtypeStruct((batch, sbw, seq), F32)],
        compiler_params=_cparams("parallel", "parallel", "arbitrary"),
        name="sb_prompt",
    )(sb_bias, z_main, z_main, z_main)


def _sbs_kernel(pt_ref, q_ref, qb_ref, knew_ref, vnew_ref, bias_ref, *refs, n_slots, heads, page):
    k_refs = refs[:n_slots]
    v_refs = refs[n_slots:2 * n_slots]
    o_ref = refs[2 * n_slots]
    ls_ref, new_ref, acc_ref = refs[2 * n_slots + 1:]
    step = pl.program_id(1)
    d = SB_HEAD_DIM
    sub = 8
    bias = bias_ref[...]

    @pl.when(step == 0)
    def _():
        z_new = jnp.sum(q_ref[...] * knew_ref[...], axis=1, keepdims=True) * (d ** -0.5) + bias
        visible = jnp.zeros(z_new.shape, jnp.bool_)
        ls_ref[...] = jnp.broadcast_to(jnp.where(visible, _softplus(z_new), 0.0), ls_ref.shape)
        new_ref[...] = jnp.where(visible, jnp.exp(z_new - _softplus(z_new)), 0.0) * vnew_ref[...]
        acc_ref[...] = jnp.zeros_like(acc_ref)

    r2 = lax.broadcasted_iota(jnp.int32, (page, 2 * page), 0)
    c2 = lax.broadcasted_iota(jnp.int32, (page, 2 * page), 1)
    after_ones = jnp.where((r2 > c2) | (c2 >= page), 1.0, 0.0).astype(BF16)
    z_rows = [[None] * heads for _ in range(n_slots)]
    for h in range(heads):
        qh = qb_ref[h]
        for s in range(n_slots):
            prod = k_refs[s][h] * qh
            z_rows[s][h] = jnp.sum(prod, axis=0, keepdims=True)
    z = jnp.concatenate([r for rows in z_rows for r in rows], axis=0) + jnp.tile(bias, (n_slots, 1))
    sp = _softplus(z)
    cum_tot = _dot_exact_rhs(sp, after_ones)
    base = z - sp - cum_tot[:, :page]
    log_surv = ls_ref[...]
    w = []
    for s in range(n_slots):
        w.append(jnp.exp(base[s * heads:(s + 1) * heads] - log_surv))
        log_surv = log_surv + cum_tot[s * heads:(s + 1) * heads, page:]
    ls_ref[...] = log_surv
    for h in range(heads):
        a = acc_ref[h].reshape(d // sub, sub, page)
        for s in range(n_slots):
            w_row = jnp.broadcast_to(w[s][h:h + 1, :], (sub, page))
            a = a + v_refs[s][h].reshape(d // sub, sub, page) * w_row[None]
        acc_ref[h] = a.reshape(d, page)

    @pl.when(step == pl.num_programs(1) - 1)
    def _():
        o_ref[...] = new_ref[...] + jnp.sum(acc_ref[...], axis=-1)


def _sb_sample(q, k_new, v_new, sb_bias, cache_k, cache_v, layer, page_table):
    bsz, heads, d = q.shape
    page = cache_k.shape[2]
    assert d == SB_HEAD_DIM and page == LANES
    n_pages = page_table.shape[1]
    n_slots = _pick(n_pages, (16, 8, 4, 2, 1))
    n_steps = n_pages // n_slots
    k_t = jnp.transpose(cache_k, (0, 1, 3, 4, 2))
    v_t = jnp.transpose(cache_v, (0, 1, 3, 4, 2))
    q_lanes = jnp.broadcast_to((q * (d ** -0.5))[..., None], (bsz, heads, d, page))

    def page_map(slot):
        def index_map(b, s, pt):
            return (layer, pt[b, n_pages - 1 - (s * n_slots + slot)], 0, 0, 0)
        return index_map

    row_spec = pl.BlockSpec((None, heads, d), lambda b, s, pt: (b, 0, 0))
    page_specs = [pl.BlockSpec((None, None, heads, d, page), page_map(slot)) for slot in range(n_slots)]
    grid_spec = pltpu.PrefetchScalarGridSpec(
        num_scalar_prefetch=1,
        grid=(bsz, n_steps),
        in_specs=[row_spec, pl.BlockSpec((None, heads, d, page), lambda b, s, pt: (b, 0, 0, 0)), row_spec, row_spec,
                  pl.BlockSpec((heads, 1), lambda b, s, pt: (0, 0))] + page_specs + page_specs,
        out_specs=row_spec,
        scratch_shapes=[pltpu.VMEM((heads, page), F32), pltpu.VMEM((heads, d), F32), pltpu.VMEM((heads, d, page), F32)],
    )
    return pl.pallas_call(
        functools.partial(_sbs_kernel, n_slots=n_slots, heads=heads, page=page),
        grid_spec=grid_spec,
        out_shape=jax.ShapeDtypeStruct((bsz, heads, d), F32),
        compiler_params=_cparams("parallel", "arbitrary"),
        name="sb_sample",
    )(page_table, q, q_lanes, k_new, v_new, sb_bias.reshape(heads, 1), *([k_t] * n_slots), *([v_t] * n_slots))


def _conv_heads(c, gh):
    outs = []
    for hh in range(3 * gh):
        x = c[:, hh * LANES:(hh + 1) * LANES]
        if hh < 2 * gh:
            x = x * lax.rsqrt(jnp.sum(x * x, axis=-1, keepdims=True) + RMS_EPS)
            if hh < gh:
                x = x * (GDN_HEAD_DIM ** -0.5)
        outs.append(x)
    return outs


def _gprep_kernel(prev_ref, cur_ref, cw_ref, o_ref, ext_ref, *, tb, gh, width):
    t = pl.program_id(1)
    ext_ref[8:, :] = cur_ref[...]
    ext_ref[0:8, :] = jnp.where(t == 0, 0.0, prev_ref[...])
    off = 8 - (width - 1)
    acc = ext_ref[off:off + tb, :] * cw_ref[0:1, :]
    for i in range(1, width):
        acc = acc + ext_ref[off + i:off + i + tb, :] * cw_ref[i:i + 1, :]
    c = _silu(acc)
    for hh, x in enumerate(_conv_heads(c, gh)):
        o_ref[:, hh * LANES:(hh + 1) * LANES] = x


def _gdn_prep(z_main, conv_w, batch, seq, col_block, gh):
    width, cc = conv_w.shape
    tb = _pick(seq, (256, 128, 64, 32, 16, 8))
    nt = seq // tb
    return pl.pallas_call(
        functools.partial(_gprep_kernel, tb=tb, gh=gh, width=width),
        grid=(batch, nt),
        in_specs=[pl.BlockSpec((8, cc), lambda b, t: (jnp.maximum(b * (seq // 8) + t * (tb // 8) - 1, 0), col_block)),
                  pl.BlockSpec((tb, cc), lambda b, t: (b * nt + t, col_block)),
                  pl.BlockSpec((width, cc), lambda b, t: (0, 0))],
        out_specs=pl.BlockSpec((tb, cc), lambda b, t: (b * nt + t, 0)),
        out_shape=jax.ShapeDtypeStruct((batch * seq, cc), F32),
        scratch_shapes=[pltpu.VMEM((tb + 8, cc), F32)],
        compiler_params=_cparams("parallel", "arbitrary"),
        name="gdn_prep",
    )(z_main, z_main, conv_w)


def _unit_lower_inverses(mats):
    n = mats[0].shape[0]
    eye = jnp.where(lax.broadcasted_iota(jnp.int32, (n, n), 0) == lax.broadcasted_iota(jnp.int32, (n, n), 1), 1.0, 0.0)
    ps = [eye - a for a in mats]
    pws = [_dot3(a, a) for a in mats]
    size = 2
    while size < n:
        ps = [p + _dot3(p, pw) for p, pw in zip(ps, pws)]
        size *= 2
        if size < n:
            pws = [_dot3(pw, pw) for pw in pws]
    return ps


def _gdn_kernel(qkv_ref, ba_ref, zb_ref, alog_ref, dtb_ref, ng_ref, y_ref, sfin_ref, s_scr, *, gh):
    c = pl.program_id(1)
    n = GDN_CHUNK
    gw = gh * GDN_HEAD_DIM

    @pl.when(c == 0)
    def _():
        s_scr[...] = jnp.zeros_like(s_scr)

    row = lax.broadcasted_iota(jnp.int32, (n, n), 0)
    col = lax.broadcasted_iota(jnp.int32, (n, n), 1)
    tri = row >= col
    strict = row > col
    lmat = jnp.where(tri, 1.0, 0.0).astype(BF16)
    ba = ba_ref[...]
    heads = range(gh)
    q = [qkv_ref[:, h * LANES:(h + 1) * LANES] for h in heads]
    k = [qkv_ref[:, gw + h * LANES:gw + (h + 1) * LANES] for h in heads]
    v = [qkv_ref[:, 2 * gw + h * LANES:2 * gw + (h + 1) * LANES] for h in heads]
    beta_b = [jnp.broadcast_to(_sigmoid(ba[:, h:h + 1]), (n, GDN_HEAD_DIM)) for h in heads]
    g_b = [jnp.broadcast_to(-jnp.exp(alog_ref[0:1, h:h + 1])
                            * _softplus(ba[:, gh + h:gh + h + 1] + dtb_ref[0:1, h:h + 1]), (n, n)) for h in heads]
    gm = [_dot_exact_lhs(lmat, jnp.concatenate([jnp.where(strict, g, 0.0), g], axis=1)) for g in g_b]
    decay = [jnp.where(tri, jnp.exp(m[:, :n]), 0.0) for m in gm]
    gc = [m[:, n:] for m in gm]
    e_gc = [jnp.exp(x) for x in gc]
    e_rest = [jnp.exp(x[n - 1:n, :] - x) for x in gc]
    g_last = [jnp.exp(x[n - 1:n, :]) for x in gc]
    kbeta = [a * b for a, b in zip(k, beta_b)]
    a_low = [jnp.where(strict, _dot_nt(kb_, k_) * d, 0.0) for kb_, k_, d in zip(kbeta, k, decay)]
    qk = [jnp.where(tri, _dot_nt(q_, k_) * d, 0.0) for q_, k_, d in zip(q, k, decay)]
    t_inv = _unit_lower_inverses(a_low)
    sol = [_dot3(t, jnp.concatenate([v_ * b, kb_ * e], axis=1))
           for t, v_, b, kb_, e in zip(t_inv, v, beta_b, kbeta, e_gc)]
    s = [s_scr[h] for h in heads]
    v_new = [x[:, :GDN_HEAD_DIM] - _dot(x[:, GDN_HEAD_DIM:], s_) for x, s_ in zip(sol, s)]
    o = [_dot(q_ * e, s_) + _dot(qk_, vn) for q_, e, s_, qk_, vn in zip(q, e_gc, s, qk, v_new)]
    for h in heads:
        s_scr[h] = s[h] * g_last[h] + _dot_tn(k[h] * e_rest[h], v_new[h])
    for h in heads:
        on = o[h] * lax.rsqrt(jnp.mean(o[h] * o[h], axis=-1, keepdims=True) + RMS_EPS) * ng_ref[...]
        y_ref[:, h * LANES:(h + 1) * LANES] = (on * _silu(zb_ref[:, h * LANES:(h + 1) * LANES])).astype(y_ref.dtype)

    @pl.when(c == pl.num_programs(1) - 1)
    def _():
        sfin_ref[...] = s_scr[...]


def _gdn_prompt(qkv_n, ba, z_main, zb_col_block, a_log, dt_bias, norm_g, batch, seq, gh, out_dtype):
    assert seq % GDN_CHUNK == 0 and GDN_CHUNK == GDN_HEAD_DIM == LANES
    gw = gh * GDN_HEAD_DIM
    nc = seq // GDN_CHUNK
    return pl.pallas_call(
        functools.partial(_gdn_kernel, gh=gh),
        grid=(batch, nc),
        in_specs=[pl.BlockSpec((GDN_CHUNK, 3 * gw), lambda b, c: (b * nc + c, 0)),
                  pl.BlockSpec((GDN_CHUNK, 2 * gh), lambda b, c: (b * nc + c, 0)),
                  pl.BlockSpec((GDN_CHUNK, gw), lambda b, c: (b * nc + c, zb_col_block)),
                  pl.BlockSpec((1, gh), lambda b, c: (0, 0)),
                  pl.BlockSpec((1, gh), lambda b, c: (0, 0)),
                  pl.BlockSpec((1, GDN_HEAD_DIM), lambda b, c: (0, 0))],
        out_specs=[pl.BlockSpec((GDN_CHUNK, gw), lambda b, c: (b * nc + c, 0)),
                   pl.BlockSpec((None, gh, GDN_HEAD_DIM, GDN_HEAD_DIM), lambda b, c: (b, 0, 0, 0))],
        out_shape=[jax.ShapeDtypeStruct((batch * seq, gw), out_dtype),
                   jax.ShapeDtypeStruct((batch, gh, GDN_HEAD_DIM, GDN_HEAD_DIM), F32)],
        scratch_shapes=[pltpu.VMEM((gh, GDN_HEAD_DIM, GDN_HEAD_DIM), F32)],
        compiler_params=_cparams("parallel", "arbitrary"),
        name="gdn_prompt",
    )(qkv_n, ba, z_main, a_log.reshape(1, gh), dt_bias.reshape(1, gh), norm_g.reshape(1, GDN_HEAD_DIM))


def _gdns_kernel(sc_ref, zrow_ref, zb_ref, ba_ref, cw_ref, alog_ref, dtb_ref, ng_ref, s_ref,
                 y_ref, cnew_ref, snew_ref, *, gh, width):
    rows = [sc_ref[i:i + 1, :] for i in range(width - 1)] + [zrow_ref[...]]
    acc = rows[0] * cw_ref[0:1, :]
    for i in range(1, width):
        acc = acc + rows[i] * cw_ref[i:i + 1, :]
    for i in range(width - 1):
        cnew_ref[i:i + 1, :] = rows[i + 1]
    heads = _conv_heads(_silu(acc), gh)
    n = GDN_HEAD_DIM
    eye = lax.broadcasted_iota(jnp.int32, (n, n), 0) == lax.broadcasted_iota(jnp.int32, (n, n), 1)

    def column(r):
        return jnp.sum(jnp.where(eye, jnp.broadcast_to(r, (n, n)), 0.0), axis=1, keepdims=True)

    ba = ba_ref[...]
    for h in range(gh):
        q, k, v = heads[h], heads[gh + h], heads[2 * gh + h]
        beta = _sigmoid(ba[:, h:h + 1])
        g = -jnp.exp(alog_ref[0:1, h:h + 1]) * _softplus(ba[:, gh + h:gh + h + 1] + dtb_ref[0:1, h:h + 1])
        eg = jnp.exp(g)
        kcol = column(k)
        s = s_ref[h]
        ks = jnp.sum(kcol * s, axis=0, keepdims=True)
        v_new = beta * v - (beta * eg) * ks
        s_new = s * eg + kcol * v_new
        snew_ref[h] = s_new
        o = jnp.sum(column(q) * s_new, axis=0, keepdims=True)
        o = o * lax.rsqrt(jnp.mean(o * o, axis=-1, keepdims=True) + RMS_EPS) * ng_ref[...]
        y_ref[:, h * LANES:(h + 1) * LANES] = o * _silu(zb_ref[:, h * LANES:(h + 1) * LANES])


def _gdn_sample(state_conv, z3, ba3, conv_w, a_log, dt_bias, norm_g, state_ssm, qkvb_col_block, zb_col_block, gh):
    bsz, wm1, cc = state_conv.shape
    width = wm1 + 1
    gw = gh * GDN_HEAD_DIM
    small = lambda shape: pl.BlockSpec(shape, lambda b: (0,) * len(shape))
    return pl.pallas_call(
        functools.partial(_gdns_kernel, gh=gh, width=width),
        grid=(bsz,),
        in_specs=[pl.BlockSpec((None, wm1, cc), lambda b: (b, 0, 0)),
                  pl.BlockSpec((None, 1, cc), lambda b: (b, 0, qkvb_col_block)),
                  pl.BlockSpec((None, 1, gw), lambda b: (b, 0, zb_col_block)),
                  pl.BlockSpec((None, 1, 2 * gh), lambda b: (b, 0, 0)),
                  small((width, cc)), small((1, gh)), small((1, gh)), small((1, GDN_HEAD_DIM)),
                  pl.BlockSpec((None, gh, GDN_HEAD_DIM, GDN_HEAD_DIM), lambda b: (b, 0, 0, 0))],
        out_specs=[pl.BlockSpec((None, 1, gw), lambda b: (b, 0, 0)),
                   pl.BlockSpec((None, wm1, cc), lambda b: (b, 0, 0)),
                   pl.BlockSpec((None, gh, GDN_HEAD_DIM, GDN_HEAD_DIM), lambda b: (b, 0, 0, 0))],
        out_shape=[jax.ShapeDtypeStruct((bsz, 1, gw), F32),
                   jax.ShapeDtypeStruct((bsz, wm1, cc), F32),
                   jax.ShapeDtypeStruct((bsz, gh, GDN_HEAD_DIM, GDN_HEAD_DIM), F32)],
        compiler_params=_cparams("parallel"),
        name="gdn_sample",
    )(state_conv, z3, z3, ba3, conv_w, a_log.reshape(1, gh), dt_bias.reshape(1, gh),
      norm_g.reshape(1, GDN_HEAD_DIM), state_ssm)


def _mixa_kernel(x_ref, ya_ref, yb_ref, wga_ref, wgb_ref, wpa_ref, wpb_ref, o_ref, *, hp):
    x = x_ref[...]
    ga = _dot_nt(x, wga_ref[...], hp)
    gb = _dot_nt(x, wgb_ref[...], hp)
    pa = _dot(ya_ref[...], wpa_ref[...], hp)
    pb = _dot(yb_ref[...], wpb_ref[...], hp)
    o_ref[...] = (_sigmoid(ga) * pa + _sigmoid(gb) * pb).astype(o_ref.dtype)


def _mix_merge(x, ya, yb, w_gates, w_pa, w_pb, tm, tn, hp):
    m, dm = x.shape
    nj = dm // tn
    return pl.pallas_call(
        functools.partial(_mixa_kernel, hp=hp),
        grid=(m // tm, nj),
        in_specs=[pl.BlockSpec((tm, dm), lambda i, j: (i, 0)),
                  pl.BlockSpec((tm, ya.shape[1]), lambda i, j: (i, 0)),
                  pl.BlockSpec((tm, yb.shape[1]), lambda i, j: (i, 0)),
                  pl.BlockSpec((tn, dm), lambda i, j: (j, 0)),
                  pl.BlockSpec((tn, dm), lambda i, j: (nj + j, 0)),
                  pl.BlockSpec((w_pa.shape[0], tn), lambda i, j: (0, j)),
                  pl.BlockSpec((w_pb.shape[0], tn), lambda i, j: (0, j))],
        out_specs=pl.BlockSpec((tm, tn), lambda i, j: (i, j)),
        out_shape=jax.ShapeDtypeStruct((m, dm), F32 if hp else BF16),
        compiler_params=_cparams("parallel", "arbitrary"),
        name="mix_merge",
    )(x, ya, yb, w_gates, w_gates, w_pa, w_pb)


def _route(logits):
    lane = lax.broadcasted_iota(jnp.int32, logits.shape, 1).astype(F32)
    is_group = (lane >= N_EXPERTS) & (lane < N_EXPERTS + N_GROUPS)
    gl = jnp.where(is_group, logits, NEG_BIG)
    gmax = jnp.max(gl, axis=1, keepdims=True)
    gsel = jnp.min(jnp.where(gl == gmax, lane, 1e9), axis=1, keepdims=True) - N_EXPERTS
    p_group = 1.0 / jnp.sum(jnp.where(is_group, jnp.exp(gl - gmax), 0.0), axis=1, keepdims=True)
    in_group = (lane >= gsel * EXPERTS_PER_GROUP) & (lane < (gsel + 1.0) * EXPERTS_PER_GROUP)
    el = jnp.where(in_group, logits, NEG_BIG)
    m1 = jnp.max(el, axis=1, keepdims=True)
    i1 = jnp.min(jnp.where(el == m1, lane, 1e9), axis=1, keepdims=True)
    el2 = jnp.where(lane == i1, NEG_BIG, el)
    m2 = jnp.max(el2, axis=1, keepdims=True)
    i2 = jnp.min(jnp.where(el2 == m2, lane, 1e9), axis=1, keepdims=True)
    e2 = jnp.exp(m2 - m1)
    g1 = p_group / (1.0 + e2)
    g2 = p_group * e2 / (1.0 + e2)
    return jnp.where(lane == 0.0, i1, jnp.where(lane == 1.0, i2, jnp.where(lane == 2.0, g1,
                                                                            jnp.where(lane == 3.0, g2, 0.0))))


def _mixb_kernel(x_ref, m_ref, wout_ref, g_ref, b_ref, rw_ref, rb_ref, cnt0_ref,
                 hlo_ref, hhi_ref, route_ref, cnt_ref, cnt_scr, *, alpha, hp):
    @pl.when(pl.program_id(0) == 0)
    def _():
        cnt_scr[...] = cnt0_ref[...]

    t = alpha * x_ref[...] + _dot(m_ref[...], wout_ref[...], hp)
    h = _layer_norm(t, g_ref[...], b_ref[...])
    half = h.shape[1] // 2
    hlo_ref[...] = h[:, :half]
    hhi_ref[...] = h[:, half:]
    if hp:
        logits = _dot(h, rw_ref[...], True) + rb_ref[...]
    else:
        logits = _dot3(h, rw_ref[...]) + rb_ref[...]
    route = _route(logits)
    tm = h.shape[0]
    lane = lax.broadcasted_iota(jnp.int32, (tm, LANES), 1).astype(F32)
    first = jnp.where(lane == route[:, 0:1], 1.0, 0.0)
    second = jnp.where(lane == route[:, 1:2], 1.0, 0.0)
    earlier = jnp.where(lax.broadcasted_iota(jnp.int32, (tm, tm), 0) > lax.broadcasted_iota(jnp.int32, (tm, tm), 1),
                        1.0, 0.0).astype(BF16)
    n_first = jnp.sum(first, axis=0, keepdims=True)
    rank1 = jnp.dot(earlier, first.astype(BF16), preferred_element_type=F32) + cnt_scr[...]
    rank2 = jnp.dot(earlier, second.astype(BF16), preferred_element_type=F32) + cnt_scr[...] + n_first
    pos1 = jnp.sum(first * rank1, axis=1, keepdims=True)
    pos2 = jnp.sum(second * rank2, axis=1, keepdims=True)
    cnt_scr[...] = cnt_scr[...] + n_first + jnp.sum(second, axis=0, keepdims=True)
    cnt_ref[...] = cnt_scr[...]
    route_ref[...] = jnp.where(lane == 4.0, pos1, jnp.where(lane == 5.0, pos2, route))


def _mix_out(x, merged, w_out, ln_g, ln_b, router_w, router_b, counts0, alpha, tm, hp):
    m, dm = x.shape
    const = lambda shape: pl.BlockSpec(shape, lambda i: (0,) * len(shape))
    return pl.pallas_call(
        functools.partial(_mixb_kernel, alpha=alpha, hp=hp),
        grid=(m // tm,),
        in_specs=[pl.BlockSpec((tm, dm), lambda i: (i, 0)),
                  pl.BlockSpec((tm, dm), lambda i: (i, 0)),
                  const((dm, dm)), const((1, dm)), const((1, dm)), const((dm, LANES)), const((1, LANES)),
                  const((1, LANES))],
        out_specs=[pl.BlockSpec((tm, dm // 2), lambda i: (i, 0)),
                   pl.BlockSpec((tm, dm // 2), lambda i: (i, 0)),
                   pl.BlockSpec((tm, LANES), lambda i: (i, 0)),
                   const((1, LANES))],
        out_shape=[jax.ShapeDtypeStruct((m, dm // 2), F32), jax.ShapeDtypeStruct((m, dm // 2), F32),
                   jax.ShapeDtypeStruct((m, LANES), F32), jax.ShapeDtypeStruct((1, LANES), F32)],
        scratch_shapes=[pltpu.VMEM((1, LANES), F32)],
        compiler_params=_cparams("arbitrary"),
        name="mix_out_ln_route",
    )(x, merged, w_out, ln_g.reshape(1, dm), ln_b.reshape(1, dm), router_w, router_b, counts0)


MOE_TILE = 128
ROW_UNROLL = 8
MOE_COL_SPLIT = 4


def _moe_plan(route, counts, n_tok):
    experts = jnp.arange(N_EXPERTS, dtype=jnp.int32)
    padded = (counts[0, :N_EXPERTS].astype(jnp.int32) + MOE_TILE - 1) // MOE_TILE * MOE_TILE
    ends = jnp.cumsum(padded)
    starts = ends - padded
    eid = route[:, :2].astype(jnp.int32)
    pos = route[:, 4:6].astype(jnp.int32)
    seg = jnp.sum(jnp.where(eid[..., None] == experts, starts, 0), axis=-1)
    dest = (seg + pos).reshape(-1)
    n_rows = (2 * n_tok + MOE_TILE - 1) // MOE_TILE * MOE_TILE + N_EXPERTS * MOE_TILE
    tile_start = jnp.arange(n_rows // MOE_TILE, dtype=jnp.int32) * MOE_TILE
    tile_expert = jnp.minimum(jnp.sum((ends[None, :] <= tile_start[:, None]).astype(jnp.int32), axis=1),
                              N_EXPERTS - 1)
    n_valid = (ends[-1] // MOE_TILE).astype(jnp.int32).reshape(1)
    pad_lo = jnp.concatenate([starts + counts[0, :N_EXPERTS].astype(jnp.int32), ends[-1:]])
    pad_hi = jnp.concatenate([ends, jnp.full((1,), n_rows, jnp.int32)])
    return dest, tile_expert, n_valid, n_rows, pad_lo, pad_hi


def _invert_kernel(dest_ref, pad_lo_ref, pad_hi_ref, src_ref, *, n_tok):
    def clear_range(e, carry):
        def clear(r, inner):
            src_ref[r] = 0
            return inner

        lax.fori_loop(pad_lo_ref[e], pad_hi_ref[e], clear, 0)
        return carry

    def fill(t, carry):
        src_ref[dest_ref[2 * t]] = t
        src_ref[dest_ref[2 * t + 1]] = t
        return carry

    lax.fori_loop(0, pad_lo_ref.shape[0], clear_range, 0)
    lax.fori_loop(0, n_tok, fill, 0, unroll=ROW_UNROLL)


def _invert(dest, pad_lo, pad_hi, n_rows):
    grid_spec = pltpu.PrefetchScalarGridSpec(
        num_scalar_prefetch=3, grid=(1,), in_specs=[],
        out_specs=pl.BlockSpec(memory_space=pltpu.SMEM))
    return pl.pallas_call(
        functools.partial(_invert_kernel, n_tok=dest.shape[0] // 2),
        grid_spec=grid_spec,
        out_shape=jax.ShapeDtypeStruct((n_rows,), jnp.int32),
        compiler_params=_cparams("arbitrary"),
        name="moe_invert",
    )(dest, pad_lo, pad_hi)


def _gather_kernel(src_ref, x_ref, o_ref, rows_scr, *, rows):
    base = pl.program_id(0) * rows

    def body(r, carry):
        rows_scr[pl.ds(r, 1), :] = x_ref[pl.ds(src_ref[base + r], 1), :]
        return carry

    lax.fori_loop(0, rows, body, 0, unroll=ROW_UNROLL)
    o_ref[...] = rows_scr[...].astype(o_ref.dtype)


def _gather_rows(src, x):
    n_rows = src.shape[0]
    rows = _pick(n_rows, (256, 128))
    grid_spec = pltpu.PrefetchScalarGridSpec(
        num_scalar_prefetch=1,
        grid=(n_rows // rows,),
        in_specs=[pl.BlockSpec(memory_space=pltpu.VMEM)],
        out_specs=pl.BlockSpec((rows, x.shape[1]), lambda i, src: (i, 0)),
        scratch_shapes=[pltpu.VMEM((rows, x.shape[1]), F32)],
    )
    return pl.pallas_call(
        functools.partial(_gather_kernel, rows=rows),
        grid_spec=grid_spec,
        out_shape=jax.ShapeDtypeStruct((n_rows, x.shape[1]), BF16),
        compiler_params=_cparams("arbitrary"),
        name="moe_gather",
    )(src, x)


def _experts_kernel(te_ref, nv_ref, xlo_ref, xhi_ref, wg_ref, wu_ref, wd_ref, *refs):
    y_refs = refs[:MOE_COL_SPLIT]
    wg_scr, wu_scr, wd_scr = refs[MOE_COL_SPLIT:]
    i = pl.program_id(0)

    @pl.when((i == 0) | (te_ref[i] != te_ref[jnp.maximum(i - 1, 0)]))
    def _():
        wg_scr[...] = wg_ref[...].astype(BF16)
        wu_scr[...] = wu_ref[...].astype(BF16)
        wd_scr[...] = wd_ref[...].astype(BF16)

    @pl.when(i < nv_ref[0])
    def _():
        x = jnp.concatenate([xlo_ref[...], xhi_ref[...]], axis=1)
        a = jnp.dot(x, wg_scr[...], preferred_element_type=F32)
        u = jnp.dot(x, wu_scr[...], preferred_element_type=F32)
        y = jnp.dot((_silu(a) * u).astype(BF16), wd_scr[...], preferred_element_type=F32)
        q = y.shape[1] // MOE_COL_SPLIT
        for c, y_ref in enumerate(y_refs):
            y_ref[...] = y[:, c * q:(c + 1) * q]

    @pl.when(i >= nv_ref[0])
    def _():
        for y_ref in y_refs:
            y_ref[...] = jnp.zeros_like(y_ref)


def _experts(x_lo, x_hi, tile_expert, n_valid, w_gate, w_up, w_down):
    n_rows, half = x_lo.shape
    dm = 2 * half
    _, _, ff = w_gate.shape
    q = dm // MOE_COL_SPLIT
    grid_spec = pltpu.PrefetchScalarGridSpec(
        num_scalar_prefetch=2,
        grid=(n_rows // MOE_TILE,),
        in_specs=[pl.BlockSpec((MOE_TILE, half), lambda i, te, nv: (i, 0)),
                  pl.BlockSpec((MOE_TILE, half), lambda i, te, nv: (i, 0)),
                  pl.BlockSpec((None, dm, ff), lambda i, te, nv: (te[i], 0, 0)),
                  pl.BlockSpec((None, dm, ff), lambda i, te, nv: (te[i], 0, 0)),
                  pl.BlockSpec((None, ff, dm), lambda i, te, nv: (te[i], 0, 0))],
        out_specs=[pl.BlockSpec((MOE_TILE, q), lambda i, te, nv: (i, 0))] * MOE_COL_SPLIT,
        scratch_shapes=[pltpu.VMEM((dm, ff), BF16), pltpu.VMEM((dm, ff), BF16), pltpu.VMEM((ff, dm), BF16)],
    )
    return pl.pallas_call(
        _experts_kernel,
        grid_spec=grid_spec,
        out_shape=[jax.ShapeDtypeStruct((n_rows, q), F32)] * MOE_COL_SPLIT,
        compiler_params=_cparams("arbitrary"),
        name="moe_experts",
    )(tile_expert, n_valid, x_lo, x_hi, w_gate, w_up, w_down)


def _combine_kernel(dest_ref, gate_ref, y_ref, o_ref, *, rows):
    base = pl.program_id(0) * rows

    def body(t, carry):
        a = 2 * (base + t)
        first = y_ref[pl.ds(dest_ref[a], 1), :]
        second = y_ref[pl.ds(dest_ref[a + 1], 1), :]
        o_ref[pl.ds(t, 1), :] = gate_ref[a] * first + gate_ref[a + 1] * second
        return carry

    lax.fori_loop(0, rows, body, 0, unroll=ROW_UNROLL)


def _combine(dest, gates, y_slab):
    n_tok = dest.shape[0] // 2
    rows = _pick(n_tok, (256, 128, 64, 32))
    width = y_slab.shape[1]
    grid_spec = pltpu.PrefetchScalarGridSpec(
        num_scalar_prefetch=2,
        grid=(n_tok // rows,),
        in_specs=[pl.BlockSpec(memory_space=pltpu.VMEM)],
        out_specs=pl.BlockSpec((rows, width), lambda i, dest, gates: (i, 0)),
    )
    return pl.pallas_call(
        functools.partial(_combine_kernel, rows=rows),
        grid_spec=grid_spec,
        out_shape=jax.ShapeDtypeStruct((n_tok, width), F32),
        compiler_params=_cparams("arbitrary"),
        name="moe_combine",
    )(dest, gates, y_slab)


def _ln2_kernel(hlo_ref, hhi_ref, *refs, alpha):
    moe_refs = refs[:MOE_COL_SPLIT]
    g_ref, b_ref, o_ref = refs[MOE_COL_SPLIT:]
    h = jnp.concatenate([hlo_ref[...], hhi_ref[...]], axis=1)
    moe = jnp.concatenate([r[...] for r in moe_refs], axis=1)
    o_ref[...] = _layer_norm(alpha * h + moe, g_ref[...], b_ref[...])


def _ln2(h_lo, h_hi, moe_slabs, row_block0, ln_g, ln_b, alpha, tm):
    m, half = h_lo.shape
    dm = 2 * half
    q = dm // MOE_COL_SPLIT
    return pl.pallas_call(
        functools.partial(_ln2_kernel, alpha=alpha),
        grid=(m // tm,),
        in_specs=[pl.BlockSpec((tm, half), lambda i: (i, 0)), pl.BlockSpec((tm, half), lambda i: (i, 0))]
        + [pl.BlockSpec((tm, q), lambda i: (i + row_block0, 0))] * MOE_COL_SPLIT
        + [pl.BlockSpec((1, dm), lambda i: (0, 0)), pl.BlockSpec((1, dm), lambda i: (0, 0))],
        out_specs=pl.BlockSpec((tm, dm), lambda i: (i, 0)),
        out_shape=jax.ShapeDtypeStruct((m, dm), F32),
        compiler_params=_cparams("parallel"),
        name="moe_ln2",
    )(h_lo, h_hi, *moe_slabs, ln_g.reshape(1, dm), ln_b.reshape(1, dm))


def _moe(h_p, route_p, h_s, route_s, counts, w_gate, w_up, w_down, ln_g, ln_b, alpha):
    m_p, m_s = h_p[0].shape[0], h_s[0].shape[0]
    n_tok = m_p + m_s
    assert m_p % m_s == 0 and m_s % 8 == 0
    n_pad = (n_tok + 255) // 256 * 256
    route = jnp.concatenate([route_p, route_s], axis=0)
    dest, tile_expert, n_valid, n_rows, pad_lo, pad_hi = _moe_plan(route, counts, n_tok)
    src = _invert(dest, pad_lo, pad_hi, n_rows)
    x_lo = _gather_rows(src, jnp.concatenate([h_p[0], h_s[0]], axis=0))
    x_hi = _gather_rows(src, jnp.concatenate([h_p[1], h_s[1]], axis=0))
    y_slabs = _experts(x_lo, x_hi, tile_expert, n_valid, w_gate, w_up, w_down)
    gates_pad = jnp.concatenate([route[:, 2:4].reshape(-1), jnp.zeros((2 * (n_pad - n_tok),), F32)])
    dest_pad = jnp.concatenate([dest, jnp.zeros((2 * (n_pad - n_tok),), jnp.int32)])
    moe_slabs = [_combine(dest_pad, gates_pad, y) for y in y_slabs]
    out_p = _ln2(h_p[0], h_p[1], moe_slabs, 0, ln_g, ln_b, alpha, _pick(m_p, (256, 128, 64, 32, 16, 8)))
    out_s = _ln2(h_s[0], h_s[1], moe_slabs, m_p // m_s, ln_g, ln_b, alpha, m_s)
    return out_p, out_s


def _layer(hp_x, hs_x, cache_k, cache_v, page_table, state_conv, state_ssm, layer, w, alpha):
    batch, seq, dm = hp_x.shape
    dec_batch, dec_seq, _ = hs_x.shape
    assert dec_seq == 1
    sb_heads = w["sb_bias"].shape[0]
    gh = w["a_log"].shape[0]
    sbw = sb_heads * SB_HEAD_DIM
    gw = gh * GDN_HEAD_DIM
    cc = 3 * gw
    n_main = 3 * sbw + cc + gw
    assert sbw % LANES == 0 and 3 * sbw % cc == 0 and n_main % gw == 0 and sbw == gw
    qkvb_block = 3 * sbw // cc
    zb_block = (3 * sbw + cc) // gw
    w_t = jnp.transpose(w["w_in"])
    w_ba = w_t[n_main:n_main + 2 * gh]
    w_gates = w_t[n_main + 2 * gh:]
    router_w = jnp.concatenate(
        [jnp.transpose(w["router_expert"], (1, 0, 2)).reshape(dm, N_EXPERTS), w["router_group"],
         jnp.zeros((dm, LANES - N_EXPERTS - N_GROUPS), F32)], axis=1)
    router_b = jnp.concatenate(
        [w["router_expert_b"].reshape(N_EXPERTS), w["router_group_b"],
         jnp.zeros((LANES - N_EXPERTS - N_GROUPS,), F32)]).reshape(1, LANES)

    m = batch * seq
    xp = hp_x.reshape(m, dm)
    tm = _pick(m, (1024, 512, 256, 128, 64, 32, 16, 8))
    z_main = _matmul(xp, w_t, n_main, tm, _pick(n_main, (1024, 512, 256, 128)), False)
    ba = _matmul(xp, w_ba, 2 * gh, tm, 2 * gh, False)
    y_a, k_t, v_t = _sb_prompt(z_main, w["sb_bias"], batch, seq, sbw, BF16)
    qkv_n = _gdn_prep(z_main, w["conv_w"], batch, seq, qkvb_block, gh)
    y_b, ssm_p = _gdn_prompt(qkv_n, ba, z_main, zb_block, w["a_log"], w["dt_bias"], w["gdn_norm_g"],
                             batch, seq, gh, BF16)
    merged = _mix_merge(xp, y_a, y_b, w_gates.astype(BF16), w["w_proj_a"].astype(BF16), w["w_proj_b"].astype(BF16),
                        tm, _pick(dm, (512, 256, 128)), False)
    h1_lo, h1_hi, route, counts = _mix_out(xp, merged, w["w_out"].astype(BF16), w["ln1_g"], w["ln1_b"], router_w, router_b,
                                      jnp.zeros((1, LANES), F32), alpha, _pick(m, (256, 128, 64, 32, 16, 8)), False)
    k_p = jnp.transpose(k_t.reshape(batch, sb_heads, SB_HEAD_DIM, seq), (0, 3, 1, 2))
    v_p = jnp.transpose(v_t.reshape(batch, sb_heads, SB_HEAD_DIM, seq), (0, 3, 1, 2))
    conv_p = z_main.reshape(batch, seq, n_main)[:, seq - (w["conv_w"].shape[0] - 1):, 3 * sbw:3 * sbw + cc]

    xs = hs_x.reshape(dec_batch, dm)
    zs = _matmul(xs, w_t, n_main, dec_batch, _pick(n_main, (1024, 512, 256, 128)), True)
    ba_s = _matmul(xs, w_ba, 2 * gh, dec_batch, 2 * gh, True)
    q_s = zs[:, :sbw].reshape(dec_batch, sb_heads, SB_HEAD_DIM)
    k_s = zs[:, sbw:2 * sbw].reshape(dec_batch, sb_heads, SB_HEAD_DIM)
    v_s = zs[:, 2 * sbw:3 * sbw].reshape(dec_batch, sb_heads, SB_HEAD_DIM)
    ya_s = _sb_sample(q_s, k_s, v_s, w["sb_bias"], cache_k, cache_v, layer, page_table).reshape(dec_batch, sbw)
    yb_s, conv_s, ssm_s = _gdn_sample(state_conv, zs.reshape(dec_batch, 1, n_main), ba_s.reshape(dec_batch, 1, 2 * gh),
                                      w["conv_w"], w["a_log"], w["dt_bias"], w["gdn_norm_g"], state_ssm,
                                      qkvb_block, zb_block, gh)
    merged_s = _mix_merge(xs, ya_s, yb_s.reshape(dec_batch, gw), w_gates, w["w_proj_a"], w["w_proj_b"],
                          dec_batch, _pick(dm, (512, 256, 128)), True)
    h1s_lo, h1s_hi, route_s, counts = _mix_out(xs, merged_s, w["w_out"], w["ln1_g"], w["ln1_b"], router_w, router_b,
                                               counts, alpha, dec_batch, True)

    out_p, out_s = _moe((h1_lo, h1_hi), route, (h1s_lo, h1s_hi), route_s, counts, w["w_gate"], w["w_up"],
                        w["w_down"], w["ln2_g"], w["ln2_b"], alpha)

    return (out_p.reshape(batch, seq, dm), out_s.reshape(dec_batch, 1, dm), k_p, v_p,
            k_s.reshape(dec_batch, 1, sb_heads, SB_HEAD_DIM), v_s.reshape(dec_batch, 1, sb_heads, SB_HEAD_DIM),
            conv_p, conv_s, ssm_p, ssm_s)


def kernel(x_prompt, x_sample, cache_k, cache_v, page_table, state_conv, state_ssm, w_in, sb_bias, conv_w, a_log,
           dt_bias, gdn_norm_g, w_proj_a, w_proj_b, w_out, ln1_g, ln1_b, router_group, router_group_b,
           router_expert, router_expert_b, w_gate, w_up, w_down, ln2_g, ln2_b):
    depth = w_in.shape[0]
    alpha = (2.0 * depth) ** 0.25
    stacked = dict(w_in=w_in, sb_bias=sb_bias, conv_w=conv_w, a_log=a_log, dt_bias=dt_bias, gdn_norm_g=gdn_norm_g,
                   w_proj_a=w_proj_a, w_proj_b=w_proj_b, w_out=w_out, ln1_g=ln1_g, ln1_b=ln1_b,
                   router_group=router_group, router_group_b=router_group_b, router_expert=router_expert,
                   router_expert_b=router_expert_b, w_gate=w_gate, w_up=w_up, w_down=w_down, ln2_g=ln2_g, ln2_b=ln2_b)
    hp_x, hs_x = x_prompt, x_sample
    per_layer = []
    for layer in range(depth):
        w = {name: t[layer] for name, t in stacked.items()}
        outs = _layer(hp_x, hs_x, cache_k, cache_v, page_table, state_conv[layer], state_ssm[layer], layer, w, alpha)
        hp_x, hs_x = outs[0], outs[1]
        per_layer.append(outs[2:])
    return (hp_x, hs_x) + tuple(jnp.stack([o[i] for o in per_layer]) for i in range(8))
```
